```python
import math
import jax, jax.numpy as jnp
from jax import lax
import numpy as np

D_MODEL = 2048
BATCH = 8
SEQ = 2048
DEPTH = 1

MLA_HEADS = D_MODEL // 256
MLA_Q_RANK = D_MODEL // 4
MLA_KV_RANK = D_MODEL // 8
MLA_NOPE_DIM = 128
MLA_ROPE_DIM = 64
MLA_V_DIM = 128
MLA_SCALE = (MLA_NOPE_DIM + MLA_ROPE_DIM) ** -0.5
DIFF_HEADS = D_MODEL // 256
DIFF_QK_DIM = 64
DIFF_V_DIM = 2 * DIFF_QK_DIM
DIFF_SCALE = DIFF_QK_DIM ** -0.5
MIX_WIDTH = MLA_HEADS * MLA_V_DIM + DIFF_HEADS * DIFF_V_DIM
IN_Q_LAT = MLA_Q_RANK
IN_KV_LAT = MLA_KV_RANK
IN_K_ROPE = MLA_ROPE_DIM
IN_DIFF_Q = DIFF_HEADS * 2 * DIFF_QK_DIM
IN_DIFF_K = DIFF_HEADS * 2 * DIFF_QK_DIM
IN_DIFF_V = DIFF_HEADS * DIFF_V_DIM
IN_WIDTH = IN_Q_LAT + IN_KV_LAT + IN_K_ROPE + IN_DIFF_Q + IN_DIFF_K + IN_DIFF_V
SPLIT_POINTS = (IN_Q_LAT,
                IN_Q_LAT + IN_KV_LAT,
                IN_Q_LAT + IN_KV_LAT + IN_K_ROPE,
                IN_Q_LAT + IN_KV_LAT + IN_K_ROPE + IN_DIFF_Q,
                IN_Q_LAT + IN_KV_LAT + IN_K_ROPE + IN_DIFF_Q + IN_DIFF_K)
D_FF = 5632
CONV_WIDTH = 3
REL_BUCKETS = 32
REL_MAX_DIST = 128
ROPE_THETA = 10000.0
Q_BLOCK = 128
NORM_EPS = 1e-6
N_MOD = 6

kernel_name = "hybrid_mla_diffattn_convffn_encoder"


def rms_norm(x, w):
    xf = x.astype(jnp.float32)
    y = xf * lax.rsqrt(jnp.mean(xf * xf, axis=-1, keepdims=True) + NORM_EPS)
    return (y * w.astype(jnp.float32)).astype(x.dtype)


def modulate(h, shift, scale):
    return h * (1 + scale[:, None, :]) + shift[:, None, :]


def rope_tables(positions):
    inv = 1.0 / (ROPE_THETA ** (jnp.arange(0, MLA_ROPE_DIM, 2, dtype=jnp.float32) / MLA_ROPE_DIM))
    ang = positions.astype(jnp.float32)[..., None] * inv
    return jnp.cos(ang), jnp.sin(ang)


def apply_rope(t, cos, sin):
    half = t.shape[-1] // 2
    t1, t2 = t[..., :half], t[..., half:]
    cos = cos.astype(t.dtype)
    sin = sin.astype(t.dtype)
    return jnp.concatenate([t1 * cos - t2 * sin, t1 * sin + t2 * cos], axis=-1)


def t5_bucket(rel):
    half = REL_BUCKETS // 2
    max_exact = half // 2
    ret = jnp.where(rel > 0, half, 0)
    n = jnp.abs(rel)
    nf = jnp.maximum(n, 1).astype(jnp.float32)
    large = max_exact + (jnp.log(nf / max_exact) / math.log(REL_MAX_DIST / max_exact)
                         * (half - max_exact)).astype(jnp.int32)
    large = jnp.minimum(large, half - 1)
    return ret + jnp.where(n < max_exact, n, large)


def to_blocks(t):
    b, s = t.shape[0], t.shape[1]
    return jnp.moveaxis(t.reshape(b, s // Q_BLOCK, Q_BLOCK, *t.shape[2:]), 1, 0)


def from_blocks(t):
    nb, b = t.shape[0], t.shape[1]
    t = jnp.moveaxis(t, 0, 1)
    return t.reshape(b, nb * Q_BLOCK, *t.shape[3:])


def hybrid_mixer(h, positions, cos, sin, rel_bias, w_in, q_norm, w_uq, kv_norm, w_ukv,
                 lq1, lk1, lq2, lk2, subln, w_o, lambda_init):
    B, S, _ = h.shape
    proj = h @ w_in
    q_lat, kv_lat, k_rope, dq, dk, dv = jnp.split(proj, SPLIT_POINTS, axis=-1)

    q = (rms_norm(q_lat, q_norm) @ w_uq).reshape(B, S, MLA_HEADS, MLA_NOPE_DIM + MLA_ROPE_DIM)
    q_nope = q[..., :MLA_NOPE_DIM]
    q_rope = apply_rope(q[..., MLA_NOPE_DIM:], cos[:, :, None, :], sin[:, :, None, :])
    kv = (rms_norm(kv_lat, kv_norm) @ w_ukv).reshape(B, S, MLA_HEADS, MLA_NOPE_DIM + MLA_V_DIM)
    k_nope = kv[..., :MLA_NOPE_DIM]
    v_mla = kv[..., MLA_NOPE_DIM:]
    k_rope = apply_rope(k_rope, cos, sin)

    dq = dq.reshape(B, S, DIFF_HEADS, 2, DIFF_QK_DIM)
    dk = dk.reshape(B, S, DIFF_HEADS, 2, DIFF_QK_DIM)
    dv = dv.reshape(B, S, DIFF_HEADS, DIFF_V_DIM)
    lam = (jnp.exp(jnp.sum(lq1.astype(jnp.float32) * lk1.astype(jnp.float32)))
           - jnp.exp(jnp.sum(lq2.astype(jnp.float32) * lk2.astype(jnp.float32)))
           + lambda_init)
    bias_tab = rel_bias.astype(jnp.float32)

    def block(args):
        qn, qr, dqb, pq = args
        s = (jnp.einsum('bqhd,bkhd->bhqk', qn, k_nope)
             + jnp.einsum('bqhr,bkr->bhqk', qr, k_rope)) * MLA_SCALE
        p = jax.nn.softmax(s.astype(jnp.float32), axis=-1).astype(v_mla.dtype)
        o_mla = jnp.einsum('bhqk,bkhd->bqhd', p, v_mla)

        rel = positions[:, None, :] - pq[:, :, None]
        bias = jnp.take(bias_tab, t5_bucket(rel), axis=0)
        bias = jnp.transpose(bias, (0, 3, 1, 2))
        s1 = jnp.einsum('bqhd,bkhd->bhqk', dqb[..., 0, :], dk[..., 0, :]).astype(jnp.float32) * DIFF_SCALE + bias
        s2 = jnp.einsum('bqhd,bkhd->bhqk', dqb[..., 1, :], dk[..., 1, :]).astype(jnp.float32) * DIFF_SCALE + bias
        a = jax.nn.softmax(s1, axis=-1) - lam * jax.nn.softmax(s2, axis=-1)
        o_diff = jnp.einsum('bhqk,bkhd->bqhd', a.astype(dv.dtype), dv)
        return o_mla, o_diff

    o_mla, o_diff = lax.map(block, (to_blocks(q_nope), to_blocks(q_rope),
                                    to_blocks(dq), to_blocks(positions)))
    o_mla = from_blocks(o_mla)
    o_diff = from_blocks(o_diff)
    o_diff = rms_norm(o_diff, subln) * (1.0 - lambda_init)
    out = jnp.concatenate([o_mla.reshape(B, S, MLA_HEADS * MLA_V_DIM),
                           o_diff.reshape(B, S, DIFF_HEADS * DIFF_V_DIM)], axis=-1)
    return out @ w_o


def conv_geglu_ffn(h, w_up, conv_w, conv_b, w_down):
    u = h @ w_up
    pad = CONV_WIDTH // 2
    up = jnp.pad(u, ((0, 0), (pad, pad), (0, 0)))
    S = u.shape[1]
    u = conv_b + sum(up[:, t:t + S] * conv_w[t] for t in range(CONV_WIDTH))
    gate, val = jnp.split(u, 2, axis=-1)
    return (jax.nn.gelu(gate, approximate=True) * val) @ w_down


def setup_inputs(seed: int = 0) -> dict:
    key = jax.random.key(seed)
    ks = jax.random.split(key, 32)
    f32 = jnp.float32
    nrm = lambda k, shape, s: jax.random.normal(k, shape, f32) * s
    L = DEPTH
    x = jax.random.normal(ks[0], (BATCH, SEQ, D_MODEL), f32)
    c = jax.random.normal(ks[1], (BATCH, D_MODEL), f32)
    positions = (jnp.arange(SEQ, dtype=jnp.int32)[None, :]
                 + jax.random.randint(ks[2], (BATCH, 1), 0, 1024, dtype=jnp.int32))
    return {
        "x": x,
        "c": c,
        "positions": positions,
        "rel_bias": nrm(ks[3], (REL_BUCKETS, DIFF_HEADS), 0.5),
        "ada_w": nrm(ks[4], (L, D_MODEL, N_MOD * D_MODEL), 0.5 * D_MODEL ** -0.5),
        "ada_b": nrm(ks[5], (L, N_MOD * D_MODEL), 0.02),
        "attn_pre_norm": 1.0 + nrm(ks[6], (L, D_MODEL), 0.05),
        "attn_post_norm": 1.0 + nrm(ks[7], (L, D_MODEL), 0.05),
        "w_in": nrm(ks[8], (L, D_MODEL, IN_WIDTH), D_MODEL ** -0.5),
        "q_norm": 1.0 + nrm(ks[9], (L, MLA_Q_RANK), 0.05),
        "w_uq": nrm(ks[10], (L, MLA_Q_RANK, MLA_HEADS * (MLA_NOPE_DIM + MLA_ROPE_DIM)), MLA_Q_RANK ** -0.5),
        "kv_norm": 1.0 + nrm(ks[11], (L, MLA_KV_RANK), 0.05),
        "w_ukv": nrm(ks[12], (L, MLA_KV_RANK, MLA_HEADS * (MLA_NOPE_DIM + MLA_V_DIM)), MLA_KV_RANK ** -0.5),
        "lambda_q1": nrm(ks[13], (L, DIFF_QK_DIM), 0.1),
        "lambda_k1": nrm(ks[14], (L, DIFF_QK_DIM), 0.1),
        "lambda_q2": nrm(ks[15], (L, DIFF_QK_DIM), 0.1),
        "lambda_k2": nrm(ks[16], (L, DIFF_QK_DIM), 0.1),
        "diff_subln": 1.0 + nrm(ks[17], (L, DIFF_V_DIM), 0.05),
        "w_o": nrm(ks[18], (L, MIX_WIDTH, D_MODEL), MIX_WIDTH ** -0.5),
        "ffn_pre_norm": 1.0 + nrm(ks[19], (L, D_MODEL), 0.05),
        "ffn_post_norm": 1.0 + nrm(ks[20], (L, D_MODEL), 0.05),
        "w_up": nrm(ks[21], (L, D_MODEL, 2 * D_FF), D_MODEL ** -0.5),
        "conv_w": nrm(ks[22], (L, CONV_WIDTH, 2 * D_FF), CONV_WIDTH ** -0.5),
        "conv_b": nrm(ks[23], (L, 2 * D_FF), 0.02),
        "w_down": nrm(ks[24], (L, D_FF, D_MODEL), D_FF ** -0.5),
    }


def reference(x, c, positions, rel_bias, ada_w, ada_b, attn_pre_norm, attn_post_norm,
              w_in, q_norm, w_uq, kv_norm, w_ukv, lambda_q1, lambda_k1, lambda_q2, lambda_k2,
              diff_subln, w_o, ffn_pre_norm, ffn_post_norm, w_up, conv_w, conv_b, w_down):
    cos, sin = rope_tables(positions)
    c_act = jax.nn.silu(c)
    for l in range(DEPTH):
        lambda_init = 0.8 - 0.6 * math.exp(-0.3 * l)
        mod = c_act @ ada_w[l] + ada_b[l]
        sh_a, sc_a, g_a, sh_f, sc_f, g_f = jnp.split(mod, N_MOD, axis=-1)
        h = modulate(rms_norm(x, attn_pre_norm[l]), sh_a, sc_a)
        y = hybrid_mixer(h, positions, cos, sin, rel_bias, w_in[l], q_norm[l], w_uq[l],
                         kv_norm[l], w_ukv[l], lambda_q1[l], lambda_k1[l], lambda_q2[l],
                         lambda_k2[l], diff_subln[l], w_o[l], lambda_init)
        x = x + g_a[:, None, :] * rms_norm(y, attn_post_norm[l])
        h = modulate(rms_norm(x, ffn_pre_norm[l]), sh_f, sc_f)
        y = conv_geglu_ffn(h, w_up[l], conv_w[l], conv_b[l], w_down[l])
        x = x + g_f[:, None, :] * rms_norm(y, ffn_post_norm[l])
    return x
```

```python
import math
from functools import partial

import jax
import jax.numpy as jnp
import numpy as np
from jax import lax
from jax.experimental import pallas as pl
from jax.experimental.pallas import tpu as pltpu

D_MODEL = 2048
MLA_HEADS = 8
MLA_Q_RANK = 512
MLA_KV_RANK = 256
MLA_NOPE = 128
MLA_ROPE = 64
MLA_V = 128
MLA_SCALE = (MLA_NOPE + MLA_ROPE) ** -0.5
DIFF_HEADS = 8
DIFF_QK = 64
DIFF_V = 128
DIFF_SCALE = DIFF_QK ** -0.5
D_FF = 5632
REL_BUCKETS = 32
ROPE_THETA = 10000.0
NORM_EPS = 1e-6
N_MOD = 6
LOG2E = math.log2(math.e)
T5_SATURATE = 128
T5_THRESHOLDS = (12, 16, 23, 32, 46, 64, 91)

LANE = 128
SUBLANE = 8
VMEM_LIMIT_BYTES = 56 * 1024 * 1024

ADA_TN = 1536
INPROJ_TM = 256
ATTN_TQ = 512
ATTN_CK = 512
BIAS_CK = 256
OUTPROJ_TM = 512
UP_TM = 1024
UP_TN = 512
DOWN_TM = 256

F32 = jnp.float32
BF16 = jnp.bfloat16


def _params(*semantics):
    return pltpu.CompilerParams(dimension_semantics=semantics, vmem_limit_bytes=VMEM_LIMIT_BYTES)


def _resident(shape):
    nd = len(shape)
    return pl.BlockSpec(shape, lambda *_: (0,) * nd, pipeline_mode=pl.Buffered(1))


def _rms(x, w_row):
    ms = jnp.mean(x * x, axis=-1, keepdims=True)
    return (x * lax.rsqrt(ms + NORM_EPS)) * w_row


def _nt_dot(a, b):
    return lax.dot_general(a, b, (((1,), (1,)), ((), ())), preferred_element_type=F32)


def _ada_kernel(c_ref, w_ref, b_ref, o_ref):
    c = c_ref[...]
    c_act = (c * jax.nn.sigmoid(c)).astype(BF16)
    o_ref[...] = jnp.dot(c_act, w_ref[...].astype(BF16), preferred_element_type=F32) + b_ref[...]


def _ada_mod(c, ada_w, ada_b):
    bsz, d = c.shape
    n = ada_w.shape[1]
    return pl.pallas_call(
        _ada_kernel,
        grid=(n // ADA_TN,),
        in_specs=[
            pl.BlockSpec((bsz, d), lambda j: (0, 0)),
            pl.BlockSpec((d, ADA_TN), lambda j: (0, j)),
            pl.BlockSpec((1, ADA_TN), lambda j: (0, j)),
        ],
        out_specs=pl.BlockSpec((bsz, ADA_TN), lambda j: (0, j)),
        out_shape=jax.ShapeDtypeStruct((bsz, n), F32),
        compiler_params=_params("arbitrary"),
        name="ada_mod",
    )(c, ada_w, ada_b.reshape(1, n))


N_W1 = MLA_Q_RANK + MLA_KV_RANK + 2 * DIFF_HEADS * 2 * DIFF_QK
N_WT = 2 * LANE + DIFF_HEADS * 2 * DIFF_QK
Q_HEAD_COLS = 3 * LANE


def _in_proj_kernel(x_ref, pcol_ref, prow_ref, sh_ref, sc_ref, pre_ref, w1_ref, wt_ref,
                    qn_ref, wuq_ref, kvn_ref, wukt_ref, wuv_ref, invr_ref, invc_ref, sgr_ref, sgc_ref,
                    q_out, knt_out, krt_out, v_out, dq_out, dkt_out, dv_out):
    x = x_ref[...]
    a_row = pre_ref[...] * (1.0 + sc_ref[0])
    ms = jnp.mean(x * x, axis=-1, keepdims=True)
    h = ((x * lax.rsqrt(ms + NORM_EPS)) * a_row + sh_ref[0]).astype(BF16)

    p1 = jnp.dot(h, w1_ref[...], preferred_element_type=F32)
    p2t = _nt_dot(wt_ref[...], h)

    ang = pcol_ref[0].astype(F32) * invr_ref[...]
    cos_r = jnp.cos(ang)
    sin_r = jnp.sin(ang) * sgr_ref[...]
    ang_t = invc_ref[...] * prow_ref[0].astype(F32)
    cos_t = jnp.cos(ang_t)
    sin_t = jnp.sin(ang_t) * sgc_ref[...]

    q_lat = p1[:, 0:MLA_Q_RANK]
    rq = _rms(q_lat, qn_ref[...] * (MLA_SCALE * LOG2E)).astype(BF16)
    qf = jnp.dot(rq, wuq_ref[...], preferred_element_type=F32)
    for hd in range(MLA_HEADS):
        c0 = hd * Q_HEAD_COLS
        q_out[0, hd, :, 0:LANE] = qf[:, c0:c0 + LANE].astype(BF16)
        rope = qf[:, c0 + LANE:c0 + 2 * LANE] * cos_r + qf[:, c0 + 2 * LANE:c0 + 3 * LANE] * sin_r
        q_out[0, hd, :, LANE:2 * LANE] = rope.astype(BF16)

    kv_lat = p1[:, MLA_Q_RANK:MLA_Q_RANK + MLA_KV_RANK]
    rkv = _rms(kv_lat, kvn_ref[...]).astype(BF16)
    knt = _nt_dot(wukt_ref[...], rkv)
    vv = jnp.dot(rkv, wuv_ref[...], preferred_element_type=F32)
    for hd in range(MLA_HEADS):
        knt_out[0, hd] = knt[hd * LANE:(hd + 1) * LANE, :].astype(BF16)
        v_out[0, hd] = vv[:, hd * LANE:(hd + 1) * LANE].astype(BF16)
    krt_out[0] = (p2t[0:LANE, :] * cos_t + p2t[LANE:2 * LANE, :] * sin_t).astype(BF16)

    o_dq = MLA_Q_RANK + MLA_KV_RANK
    o_dv = o_dq + DIFF_HEADS * 2 * DIFF_QK
    for hd in range(DIFF_HEADS):
        dq_out[0, hd] = (p1[:, o_dq + hd * LANE:o_dq + (hd + 1) * LANE] * (DIFF_SCALE * LOG2E)).astype(BF16)
        dv_out[0, hd] = p1[:, o_dv + hd * LANE:o_dv + (hd + 1) * LANE].astype(BF16)
        dkt_out[0, hd] = p2t[2 * LANE + hd * LANE:2 * LANE + (hd + 1) * LANE, :].astype(BF16)


def _in_proj(x2, pos_col, pos_row, sh_a, sc_a, pre_w, w1, wt, q_norm, wuq, kv_norm, wukt, wuv,
             inv_row, inv_col, sg_row, sg_col, bsz, seq):
    tm = INPROJ_TM
    tpb = seq // tm
    d = x2.shape[1]
    bmap3 = lambda i: (i // tpb, 0, 0)
    out_shapes = (
        jax.ShapeDtypeStruct((bsz, MLA_HEADS, seq, 2 * LANE), BF16),
        jax.ShapeDtypeStruct((bsz, MLA_HEADS, LANE, seq), BF16),
        jax.ShapeDtypeStruct((bsz, LANE, seq), BF16),
        jax.ShapeDtypeStruct((bsz, MLA_HEADS, seq, LANE), BF16),
        jax.ShapeDtypeStruct((bsz, DIFF_HEADS, seq, LANE), BF16),
        jax.ShapeDtypeStruct((bsz, DIFF_HEADS, LANE, seq), BF16),
        jax.ShapeDtypeStruct((bsz, DIFF_HEADS, seq, LANE), BF16),
    )
    row_blk = lambda nh, w: pl.BlockSpec((1, nh, tm, w), lambda i: (i // tpb, 0, i % tpb, 0))
    col_blk = lambda nh: pl.BlockSpec((1, nh, LANE, tm), lambda i: (i // tpb, 0, 0, i % tpb))
    return pl.pallas_call(
        _in_proj_kernel,
        grid=(bsz * tpb,),
        in_specs=[
            pl.BlockSpec((tm, d), lambda i: (i, 0)),
            pl.BlockSpec((1, tm, 1), lambda i: (i // tpb, i % tpb, 0)),
            pl.BlockSpec((1, 1, tm), lambda i: (i // tpb, 0, i % tpb)),
            pl.BlockSpec((1, 1, d), bmap3),
            pl.BlockSpec((1, 1, d), bmap3),
            _resident((1, d)),
            _resident(w1.shape),
            _resident(wt.shape),
            _resident(q_norm.shape),
            _resident(wuq.shape),
            _resident(kv_norm.shape),
            _resident(wukt.shape),
            _resident(wuv.shape),
            _resident(inv_row.shape),
            _resident(inv_col.shape),
            _resident(sg_row.shape),
            _resident(sg_col.shape),
        ],
        out_specs=(
            row_blk(MLA_HEADS, 2 * LANE),
            col_blk(MLA_HEADS),
            pl.BlockSpec((1, LANE, tm), lambda i: (i // tpb, 0, i % tpb)),
            row_blk(MLA_HEADS, LANE),
            row_blk(DIFF_HEADS, LANE),
            col_blk(DIFF_HEADS),
            row_blk(DIFF_HEADS, LANE),
        ),
        out_shape=out_shapes,
        compiler_params=_params("arbitrary"),
        name="in_proj",
    )(x2, pos_col, pos_row, sh_a, sc_a, pre_w, w1, wt, q_norm, wuq, kv_norm, wukt, wuv,
      inv_row, inv_col, sg_row, sg_col)


def _lane_tile_reduce(x, op):
    out = x[:, 0:LANE]
    for t in range(1, x.shape[1] // LANE):
        out = op(out, x[:, t * LANE:(t + 1) * LANE])
    return out


def _softmax_pv(s_ref, m_col, v_ref, seq):
    acc = None
    lsum = None
    for c in range(seq // ATTN_CK):
        p = jnp.exp2(s_ref[:, c * ATTN_CK:(c + 1) * ATTN_CK] - m_col)
        lp = _lane_tile_reduce(p, jnp.add)
        pv = jnp.dot(p.astype(BF16), v_ref[c * ATTN_CK:(c + 1) * ATTN_CK, :], preferred_element_type=F32)
        acc = pv if acc is None else acc + pv
        lsum = lp if lsum is None else lsum + lp
    return acc, jnp.sum(lsum, axis=-1, keepdims=True)


def _mla_kernel(q_ref, knt_ref, krt_ref, v_ref, o_ref, s_ref):
    seq = s_ref.shape[1]
    q = q_ref[0, 0]
    m_part = None
    for c in range(seq // ATTN_CK):
        sl = slice(c * ATTN_CK, (c + 1) * ATTN_CK)
        kt = jnp.concatenate([knt_ref[0, 0, :, sl], krt_ref[0, :, sl]], axis=0)
        s = jnp.dot(q, kt, preferred_element_type=F32)
        s_ref[:, sl] = s
        mp = _lane_tile_reduce(s, jnp.maximum)
        m_part = mp if m_part is None else jnp.maximum(m_part, mp)
    m_col = jnp.max(m_part, axis=-1, keepdims=True)
    acc, l_col = _softmax_pv(s_ref, m_col, v_ref.at[0, 0], seq)
    o_ref[0] = (acc * (1.0 / l_col)).astype(o_ref.dtype)


def _mla_attn(q, knt, krt, v):
    bsz, nh, seq, _ = q.shape
    tq = ATTN_TQ
    return pl.pallas_call(
        _mla_kernel,
        grid=(bsz, nh, seq // tq),
        in_specs=[
            pl.BlockSpec((1, 1, tq, 2 * LANE), lambda b, h, i: (b, h, i, 0)),
            pl.BlockSpec((1, 1, LANE, seq), lambda b, h, i: (b, h, 0, 0)),
            pl.BlockSpec((1, LANE, seq), lambda b, h, i: (b, 0, 0)),
            pl.BlockSpec((1, 1, seq, LANE), lambda b, h, i: (b, h, 0, 0)),
        ],
        out_specs=pl.BlockSpec((1, tq, LANE), lambda b, h, i: (b, i, h)),
        out_shape=jax.ShapeDtypeStruct((bsz, seq, nh * MLA_V), BF16),
        scratch_shapes=[pltpu.VMEM((tq, seq), F32)],
        compiler_params=_params("arbitrary", "arbitrary", "arbitrary"),
        name="mla_attn",
    )(q, knt, krt, v)


def _t5_bucket(rel):
    n = jnp.abs(rel)
    large = jnp.full(rel.shape, 8, jnp.int32)
    for t in T5_THRESHOLDS:
        large = large + jnp.where(n >= t, 1, 0)
    return jnp.where(rel > 0, REL_BUCKETS // 2, 0) + jnp.where(n < 8, n, large)


def _diff_kernel(pos_sm, tab_sm, dq_ref, dkt_ref, dv_ref, pcol_ref, prow_ref, tabt_ref,
                 lq1_ref, lk1_ref, lq2_ref, lk2_ref, subln_ref, o_ref,
                 s1_ref, s2_ref, bkt_ref, m1_ref, m2_ref, *, lambda_init):
    b = pl.program_id(0)
    qi = pl.program_id(1)
    hd = pl.program_id(2)
    tq, seq = s1_ref.shape
    nbc = seq // BIAS_CK

    q_lo = pos_sm[b, qi * tq]
    q_hi = pos_sm[b, qi * tq + tq - 1]
    is_neg, is_pos, cval = [], [], []
    for c in range(nbc):
        rel_hi = pos_sm[b, c * BIAS_CK + BIAS_CK - 1] - q_lo
        rel_lo = pos_sm[b, c * BIAS_CK] - q_hi
        neg = rel_hi <= -T5_SATURATE
        pos = rel_lo >= T5_SATURATE
        is_neg.append(neg)
        is_pos.append(pos)
        cval.append(jnp.where(neg, tab_sm[REL_BUCKETS // 2 - 1, hd],
                              jnp.where(pos, tab_sm[REL_BUCKETS - 1, hd], 0.0)) * LOG2E)

    @pl.when(hd == 0)
    def _():
        pq = pcol_ref[0]
        for c in range(nbc):
            @pl.when(jnp.logical_not(jnp.logical_or(is_neg[c], is_pos[c])))
            def _():
                pk = prow_ref[0, :, c * BIAS_CK:(c + 1) * BIAS_CK]
                bkt_ref[:, c * BIAS_CK:(c + 1) * BIAS_CK] = _t5_bucket(pk - pq)

    q = dq_ref[0, 0]
    lane = lax.broadcasted_iota(jnp.int32, q.shape, 1)
    zero = jnp.zeros_like(q)
    q1 = jnp.where(lane < DIFF_QK, q, zero)
    q2 = jnp.where(lane >= DIFF_QK, q, zero)

    neg_inf = jnp.full((tq, LANE), -jnp.inf, F32)
    m1 = neg_inf
    m2 = neg_inf
    for c in range(nbc):
        sl = slice(c * BIAS_CK, (c + 1) * BIAS_CK)
        kt = dkt_ref[0, 0, :, sl]
        sa = jnp.dot(q1, kt, preferred_element_type=F32) + cval[c]
        sb = jnp.dot(q2, kt, preferred_element_type=F32) + cval[c]
        s1_ref[:, sl] = sa
        s2_ref[:, sl] = sb
        const = jnp.logical_or(is_neg[c], is_pos[c])
        m1 = jnp.where(const, jnp.maximum(m1, _lane_tile_reduce(sa, jnp.maximum)), m1)
        m2 = jnp.where(const, jnp.maximum(m2, _lane_tile_reduce(sb, jnp.maximum)), m2)
    m1_ref[...] = m1
    m2_ref[...] = m2

    tab_b = jnp.broadcast_to(tabt_ref[pl.ds(hd, 1), :] * LOG2E, (tq, LANE))
    for c in range(nbc):
        @pl.when(jnp.logical_not(jnp.logical_or(is_neg[c], is_pos[c])))
        def _():
            for t in range(BIAS_CK // LANE):
                sl = slice(c * BIAS_CK + t * LANE, c * BIAS_CK + (t + 1) * LANE)
                bias = jnp.take_along_axis(tab_b, bkt_ref[:, sl], axis=1, mode="promise_in_bounds")
                va = s1_ref[:, sl] + bias
                vb = s2_ref[:, sl] + bias
                s1_ref[:, sl] = va
                s2_ref[:, sl] = vb
                m1_ref[...] = jnp.maximum(m1_ref[...], va)
                m2_ref[...] = jnp.maximum(m2_ref[...], vb)

    m1c = jnp.max(m1_ref[...], axis=-1, keepdims=True)
    m2c = jnp.max(m2_ref[...], axis=-1, keepdims=True)
    acc1, l1 = _softmax_pv(s1_ref, m1c, dv_ref.at[0, 0], seq)
    acc2, l2 = _softmax_pv(s2_ref, m2c, dv_ref.at[0, 0], seq)

    lam = (jnp.exp(jnp.sum(lq1_ref[...] * lk1_ref[...], axis=-1, keepdims=True))
           - jnp.exp(jnp.sum(lq2_ref[...] * lk2_ref[...], axis=-1, keepdims=True)) + lambda_init)
    o = acc1 * (1.0 / l1) - (acc2 * (1.0 / l2)) * lam
    o_ref[0] = (_rms(o, subln_ref[...]) * (1.0 - lambda_init)).astype(o_ref.dtype)


def _diff_attn(positions, rel_bias, dq, dkt, dv, pos_col, pos_row, tab_t, lq1, lk1, lq2, lk2, subln,
               lambda_init):
    bsz, nh, seq, _ = dq.shape
    tq = ATTN_TQ
    grid_spec = pltpu.PrefetchScalarGridSpec(
        num_scalar_prefetch=2,
        grid=(bsz, seq // tq, nh),
        in_specs=[
            pl.BlockSpec((1, 1, tq, LANE), lambda b, i, h, *_: (b, h, i, 0)),
            pl.BlockSpec((1, 1, LANE, seq), lambda b, i, h, *_: (b, h, 0, 0)),
            pl.BlockSpec((1, 1, seq, LANE), lambda b, i, h, *_: (b, h, 0, 0)),
            pl.BlockSpec((1, tq, 1), lambda b, i, h, *_: (b, i, 0)),
            pl.BlockSpec((1, 1, seq), lambda b, i, h, *_: (b, 0, 0)),
            pl.BlockSpec(tab_t.shape, lambda b, i, h, *_: (0, 0)),
            pl.BlockSpec((1, DIFF_QK), lambda b, i, h, *_: (0, 0)),
            pl.BlockSpec((1, DIFF_QK), lambda b, i, h, *_: (0, 0)),
            pl.BlockSpec((1, DIFF_QK), lambda b, i, h, *_: (0, 0)),
            pl.BlockSpec((1, DIFF_QK), lambda b, i, h, *_: (0, 0)),
            pl.BlockSpec((1, DIFF_V), lambda b, i, h, *_: (0, 0)),
        ],
        out_specs=pl.BlockSpec((1, tq, LANE), lambda b, i, h, *_: (b, i, h)),
        scratch_shapes=[
            pltpu.VMEM((tq, seq), F32),
            pltpu.VMEM((tq, seq), F32),
            pltpu.VMEM((tq, seq), jnp.int32),
            pltpu.VMEM((tq, LANE), F32),
            pltpu.VMEM((tq, LANE), F32),
        ],
    )
    return pl.pallas_call(
        partial(_diff_kernel, lambda_init=lambda_init),
        grid_spec=grid_spec,
        out_shape=jax.ShapeDtypeStruct((bsz, seq, nh * DIFF_V), BF16),
        compiler_params=_params("arbitrary", "arbitrary", "arbitrary"),
        name="diff_attn",
    )(positions, rel_bias, dq, dkt, dv, pos_col, pos_row, tab_t, lq1, lk1, lq2, lk2, subln)


def _out_proj_kernel(om_ref, od_ref, wa_ref, wb_ref, x_ref, g_ref, post_ref, sh_ref, sc_ref, pre_ref,
                     x1_ref, h2_ref):
    y = (jnp.dot(om_ref[...], wa_ref[...], preferred_element_type=F32)
         + jnp.dot(od_ref[...], wb_ref[...], preferred_element_type=F32))
    x1 = x_ref[...] + g_ref[0] * _rms(y, post_ref[...])
    x1_ref[...] = x1
    a_row = pre_ref[...] * (1.0 + sc_ref[0])
    ms = jnp.mean(x1 * x1, axis=-1, keepdims=True)
    h2_ref[...] = ((x1 * lax.rsqrt(ms + NORM_EPS)) * a_row + sh_ref[0]).astype(h2_ref.dtype)


def _out_proj(o_mla, o_diff, wo_a, wo_b, x2, g_a, post_w, sh_f, sc_f, pre_w, seq):
    m, d = x2.shape
    tm = OUTPROJ_TM
    tpb = seq // tm
    ka = o_mla.shape[1]
    kb = o_diff.shape[1]
    bmap3 = lambda i: (i // tpb, 0, 0)
    return pl.pallas_call(
        _out_proj_kernel,
        grid=(m // tm,),
        in_specs=[
            pl.BlockSpec((tm, ka), lambda i: (i, 0)),
            pl.BlockSpec((tm, kb), lambda i: (i, 0)),
            _resident(wo_a.shape),
            _resident(wo_b.shape),
            pl.BlockSpec((tm, d), lambda i: (i, 0)),
            pl.BlockSpec((1, 1, d), bmap3),
            _resident((1, d)),
            pl.BlockSpec((1, 1, d), bmap3),
            pl.BlockSpec((1, 1, d), bmap3),
            _resident((1, d)),
        ],
        out_specs=(pl.BlockSpec((tm, d), lambda i: (i, 0)), pl.BlockSpec((tm, d), lambda i: (i, 0))),
        out_shape=(jax.ShapeDtypeStruct((m, d), F32), jax.ShapeDtypeStruct((m, d), BF16)),
        compiler_params=_params("arbitrary"),
        name="out_proj",
    )(o_mla, o_diff, wo_a, wo_b, x2, g_a, post_w, sh_f, sc_f, pre_w)


def _gelu_tanh(g):
    return 0.5 * g * (1.0 + jnp.tanh(math.sqrt(2.0 / math.pi) * (g + 0.044715 * (g * g * g))))


def _ffn_up_kernel(h_ref, hp_ref, hn_ref, wg_ref, wv_ref, cwg_ref, cwv_ref, cbg_ref, cbv_ref, o_ref,
                   ug_ref, uv_ref, *, tiles_per_batch):
    i = pl.program_id(1)
    tm = h_ref.shape[0]
    first = (i % tiles_per_batch) == 0
    last = (i % tiles_per_batch) == tiles_per_batch - 1
    h = h_ref[...]
    hp = hp_ref[...]
    hn = hn_ref[...]

    def conv(w_ref, cw_ref, cb_ref, u_ref):
        w = w_ref[...]
        u_ref[SUBLANE:SUBLANE + tm, :] = jnp.dot(h, w, preferred_element_type=F32)
        up = jnp.dot(hp, w, preferred_element_type=F32)
        un = jnp.dot(hn, w, preferred_element_type=F32)
        u_ref[0:SUBLANE, :] = jnp.where(first, 0.0, up)
        u_ref[SUBLANE + tm:2 * SUBLANE + tm, :] = jnp.where(last, 0.0, un)
        return (cb_ref[...]
                + u_ref[SUBLANE - 1:SUBLANE - 1 + tm, :] * cw_ref[0:1, :]
                + u_ref[SUBLANE:SUBLANE + tm, :] * cw_ref[1:2, :]
                + u_ref[SUBLANE + 1:SUBLANE + 1 + tm, :] * cw_ref[2:3, :])

    gate = conv(wg_ref, cwg_ref, cbg_ref, ug_ref)
    val = conv(wv_ref, cwv_ref, cbv_ref, uv_ref)
    o_ref[...] = (_gelu_tanh(gate) * val).astype(o_ref.dtype)


def _ffn_up(h2, w_up, conv_w, conv_b, seq):
    m, d = h2.shape
    f = w_up.shape[1] // 2
    tm, tn = UP_TM, UP_TN
    tpb = seq // tm
    nj = f // tn
    rb = tm // SUBLANE
    nrb = m // SUBLANE
    return pl.pallas_call(
        partial(_ffn_up_kernel, tiles_per_batch=tpb),
        grid=(nj, m // tm),
        in_specs=[
            pl.BlockSpec((tm, d), lambda j, i: (i, 0)),
            pl.BlockSpec((SUBLANE, d), lambda j, i: (jnp.maximum(i * rb - 1, 0), 0)),
            pl.BlockSpec((SUBLANE, d), lambda j, i: (jnp.minimum((i + 1) * rb, nrb - 1), 0)),
            pl.BlockSpec((d, tn), lambda j, i: (0, j)),
            pl.BlockSpec((d, tn), lambda j, i: (0, j + nj)),
            pl.BlockSpec((3, tn), lambda j, i: (0, j)),
            pl.BlockSpec((3, tn), lambda j, i: (0, j + nj)),
            pl.BlockSpec((1, tn), lambda j, i: (0, j)),
            pl.BlockSpec((1, tn), lambda j, i: (0, j + nj)),
        ],
        out_specs=pl.BlockSpec((tm, tn), lambda j, i: (i, j)),
        out_shape=jax.ShapeDtypeStruct((m, f), BF16),
        scratch_shapes=[pltpu.VMEM((tm + 2 * SUBLANE, tn), F32), pltpu.VMEM((tm + 2 * SUBLANE, tn), F32)],
        compiler_params=_params("arbitrary", "arbitrary"),
        name="ffn_up",
    )(h2, h2, h2, w_up, w_up, conv_w, conv_w, conv_b, conv_b)


def _ffn_down_kernel(a_ref, w_ref, x1_ref, g_ref, post_ref, o_ref):
    y = jnp.dot(a_ref[...], w_ref[...], preferred_element_type=F32)
    o_ref[...] = x1_ref[...] + g_ref[0] * _rms(y, post_ref[...])


def _ffn_down(act, w_down, x1, g_f, post_w, seq):
    m, f = act.shape
    d = w_down.shape[1]
    tm = DOWN_TM
    tpb = seq // tm
    return pl.pallas_call(
        _ffn_down_kernel,
        grid=(m // tm,),
        in_specs=[
            pl.BlockSpec((tm, f), lambda i: (i, 0)),
            _resident(w_down.shape),
            pl.BlockSpec((tm, d), lambda i: (i, 0)),
            pl.BlockSpec((1, 1, d), lambda i: (i // tpb, 0, 0)),
            _resident((1, d)),
        ],
        out_specs=pl.BlockSpec((tm, d), lambda i: (i, 0)),
        out_shape=jax.ShapeDtypeStruct((m, d), F32),
        compiler_params=_params("arbitrary"),
        name="ffn_down",
    )(act, w_down, x1, g_f, post_w)


def _rope_pair_tiles(w_rope):
    half = MLA_ROPE // 2
    t1, t2 = w_rope[:, :half], w_rope[:, half:]
    z = jnp.zeros((w_rope.shape[0], LANE - MLA_ROPE), w_rope.dtype)
    return jnp.concatenate([t1, t2, z], axis=1), jnp.concatenate([t2, t1, z], axis=1)


def _layout_w_in(w_in):
    o_kv = MLA_Q_RANK
    o_kr = o_kv + MLA_KV_RANK
    o_dq = o_kr + MLA_ROPE
    o_dk = o_dq + DIFF_HEADS * 2 * DIFF_QK
    o_dv = o_dk + DIFF_HEADS * 2 * DIFF_QK
    w1 = jnp.concatenate([w_in[:, :o_kr], w_in[:, o_dq:o_dk], w_in[:, o_dv:]], axis=1).astype(BF16)
    ra, rb = _rope_pair_tiles(w_in[:, o_kr:o_dq])
    wt = jnp.concatenate([ra, rb, w_in[:, o_dk:o_dv]], axis=1).T.astype(BF16)
    return w1, wt


def _layout_w_uq(w_uq):
    k = w_uq.shape[0]
    w3 = w_uq.reshape(k, MLA_HEADS, MLA_NOPE + MLA_ROPE)
    blocks = []
    for hd in range(MLA_HEADS):
        ra, rb = _rope_pair_tiles(w3[:, hd, MLA_NOPE:])
        blocks += [w3[:, hd, :MLA_NOPE], ra, rb]
    return jnp.concatenate(blocks, axis=1).astype(BF16)


def _layout_w_ukv(w_ukv):
    k = w_ukv.shape[0]
    w3 = w_ukv.reshape(k, MLA_HEADS, MLA_NOPE + MLA_V)
    wukt = w3[:, :, :MLA_NOPE].reshape(k, MLA_HEADS * MLA_NOPE).T.astype(BF16)
    wuv = w3[:, :, MLA_NOPE:].reshape(k, MLA_HEADS * MLA_V).astype(BF16)
    return wukt, wuv


def _rope_constants():
    half = MLA_ROPE // 2
    inv = 1.0 / (ROPE_THETA ** (np.arange(0, MLA_ROPE, 2, dtype=np.float32) / MLA_ROPE))
    inv = inv.astype(np.float32)
    inv128 = np.zeros((LANE,), np.float32)
    inv128[:half] = inv
    inv128[half:2 * half] = inv
    sgn = np.zeros((LANE,), np.float32)
    sgn[:half] = -1.0
    sgn[half:2 * half] = 1.0
    return (jnp.asarray(inv128.reshape(1, LANE)), jnp.asarray(inv128.reshape(LANE, 1)),
            jnp.asarray(sgn.reshape(1, LANE)), jnp.asarray(sgn.reshape(LANE, 1)))


def kernel(x, c, positions, rel_bias, ada_w, ada_b, attn_pre_norm, attn_post_norm, w_in, q_norm, w_uq,
           kv_norm, w_ukv, lambda_q1, lambda_k1, lambda_q2, lambda_k2, diff_subln, w_o, ffn_pre_norm,
           ffn_post_norm, w_up, conv_w, conv_b, w_down):
    bsz, seq, d = x.shape
    depth = ada_w.shape[0]
    m = bsz * seq
    positions = positions.astype(jnp.int32)
    pos_col = positions.reshape(bsz, seq, 1)
    pos_row = positions.reshape(bsz, 1, seq)
    inv_row, inv_col, sg_row, sg_col = _rope_constants()
    tab_t = jnp.zeros((DIFF_HEADS, LANE), F32).at[:, :REL_BUCKETS].set(rel_bias.astype(F32).T)
    row = lambda v: v.reshape(1, -1).astype(F32)

    x2 = x.reshape(m, d)
    for l in range(depth):
        lambda_init = 0.8 - 0.6 * math.exp(-0.3 * l)
        mod = _ada_mod(c, ada_w[l], ada_b[l])
        sh_a, sc_a, g_a, sh_f, sc_f, g_f = [t.reshape(bsz, 1, d) for t in jnp.split(mod, N_MOD, axis=-1)]

        w1, wt = _layout_w_in(w_in[l])
        wuq = _layout_w_uq(w_uq[l])
        wukt, wuv = _layout_w_ukv(w_ukv[l])
        q, knt, krt, v, dq, dkt, dv = _in_proj(
            x2, pos_col, pos_row, sh_a, sc_a, row(attn_pre_norm[l]), w1, wt, row(q_norm[l]), wuq,
            row(kv_norm[l]), wukt, wuv, inv_row, inv_col, sg_row, sg_col, bsz, seq)

        o_mla = _mla_attn(q, knt, krt, v)
        o_diff = _diff_attn(positions, rel_bias.astype(F32), dq, dkt, dv, pos_col, pos_row, tab_t,
                            row(lambda_q1[l]), row(lambda_k1[l]), row(lambda_q2[l]), row(lambda_k2[l]),
                            row(diff_subln[l]), lambda_init)

        wo = w_o[l].astype(BF16)
        n_mla = MLA_HEADS * MLA_V
        x1, h2 = _out_proj(o_mla.reshape(m, n_mla), o_diff.reshape(m, DIFF_HEADS * DIFF_V),
                           wo[:n_mla], wo[n_mla:], x2, g_a, row(attn_post_norm[l]), sh_f, sc_f,
                           row(ffn_pre_norm[l]), seq)

        act = _ffn_up(h2, w_up[l].astype(BF16), conv_w[l].astype(F32), row(conv_b[l]), seq)
        x2 = _ffn_down(act, w_down[l].astype(BF16), x1, g_f, row(ffn_post_norm[l]), seq)
    return x2.reshape(bsz, seq, d)
```

```python
import math
from functools import partial

import jax
import jax.numpy as jnp
import numpy as np
from jax import lax
from jax.experimental import pallas as pl
from jax.experimental.pallas import tpu as pltpu

D_MODEL = 2048
MLA_HEADS = 8
MLA_Q_RANK = 512
MLA_KV_RANK = 256
MLA_NOPE = 128
MLA_ROPE = 64
MLA_V = 128
MLA_SCALE = (MLA_NOPE + MLA_ROPE) ** -0.5
DIFF_HEADS = 8
DIFF_QK = 64
DIFF_V = 128
DIFF_SCALE = DIFF_QK ** -0.5
D_FF = 5632
REL_BUCKETS = 32
ROPE_THETA = 10000.0
NORM_EPS = 1e-6
N_MOD = 6
LOG2E = math.log2(math.e)
T5_SATURATE = 128
T5_THRESHOLDS = (12, 16, 23, 32, 46, 64, 91)

LANE = 128
SUBLANE = 8
VMEM_LIMIT_BYTES = 56 * 1024 * 1024

ADA_TN = 1536
INPROJ_TM = 256
ATTN_TQ = 512
ATTN_CK = 512
BIAS_CK = 256
OUTPROJ_TM = 512
UP_TM = 1024
UP_TN = 512
DOWN_TM = 256

F32 = jnp.float32
BF16 = jnp.bfloat16


def _params(*semantics):
    return pltpu.CompilerParams(dimension_semantics=semantics, vmem_limit_bytes=VMEM_LIMIT_BYTES)


def _resident(shape):
    nd = len(shape)
    return pl.BlockSpec(shape, lambda *_: (0,) * nd, pipeline_mode=pl.Buffered(1))


def _rms(x, w_row):
    ms = jnp.mean(x * x, axis=-1, keepdims=True)
    return (x * lax.rsqrt(ms + NORM_EPS)) * w_row


def _nt_dot(a, b):
    return lax.dot_general(a, b, (((1,), (1,)), ((), ())), preferred_element_type=F32)


def _ada_kernel(c_ref, w_ref, b_ref, o_ref):
    c = c_ref[...]
    c_act = (c * jax.nn.sigmoid(c)).astype(BF16)
    o_ref[...] = jnp.dot(c_act, w_ref[...].astype(BF16), preferred_element_type=F32) + b_ref[...]


def _ada_mod(c, ada_w, ada_b):
    bsz, d = c.shape
    n = ada_w.shape[1]
    return pl.pallas_call(
        _ada_kernel,
        grid=(n // ADA_TN,),
        in_specs=[
            pl.BlockSpec((bsz, d), lambda j: (0, 0)),
            pl.BlockSpec((d, ADA_TN), lambda j: (0, j)),
            pl.BlockSpec((1, ADA_TN), lambda j: (0, j)),
        ],
        out_specs=pl.BlockSpec((bsz, ADA_TN), lambda j: (0, j)),
        out_shape=jax.ShapeDtypeStruct((bsz, n), F32),
        compiler_params=_params("arbitrary"),
        name="ada_mod",
    )(c, ada_w, ada_b.reshape(1, n))


N_W1 = MLA_Q_RANK + MLA_KV_RANK + 2 * DIFF_HEADS * 2 * DIFF_QK
N_WT = 2 * LANE + DIFF_HEADS * 2 * DIFF_QK
Q_HEAD_COLS = 3 * LANE


def _in_proj_kernel(x_ref, pcol_ref, prow_ref, sh_ref, sc_ref, pre_ref, w1_ref, wt_ref,
                    qn_ref, wuq_ref, kvn_ref, wukt_ref, wuv_ref, invr_ref, invc_ref, sgr_ref, sgc_ref,
                    q_out, knt_out, krt_out, v_out, dq_out, dkt_out, dv_out):
    x = x_ref[...]
    a_row = pre_ref[...] * (1.0 + sc_ref[0])
    ms = jnp.mean(x * x, axis=-1, keepdims=True)
    h = ((x * lax.rsqrt(ms + NORM_EPS)) * a_row + sh_ref[0]).astype(BF16)

    p1 = jnp.dot(h, w1_ref[...], preferred_element_type=F32)
    p2t = _nt_dot(wt_ref[...], h)

    ang = pcol_ref[0].astype(F32) * invr_ref[...]
    cos_r = jnp.cos(ang)
    sin_r = jnp.sin(ang) * sgr_ref[...]
    ang_t = invc_ref[...] * prow_ref[0].astype(F32)
    cos_t = jnp.cos(ang_t)
    sin_t = jnp.sin(ang_t) * sgc_ref[...]

    q_lat = p1[:, 0:MLA_Q_RANK]
    rq = _rms(q_lat, qn_ref[...] * (MLA_SCALE * LOG2E)).astype(BF16)
    qf = jnp.dot(rq, wuq_ref[...], preferred_element_type=F32)
    for hd in range(MLA_HEADS):
        c0 = hd * Q_HEAD_COLS
        q_out[0, hd, :, 0:LANE] = qf[:, c0:c0 + LANE].astype(BF16)
        rope = qf[:, c0 + LANE:c0 + 2 * LANE] * cos_r + qf[:, c0 + 2 * LANE:c0 + 3 * LANE] * sin_r
        q_out[0, hd, :, LANE:2 * LANE] = rope.astype(BF16)

    kv_lat = p1[:, MLA_Q_RANK:MLA_Q_RANK + MLA_KV_RANK]
    rkv = _rms(kv_lat, kvn_ref[...]).astype(BF16)
    knt = _nt_dot(wukt_ref[...], rkv)
    vv = jnp.dot(rkv, wuv_ref[...], preferred_element_type=F32)
    for hd in range(MLA_HEADS):
        knt_out[0, hd] = knt[hd * LANE:(hd + 1) * LANE, :].astype(BF16)
        v_out[0, hd] = vv[:, hd * LANE:(hd + 1) * LANE].astype(BF16)
    krt_out[0] = (p2t[0:LANE, :] * cos_t + p2t[LANE:2 * LANE, :] * sin_t).astype(BF16)

    o_dq = MLA_Q_RANK + MLA_KV_RANK
    o_dv = o_dq + DIFF_HEADS * 2 * DIFF_QK
    for hd in range(DIFF_HEADS):
        dq_out[0, hd] = (p1[:, o_dq + hd * LANE:o_dq + (hd + 1) * LANE] * (DIFF_SCALE * LOG2E)).astype(BF16)
        dv_out[0, hd] = p1[:, o_dv + hd * LANE:o_dv + (hd + 1) * LANE].astype(BF16)
        dkt_out[0, hd] = p2t[2 * LANE + hd * LANE:2 * LANE + (hd + 1) * LANE, :].astype(BF16)


def _in_proj(x2, pos_col, pos_row, sh_a, sc_a, pre_w, w1, wt, q_norm, wuq, kv_norm, wukt, wuv,
             inv_row, inv_col, sg_row, sg_col, bsz, seq):
    tm = INPROJ_TM
    tpb = seq // tm
    d = x2.shape[1]
    bmap3 = lambda i: (i // tpb, 0, 0)
    out_shapes = (
        jax.ShapeDtypeStruct((bsz, MLA_HEADS, seq, 2 * LANE), BF16),
        jax.ShapeDtypeStruct((bsz, MLA_HEADS, LANE, seq), BF16),
        jax.ShapeDtypeStruct((bsz, LANE, seq), BF16),
        jax.ShapeDtypeStruct((bsz, MLA_HEADS, seq, LANE), BF16),
        jax.ShapeDtypeStruct((bsz, DIFF_HEADS, seq, LANE), BF16),
        jax.ShapeDtypeStruct((bsz, DIFF_HEADS, LANE, seq), BF16),
        jax.ShapeDtypeStruct((bsz, DIFF_HEADS, seq, LANE), BF16),
    )
    row_blk = lambda nh, w: pl.BlockSpec((1, nh, tm, w), lambda i: (i // tpb, 0, i % tpb, 0))
    col_blk = lambda nh: pl.BlockSpec((1, nh, LANE, tm), lambda i: (i // tpb, 0, 0, i % tpb))
    return pl.pallas_call(
        _in_proj_kernel,
        grid=(bsz * tpb,),
        in_specs=[
            pl.BlockSpec((tm, d), lambda i: (i, 0)),
            pl.BlockSpec((1, tm, 1), lambda i: (i // tpb, i % tpb, 0)),
            pl.BlockSpec((1, 1, tm), lambda i: (i // tpb, 0, i % tpb)),
            pl.BlockSpec((1, 1, d), bmap3),
            pl.BlockSpec((1, 1, d), bmap3),
            _resident((1, d)),
            _resident(w1.shape),
            _resident(wt.shape),
            _resident(q_norm.shape),
            _resident(wuq.shape),
            _resident(kv_norm.shape),
            _resident(wukt.shape),
            _resident(wuv.shape),
            _resident(inv_row.shape),
            _resident(inv_col.shape),
            _resident(sg_row.shape),
            _resident(sg_col.shape),
        ],
        out_specs=(
            row_blk(MLA_HEADS, 2 * LANE),
            col_blk(MLA_HEADS),
            pl.BlockSpec((1, LANE, tm), lambda i: (i // tpb, 0, i % tpb)),
            row_blk(MLA_HEADS, LANE),
            row_blk(DIFF_HEADS, LANE),
            col_blk(DIFF_HEADS),
            row_blk(DIFF_HEADS, LANE),
        ),
        out_shape=out_shapes,
        compiler_params=_params("arbitrary"),
        name="in_proj",
    )(x2, pos_col, pos_row, sh_a, sc_a, pre_w, w1, wt, q_norm, wuq, kv_norm, wukt, wuv,
      inv_row, inv_col, sg_row, sg_col)


def _lane_tile_reduce(x, op):
    out = x[:, 0:LANE]
    for t in range(1, x.shape[1] // LANE):
        out = op(out, x[:, t * LANE:(t + 1) * LANE])
    return out


def _softmax_pv(s_ref, m_col, v_ref, seq):
    acc = None
    lsum = None
    for c in range(seq // ATTN_CK):
        p = jnp.exp2(s_ref[:, c * ATTN_CK:(c + 1) * ATTN_CK] - m_col)
        lp = _lane_tile_reduce(p, jnp.add)
        pv = jnp.dot(p.astype(BF16), v_ref[c * ATTN_CK:(c + 1) * ATTN_CK, :], preferred_element_type=F32)
        acc = pv if acc is None else acc + pv
        lsum = lp if lsum is None else lsum + lp
    return acc, jnp.sum(lsum, axis=-1, keepdims=True)


def _mla_kernel(q_ref, knt_ref, krt_ref, v_ref, o_ref, s_ref):
    seq = s_ref.shape[2]
    for hd in range(MLA_HEADS):
        sc_ref = s_ref.at[hd % 2]
        q = q_ref[0, hd]
        m_part = None
        for c in range(seq // ATTN_CK):
            sl = slice(c * ATTN_CK, (c + 1) * ATTN_CK)
            kt = jnp.concatenate([knt_ref[0, hd, :, sl], krt_ref[0, :, sl]], axis=0)
            s = jnp.dot(q, kt, preferred_element_type=F32)
            sc_ref[:, sl] = s
            mp = _lane_tile_reduce(s, jnp.maximum)
            m_part = mp if m_part is None else jnp.maximum(m_part, mp)
        m_col = jnp.max(m_part, axis=-1, keepdims=True)
        acc, l_col = _softmax_pv(sc_ref, m_col, v_ref.at[0, hd], seq)
        o_ref[0, :, hd * MLA_V:(hd + 1) * MLA_V] = (acc * (1.0 / l_col)).astype(o_ref.dtype)


def _mla_attn(q, knt, krt, v):
    bsz, nh, seq, _ = q.shape
    tq = ATTN_TQ
    return pl.pallas_call(
        _mla_kernel,
        grid=(bsz, seq // tq),
        in_specs=[
            pl.BlockSpec((1, nh, tq, 2 * LANE), lambda b, i: (b, 0, i, 0)),
            pl.BlockSpec((1, nh, LANE, seq), lambda b, i: (b, 0, 0, 0)),
            pl.BlockSpec((1, LANE, seq), lambda b, i: (b, 0, 0)),
            pl.BlockSpec((1, nh, seq, LANE), lambda b, i: (b, 0, 0, 0)),
        ],
        out_specs=pl.BlockSpec((1, tq, nh * MLA_V), lambda b, i: (b, i, 0)),
        out_shape=jax.ShapeDtypeStruct((bsz, seq, nh * MLA_V), BF16),
        scratch_shapes=[pltpu.VMEM((2, tq, seq), F32)],
        compiler_params=_params("arbitrary", "arbitrary"),
        name="mla_attn",
    )(q, knt, krt, v)


def _t5_bucket(rel):
    n = jnp.abs(rel)
    large = jnp.full(rel.shape, 8, jnp.int32)
    for t in T5_THRESHOLDS:
        large = large + jnp.where(n >= t, 1, 0)
    return jnp.where(rel > 0, REL_BUCKETS // 2, 0) + jnp.where(n < 8, n, large)


def _diff_kernel(pos_sm, tab_sm, dq_ref, dkt_ref, dv_ref, pcol_ref, prow_ref, tabt_ref,
                 lq1_ref, lk1_ref, lq2_ref, lk2_ref, subln_ref, o_ref,
                 s1_ref, s2_ref, bkt_ref, m1_ref, m2_ref, *, lambda_init):
    b = pl.program_id(0)
    qi = pl.program_id(1)
    hd = pl.program_id(2)
    tq, seq = s1_ref.shape
    nbc = seq // BIAS_CK

    q_lo = pos_sm[b, qi * tq]
    q_hi = pos_sm[b, qi * tq + tq - 1]
    is_neg, is_pos, cval = [], [], []
    for c in range(nbc):
        rel_hi = pos_sm[b, c * BIAS_CK + BIAS_CK - 1] - q_lo
        rel_lo = pos_sm[b, c * BIAS_CK] - q_hi
        neg = rel_hi <= -T5_SATURATE
        pos = rel_lo >= T5_SATURATE
        is_neg.append(neg)
        is_pos.append(pos)
        cval.append(jnp.where(neg, tab_sm[REL_BUCKETS // 2 - 1, hd],
                              jnp.where(pos, tab_sm[REL_BUCKETS - 1, hd], 0.0)) * LOG2E)

    @pl.when(hd == 0)
    def _():
        pq = pcol_ref[0]
        for c in range(nbc):
            @pl.when(jnp.logical_not(jnp.logical_or(is_neg[c], is_pos[c])))
            def _():
                pk = prow_ref[0, :, c * BIAS_CK:(c + 1) * BIAS_CK]
                bkt_ref[:, c * BIAS_CK:(c + 1) * BIAS_CK] = _t5_bucket(pk - pq)

    q = dq_ref[0, 0]
    lane = lax.broadcasted_iota(jnp.int32, q.shape, 1)
    zero = jnp.zeros_like(q)
    q1 = jnp.where(lane < DIFF_QK, q, zero)
    q2 = jnp.where(lane >= DIFF_QK, q, zero)

    neg_inf = jnp.full((tq, LANE), -jnp.inf, F32)
    m1 = neg_inf
    m2 = neg_inf
    for c in range(nbc):
        sl = slice(c * BIAS_CK, (c + 1) * BIAS_CK)
        kt = dkt_ref[0, 0, :, sl]
        sa = jnp.dot(q1, kt, preferred_element_type=F32) + cval[c]
        sb = jnp.dot(q2, kt, preferred_element_type=F32) + cval[c]
        s1_ref[:, sl] = sa
        s2_ref[:, sl] = sb
        const = jnp.logical_or(is_neg[c], is_pos[c])
        m1 = jnp.where(const, jnp.maximum(m1, _lane_tile_reduce(sa, jnp.maximum)), m1)
        m2 = jnp.where(const, jnp.maximum(m2, _lane_tile_reduce(sb, jnp.maximum)), m2)
    m1_ref[...] = m1
    m2_ref[...] = m2

    tab_b = jnp.broadcast_to(tabt_ref[pl.ds(hd, 1), :] * LOG2E, (tq, LANE))
    for c in range(nbc):
        @pl.when(jnp.logical_not(jnp.logical_or(is_neg[c], is_pos[c])))
        def _():
            for t in range(BIAS_CK // LANE):
                sl = slice(c * BIAS_CK + t * LANE, c * BIAS_CK + (t + 1) * LANE)
                bias = jnp.take_along_axis(tab_b, bkt_ref[:, sl], axis=1, mode="promise_in_bounds")
                va = s1_ref[:, sl] + bias
                vb = s2_ref[:, sl] + bias
                s1_ref[:, sl] = va
                s2_ref[:, sl] = vb
                m1_ref[...] = jnp.maximum(m1_ref[...], va)
                m2_ref[...] = jnp.maximum(m2_ref[...], vb)

    m1c = jnp.max(m1_ref[...], axis=-1, keepdims=True)
    m2c = jnp.max(m2_ref[...], axis=-1, keepdims=True)
    acc1, l1 = _softmax_pv(s1_ref, m1c, dv_ref.at[0, 0], seq)
    acc2, l2 = _softmax_pv(s2_ref, m2c, dv_ref.at[0, 0], seq)

    lam = (jnp.exp(jnp.sum(lq1_ref[...] * lk1_ref[...], axis=-1, keepdims=True))
           - jnp.exp(jnp.sum(lq2_ref[...] * lk2_ref[...], axis=-1, keepdims=True)) + lambda_init)
    o = acc1 * (1.0 / l1) - (acc2 * (1.0 / l2)) * lam
    o_ref[0] = (_rms(o, subln_ref[...]) * (1.0 - lambda_init)).astype(o_ref.dtype)


def _diff_attn(positions, rel_bias, dq, dkt, dv, pos_col, pos_row, tab_t, lq1, lk1, lq2, lk2, subln,
               lambda_init):
    bsz, nh, seq, _ = dq.shape
    tq = ATTN_TQ
    grid_spec = pltpu.PrefetchScalarGridSpec(
        num_scalar_prefetch=2,
        grid=(bsz, seq // tq, nh),
        in_specs=[
            pl.BlockSpec((1, 1, tq, LANE), lambda b, i, h, *_: (b, h, i, 0)),
            pl.BlockSpec((1, 1, LANE, seq), lambda b, i, h, *_: (b, h, 0, 0)),
            pl.BlockSpec((1, 1, seq, LANE), lambda b, i, h, *_: (b, h, 0, 0)),
            pl.BlockSpec((1, tq, 1), lambda b, i, h, *_: (b, i, 0)),
            pl.BlockSpec((1, 1, seq), lambda b, i, h, *_: (b, 0, 0)),
            pl.BlockSpec(tab_t.shape, lambda b, i, h, *_: (0, 0)),
            pl.BlockSpec((1, DIFF_QK), lambda b, i, h, *_: (0, 0)),
            pl.BlockSpec((1, DIFF_QK), lambda b, i, h, *_: (0, 0)),
            pl.BlockSpec((1, DIFF_QK), lambda b, i, h, *_: (0, 0)),
            pl.BlockSpec((1, DIFF_QK), lambda b, i, h, *_: (0, 0)),
            pl.BlockSpec((1, DIFF_V), lambda b, i, h, *_: (0, 0)),
        ],
        out_specs=pl.BlockSpec((1, tq, LANE), lambda b, i, h, *_: (b, i, h)),
        scratch_shapes=[
            pltpu.VMEM((tq, seq), F32),
            pltpu.VMEM((tq, seq), F32),
            pltpu.VMEM((tq, seq), jnp.int32),
            pltpu.VMEM((tq, LANE), F32),
            pltpu.VMEM((tq, LANE), F32),
        ],
    )
    return pl.pallas_call(
        partial(_diff_kernel, lambda_init=lambda_init),
        grid_spec=grid_spec,
        out_shape=jax.ShapeDtypeStruct((bsz, seq, nh * DIFF_V), BF16),
        compiler_params=_params("arbitrary", "arbitrary", "arbitrary"),
        name="diff_attn",
    )(positions, rel_bias, dq, dkt, dv, pos_col, pos_row, tab_t, lq1, lk1, lq2, lk2, subln)


def _out_proj_kernel(om_ref, od_ref, wa_ref, wb_ref, x_ref, g_ref, post_ref, sh_ref, sc_ref, pre_ref,
                     x1_ref, h2_ref):
    y = (jnp.dot(om_ref[...], wa_ref[...], preferred_element_type=F32)
         + jnp.dot(od_ref[...], wb_ref[...], preferred_element_type=F32))
    x1 = x_ref[...] + g_ref[0] * _rms(y, post_ref[...])
    x1_ref[...] = x1
    a_row = pre_ref[...] * (1.0 + sc_ref[0])
    ms = jnp.mean(x1 * x1, axis=-1, keepdims=True)
    h2_ref[...] = ((x1 * lax.rsqrt(ms + NORM_EPS)) * a_row + sh_ref[0]).astype(h2_ref.dtype)


def _out_proj(o_mla, o_diff, wo_a, wo_b, x2, g_a, post_w, sh_f, sc_f, pre_w, seq):
    m, d = x2.shape
    tm = OUTPROJ_TM
    tpb = seq // tm
    ka = o_mla.shape[1]
    kb = o_diff.shape[1]
    bmap3 = lambda i: (i // tpb, 0, 0)
    return pl.pallas_call(
        _out_proj_kernel,
        grid=(m // tm,),
        in_specs=[
            pl.BlockSpec((tm, ka), lambda i: (i, 0)),
            pl.BlockSpec((tm, kb), lambda i: (i, 0)),
            _resident(wo_a.shape),
            _resident(wo_b.shape),
            pl.BlockSpec((tm, d), lambda i: (i, 0)),
            pl.BlockSpec((1, 1, d), bmap3),
            _resident((1, d)),
            pl.BlockSpec((1, 1, d), bmap3),
            pl.BlockSpec((1, 1, d), bmap3),
            _resident((1, d)),
        ],
        out_specs=(pl.BlockSpec((tm, d), lambda i: (i, 0)), pl.BlockSpec((tm, d), lambda i: (i, 0))),
        out_shape=(jax.ShapeDtypeStruct((m, d), F32), jax.ShapeDtypeStruct((m, d), BF16)),
        compiler_params=_params("arbitrary"),
        name="out_proj",
    )(o_mla, o_diff, wo_a, wo_b, x2, g_a, post_w, sh_f, sc_f, pre_w)


def _gelu_tanh(g):
    return 0.5 * g * (1.0 + jnp.tanh(math.sqrt(2.0 / math.pi) * (g + 0.044715 * (g * g * g))))


def _ffn_up_kernel(h_ref, hp_ref, hn_ref, wg_ref, wv_ref, cwg_ref, cwv_ref, cbg_ref, cbv_ref, o_ref,
                   ug_ref, uv_ref, *, tiles_per_batch):
    i = pl.program_id(1)
    tm = h_ref.shape[0]
    first = (i % tiles_per_batch) == 0
    last = (i % tiles_per_batch) == tiles_per_batch - 1
    h = h_ref[...]
    hp = hp_ref[...]
    hn = hn_ref[...]

    def conv(w_ref, cw_ref, cb_ref, u_ref):
        w = w_ref[...]
        u_ref[SUBLANE:SUBLANE + tm, :] = jnp.dot(h, w, preferred_element_type=F32)
        up = jnp.dot(hp, w, preferred_element_type=F32)
        un = jnp.dot(hn, w, preferred_element_type=F32)
        u_ref[0:SUBLANE, :] = jnp.where(first, 0.0, up)
        u_ref[SUBLANE + tm:2 * SUBLANE + tm, :] = jnp.where(last, 0.0, un)
        return (cb_ref[...]
                + u_ref[SUBLANE - 1:SUBLANE - 1 + tm, :] * cw_ref[0:1, :]
                + u_ref[SUBLANE:SUBLANE + tm, :] * cw_ref[1:2, :]
                + u_ref[SUBLANE + 1:SUBLANE + 1 + tm, :] * cw_ref[2:3, :])

    gate = conv(wg_ref, cwg_ref, cbg_ref, ug_ref)
    val = conv(wv_ref, cwv_ref, cbv_ref, uv_ref)
    o_ref[...] = (_gelu_tanh(gate) * val).astype(o_ref.dtype)


def _ffn_up(h2, w_up, conv_w, conv_b, seq):
    m, d = h2.shape
    f = w_up.shape[1] // 2
    tm, tn = UP_TM, UP_TN
    tpb = seq // tm
    nj = f // tn
    rb = tm // SUBLANE
    nrb = m // SUBLANE
    return pl.pallas_call(
        partial(_ffn_up_kernel, tiles_per_batch=tpb),
        grid=(nj, m // tm),
        in_specs=[
            pl.BlockSpec((tm, d), lambda j, i: (i, 0)),
            pl.BlockSpec((SUBLANE, d), lambda j, i: (jnp.maximum(i * rb - 1, 0), 0)),
            pl.BlockSpec((SUBLANE, d), lambda j, i: (jnp.minimum((i + 1) * rb, nrb - 1), 0)),
            pl.BlockSpec((d, tn), lambda j, i: (0, j)),
            pl.BlockSpec((d, tn), lambda j, i: (0, j + nj)),
            pl.BlockSpec((3, tn), lambda j, i: (0, j)),
            pl.BlockSpec((3, tn), lambda j, i: (0, j + nj)),
            pl.BlockSpec((1, tn), lambda j, i: (0, j)),
            pl.BlockSpec((1, tn), lambda j, i: (0, j + nj)),
        ],
        out_specs=pl.BlockSpec((tm, tn), lambda j, i: (i, j)),
        out_shape=jax.ShapeDtypeStruct((m, f), BF16),
        scratch_shapes=[pltpu.VMEM((tm + 2 * SUBLANE, tn), F32), pltpu.VMEM((tm + 2 * SUBLANE, tn), F32)],
        compiler_params=_params("arbitrary", "arbitrary"),
        name="ffn_up",
    )(h2, h2, h2, w_up, w_up, conv_w, conv_w, conv_b, conv_b)


def _ffn_down_kernel(a_ref, w_ref, x1_ref, g_ref, post_ref, o_ref):
    y = jnp.dot(a_ref[...], w_ref[...], preferred_element_type=F32)
    o_ref[...] = x1_ref[...] + g_ref[0] * _rms(y, post_ref[...])


def _ffn_down(act, w_down, x1, g_f, post_w, seq):
    m, f = act.shape
    d = w_down.shape[1]
    tm = DOWN_TM
    tpb = seq // tm
    return pl.pallas_call(
        _ffn_down_kernel,
        grid=(m // tm,),
        in_specs=[
            pl.BlockSpec((tm, f), lambda i: (i, 0)),
            _resident(w_down.shape),
            pl.BlockSpec((tm, d), lambda i: (i, 0)),
            pl.BlockSpec((1, 1, d), lambda i: (i // tpb, 0, 0)),
            _resident((1, d)),
        ],
        out_specs=pl.BlockSpec((tm, d), lambda i: (i, 0)),
        out_shape=jax.ShapeDtypeStruct((m, d), F32),
        compiler_params=_params("arbitrary"),
        name="ffn_down",
    )(act, w_down, x1, g_f, post_w)


def _rope_pair_tiles(w_rope):
    half = MLA_ROPE // 2
    t1, t2 = w_rope[:, :half], w_rope[:, half:]
    z = jnp.zeros((w_rope.shape[0], LANE - MLA_ROPE), w_rope.dtype)
    return jnp.concatenate([t1, t2, z], axis=1), jnp.concatenate([t2, t1, z], axis=1)


def _layout_w_in(w_in):
    o_kv = MLA_Q_RANK
    o_kr = o_kv + MLA_KV_RANK
    o_dq = o_kr + MLA_ROPE
    o_dk = o_dq + DIFF_HEADS * 2 * DIFF_QK
    o_dv = o_dk + DIFF_HEADS * 2 * DIFF_QK
    w1 = jnp.concatenate([w_in[:, :o_kr], w_in[:, o_dq:o_dk], w_in[:, o_dv:]], axis=1).astype(BF16)
    ra, rb = _rope_pair_tiles(w_in[:, o_kr:o_dq])
    wt = jnp.concatenate([ra, rb, w_in[:, o_dk:o_dv]], axis=1).T.astype(BF16)
    return w1, wt


def _layout_w_uq(w_uq):
    k = w_uq.shape[0]
    w3 = w_uq.reshape(k, MLA_HEADS, MLA_NOPE + MLA_ROPE)
    blocks = []
    for hd in range(MLA_HEADS):
        ra, rb = _rope_pair_tiles(w3[:, hd, MLA_NOPE:])
        blocks += [w3[:, hd, :MLA_NOPE], ra, rb]
    return jnp.concatenate(blocks, axis=1).astype(BF16)


def _layout_w_ukv(w_ukv):
    k = w_ukv.shape[0]
    w3 = w_ukv.reshape(k, MLA_HEADS, MLA_NOPE + MLA_V)
    wukt = w3[:, :, :MLA_NOPE].reshape(k, MLA_HEADS * MLA_NOPE).T.astype(BF16)
    wuv = w3[:, :, MLA_NOPE:].reshape(k, MLA_HEADS * MLA_V).astype(BF16)
    return wukt, wuv


def _rope_constants():
    half = MLA_ROPE // 2
    inv = 1.0 / (ROPE_THETA ** (np.arange(0, MLA_ROPE, 2, dtype=np.float32) / MLA_ROPE))
    inv = inv.astype(np.float32)
    inv128 = np.zeros((LANE,), np.float32)
    inv128[:half] = inv
    inv128[half:2 * half] = inv
    sgn = np.zeros((LANE,), np.float32)
    sgn[:half] = -1.0
    sgn[half:2 * half] = 1.0
    return (jnp.asarray(inv128.reshape(1, LANE)), jnp.asarray(inv128.reshape(LANE, 1)),
            jnp.asarray(sgn.reshape(1, LANE)), jnp.asarray(sgn.reshape(LANE, 1)))


def kernel(x, c, positions, rel_bias, ada_w, ada_b, attn_pre_norm, attn_post_norm, w_in, q_norm, w_uq,
           kv_norm, w_ukv, lambda_q1, lambda_k1, lambda_q2, lambda_k2, diff_subln, w_o, ffn_pre_norm,
           ffn_post_norm, w_up, conv_w, conv_b, w_down):
    bsz, seq, d = x.shape
    depth = ada_w.shape[0]
    m = bsz * seq
    positions = positions.astype(jnp.int32)
    pos_col = positions.reshape(bsz, seq, 1)
    pos_row = positions.reshape(bsz, 1, seq)
    inv_row, inv_col, sg_row, sg_col = _rope_constants()
    tab_t = jnp.zeros((DIFF_HEADS, LANE), F32).at[:, :REL_BUCKETS].set(rel_bias.astype(F32).T)
    row = lambda v: v.reshape(1, -1).astype(F32)

    x2 = x.reshape(m, d)
    for l in range(depth):
        lambda_init = 0.8 - 0.6 * math.exp(-0.3 * l)
        mod = _ada_mod(c, ada_w[l], ada_b[l])
        sh_a, sc_a, g_a, sh_f, sc_f, g_f = [t.reshape(bsz, 1, d) for t in jnp.split(mod, N_MOD, axis=-1)]

        w1, wt = _layout_w_in(w_in[l])
        wuq = _layout_w_uq(w_uq[l])
        wukt, wuv = _layout_w_ukv(w_ukv[l])
        q, knt, krt, v, dq, dkt, dv = _in_proj(
            x2, pos_col, pos_row, sh_a, sc_a, row(attn_pre_norm[l]), w1, wt, row(q_norm[l]), wuq,
            row(kv_norm[l]), wukt, wuv, inv_row, inv_col, sg_row, sg_col, bsz, seq)

        o_mla = _mla_attn(q, knt, krt, v)
        o_diff = _diff_attn(positions, rel_bias.astype(F32), dq, dkt, dv, pos_col, pos_row, tab_t,
                            row(lambda_q1[l]), row(lambda_k1[l]), row(lambda_q2[l]), row(lambda_k2[l]),
                            row(diff_subln[l]), lambda_init)

        wo = w_o[l].astype(BF16)
        n_mla = MLA_HEADS * MLA_V
        x1, h2 = _out_proj(o_mla.reshape(m, n_mla), o_diff.reshape(m, DIFF_HEADS * DIFF_V),
                           wo[:n_mla], wo[n_mla:], x2, g_a, row(attn_post_norm[l]), sh_f, sc_f,
                           row(ffn_pre_norm[l]), seq)

        act = _ffn_up(h2, w_up[l].astype(BF16), conv_w[l].astype(F32), row(conv_b[l]), seq)
        x2 = _ffn_down(act, w_down[l].astype(BF16), x1, g_f, row(ffn_post_norm[l]), seq)
    return x2.reshape(bsz, seq, d)
```

```python
import math
from functools import partial

import jax
import jax.numpy as jnp
import numpy as np
from jax import lax
from jax.experimental import pallas as pl
from jax.experimental.pallas import tpu as pltpu

D_MODEL = 2048
MLA_HEADS = 8
MLA_Q_RANK = 512
MLA_KV_RANK = 256
MLA_NOPE = 128
MLA_ROPE = 64
MLA_V = 128
MLA_SCALE = (MLA_NOPE + MLA_ROPE) ** -0.5
DIFF_HEADS = 8
DIFF_QK = 64
DIFF_V = 128
DIFF_SCALE = DIFF_QK ** -0.5
D_FF = 5632
REL_BUCKETS = 32
ROPE_THETA = 10000.0
NORM_EPS = 1e-6
N_MOD = 6
LOG2E = math.log2(math.e)
T5_SATURATE = 128
T5_THRESHOLDS = (12, 16, 23, 32, 46, 64, 91)

LANE = 128
SUBLANE = 8
VMEM_LIMIT_BYTES = 56 * 1024 * 1024

ADA_TN = 1536
INPROJ_TM = 256
ATTN_TQ = 512
DIFF_TQ = 256
ATTN_CK = 512
ROW_IL = 8
UP_TN = 512
DOWN_TM = 256

F32 = jnp.float32
BF16 = jnp.bfloat16


def _params(*semantics):
    return pltpu.CompilerParams(dimension_semantics=semantics, vmem_limit_bytes=VMEM_LIMIT_BYTES)


def _resident(shape):
    nd = len(shape)
    return pl.BlockSpec(shape, lambda *_: (0,) * nd, pipeline_mode=pl.Buffered(1))


def _rms(x, w_row):
    ms = jnp.mean(x * x, axis=-1, keepdims=True)
    return (x * lax.rsqrt(ms + NORM_EPS)) * w_row


def _nt_dot(a, b):
    return lax.dot_general(a, b, (((1,), (1,)), ((), ())), preferred_element_type=F32)


def _ada_kernel(c_ref, w_ref, b_ref, o_ref):
    c = c_ref[...]
    c_act = (c * jax.nn.sigmoid(c)).astype(BF16)
    o_ref[...] = jnp.dot(c_act, w_ref[...].astype(BF16), preferred_element_type=F32) + b_ref[...]


def _ada_mod(c, ada_w, ada_b):
    bsz, d = c.shape
    n = ada_w.shape[1]
    return pl.pallas_call(
        _ada_kernel,
        grid=(n // ADA_TN,),
        in_specs=[
            pl.BlockSpec((bsz, d), lambda j: (0, 0)),
            pl.BlockSpec((d, ADA_TN), lambda j: (0, j)),
            pl.BlockSpec((1, ADA_TN), lambda j: (0, j)),
        ],
        out_specs=pl.BlockSpec((bsz, ADA_TN), lambda j: (0, j)),
        out_shape=jax.ShapeDtypeStruct((bsz, n), F32),
        compiler_params=_params("arbitrary"),
        name="ada_mod",
    )(c, ada_w, ada_b.reshape(1, n))


N_W1 = MLA_Q_RANK + MLA_KV_RANK + 2 * DIFF_HEADS * 2 * DIFF_QK
N_WT = 2 * LANE + DIFF_HEADS * 2 * DIFF_QK
Q_HEAD_COLS = 3 * LANE


def _in_proj_kernel(x_ref, pcol_ref, prow_ref, sh_ref, sc_ref, pre_ref, w1_ref, wt_ref,
                    qn_ref, wuq_ref, kvn_ref, wukt_ref, wuv_ref, invr_ref, invc_ref, sgr_ref, sgc_ref,
                    q_out, knt_out, krt_out, v_out, dq_out, dkt_out, dv_out):
    x = x_ref[...]
    a_row = pre_ref[...] * (1.0 + sc_ref[0])
    ms = jnp.mean(x * x, axis=-1, keepdims=True)
    h = ((x * lax.rsqrt(ms + NORM_EPS)) * a_row + sh_ref[0]).astype(BF16)

    p1 = jnp.dot(h, w1_ref[...], preferred_element_type=F32)
    p2t = _nt_dot(wt_ref[...], h)

    ang = pcol_ref[0].astype(F32) * invr_ref[...]
    cos_r = jnp.cos(ang)
    sin_r = jnp.sin(ang) * sgr_ref[...]
    ang_t = invc_ref[...] * prow_ref[0].astype(F32)
    cos_t = jnp.cos(ang_t)
    sin_t = jnp.sin(ang_t) * sgc_ref[...]

    q_lat = p1[:, 0:MLA_Q_RANK]
    rq = _rms(q_lat, qn_ref[...] * (MLA_SCALE * LOG2E)).astype(BF16)
    qf = jnp.dot(rq, wuq_ref[...], preferred_element_type=F32)
    for hd in range(MLA_HEADS):
        c0 = hd * Q_HEAD_COLS
        q_out[0, hd, :, 0:LANE] = qf[:, c0:c0 + LANE].astype(BF16)
        rope = qf[:, c0 + LANE:c0 + 2 * LANE] * cos_r + qf[:, c0 + 2 * LANE:c0 + 3 * LANE] * sin_r
        q_out[0, hd, :, LANE:2 * LANE] = rope.astype(BF16)

    kv_lat = p1[:, MLA_Q_RANK:MLA_Q_RANK + MLA_KV_RANK]
    rkv = _rms(kv_lat, kvn_ref[...]).astype(BF16)
    knt = _nt_dot(wukt_ref[...], rkv)
    vv = jnp.dot(rkv, wuv_ref[...], preferred_element_type=F32)
    for hd in range(MLA_HEADS):
        knt_out[0, hd] = knt[hd * LANE:(hd + 1) * LANE, :].astype(BF16)
        v_out[0, hd] = vv[:, hd * LANE:(hd + 1) * LANE].astype(BF16)
    krt_out[0] = (p2t[0:LANE, :] * cos_t + p2t[LANE:2 * LANE, :] * sin_t).astype(BF16)

    o_dq = MLA_Q_RANK + MLA_KV_RANK
    o_dv = o_dq + DIFF_HEADS * 2 * DIFF_QK
    for hd in range(DIFF_HEADS):
        dq_out[0, hd] = (p1[:, o_dq + hd * LANE:o_dq + (hd + 1) * LANE] * (DIFF_SCALE * LOG2E)).astype(BF16)
        dv_out[0, hd] = p1[:, o_dv + hd * LANE:o_dv + (hd + 1) * LANE].astype(BF16)
        dkt_out[0, hd] = p2t[2 * LANE + hd * LANE:2 * LANE + (hd + 1) * LANE, :].astype(BF16)


def _in_proj(x2, pos_col, pos_row, sh_a, sc_a, pre_w, w1, wt, q_norm, wuq, kv_norm, wukt, wuv,
             inv_row, inv_col, sg_row, sg_col, bsz, seq):
    tm = INPROJ_TM
    tpb = seq // tm
    d = x2.shape[1]
    bmap3 = lambda i: (i // tpb, 0, 0)
    out_shapes = (
        jax.ShapeDtypeStruct((bsz, MLA_HEADS, seq, 2 * LANE), BF16),
        jax.ShapeDtypeStruct((bsz, MLA_HEADS, LANE, seq), BF16),
        jax.ShapeDtypeStruct((bsz, LANE, seq), BF16),
        jax.ShapeDtypeStruct((bsz, MLA_HEADS, seq, LANE), BF16),
        jax.ShapeDtypeStruct((bsz, DIFF_HEADS, seq, LANE), BF16),
        jax.ShapeDtypeStruct((bsz, DIFF_HEADS, LANE, seq), BF16),
        jax.ShapeDtypeStruct((bsz, DIFF_HEADS, seq, LANE), BF16),
    )
    row_blk = lambda nh, w: pl.BlockSpec((1, nh, tm, w), lambda i: (i // tpb, 0, i % tpb, 0))
    col_blk = lambda nh: pl.BlockSpec((1, nh, LANE, tm), lambda i: (i // tpb, 0, 0, i % tpb))
    return pl.pallas_call(
        _in_proj_kernel,
        grid=(bsz * tpb,),
        in_specs=[
            pl.BlockSpec((tm, d), lambda i: (i, 0)),
            pl.BlockSpec((1, tm, 1), lambda i: (i // tpb, i % tpb, 0)),
            pl.BlockSpec((1, 1, tm), lambda i: (i // tpb, 0, i % tpb)),
            pl.BlockSpec((1, 1, d), bmap3),
            pl.BlockSpec((1, 1, d), bmap3),
            _resident((1, d)),
            _resident(w1.shape),
            _resident(wt.shape),
            _resident(q_norm.shape),
            _resident(wuq.shape),
            _resident(kv_norm.shape),
            _resident(wukt.shape),
            _resident(wuv.shape),
            _resident(inv_row.shape),
            _resident(inv_col.shape),
            _resident(sg_row.shape),
            _resident(sg_col.shape),
        ],
        out_specs=(
            row_blk(MLA_HEADS, 2 * LANE),
            col_blk(MLA_HEADS),
            pl.BlockSpec((1, LANE, tm), lambda i: (i // tpb, 0, i % tpb)),
            row_blk(MLA_HEADS, LANE),
            row_blk(DIFF_HEADS, LANE),
            col_blk(DIFF_HEADS),
            row_blk(DIFF_HEADS, LANE),
        ),
        out_shape=out_shapes,
        compiler_params=_params("arbitrary"),
        name="in_proj",
    )(x2, pos_col, pos_row, sh_a, sc_a, pre_w, w1, wt, q_norm, wuq, kv_norm, wukt, wuv,
      inv_row, inv_col, sg_row, sg_col)


def _lane_tile_reduce(x, op):
    out = x[:, 0:LANE]
    for t in range(1, x.shape[1] // LANE):
        out = op(out, x[:, t * LANE:(t + 1) * LANE])
    return out


def _pv_normalised(chunks, v_ref):
    ones = jnp.ones((ATTN_CK, LANE), BF16)
    acc = None
    for c, p in enumerate(chunks):
        v_aug = jnp.concatenate([v_ref[c * ATTN_CK:(c + 1) * ATTN_CK, :], ones], axis=1)
        pv = jnp.dot(p.astype(BF16), v_aug, preferred_element_type=F32)
        acc = pv if acc is None else acc + pv
    return acc[:, 0:LANE] * (1.0 / acc[:, LANE:2 * LANE])


def _softmax_pv(s_ref, m_col, v_ref, seq):
    return _pv_normalised(
        [jnp.exp2(s_ref[:, c * ATTN_CK:(c + 1) * ATTN_CK] - m_col) for c in range(seq // ATTN_CK)], v_ref)


def _mla_kernel(q_ref, knt_ref, krt_ref, v_ref, o_ref, s_ref):
    seq = s_ref.shape[2]
    for hd in range(MLA_HEADS):
        sc_ref = s_ref.at[hd % 2]
        q = q_ref[0, hd]
        m_part = None
        for c in range(seq // ATTN_CK):
            sl = slice(c * ATTN_CK, (c + 1) * ATTN_CK)
            kt = jnp.concatenate([knt_ref[0, hd, :, sl], krt_ref[0, :, sl]], axis=0)
            s = jnp.dot(q, kt, preferred_element_type=F32)
            sc_ref[:, sl] = s
            mp = _lane_tile_reduce(s, jnp.maximum)
            m_part = mp if m_part is None else jnp.maximum(m_part, mp)
        m_col = jnp.max(m_part, axis=-1, keepdims=True)
        o = _softmax_pv(sc_ref, m_col, v_ref.at[0, hd], seq)
        o_ref[0, :, hd * MLA_V:(hd + 1) * MLA_V] = o.astype(o_ref.dtype)


def _mla_attn(q, knt, krt, v):
    bsz, nh, seq, _ = q.shape
    tq = ATTN_TQ
    return pl.pallas_call(
        _mla_kernel,
        grid=(bsz, seq // tq),
        in_specs=[
            pl.BlockSpec((1, nh, tq, 2 * LANE), lambda b, i: (b, 0, i, 0)),
            pl.BlockSpec((1, nh, LANE, seq), lambda b, i: (b, 0, 0, 0)),
            pl.BlockSpec((1, LANE, seq), lambda b, i: (b, 0, 0)),
            pl.BlockSpec((1, nh, seq, LANE), lambda b, i: (b, 0, 0, 0)),
        ],
        out_specs=pl.BlockSpec((1, tq, nh * MLA_V), lambda b, i: (b, i, 0)),
        out_shape=jax.ShapeDtypeStruct((bsz, seq, nh * MLA_V), BF16),
        scratch_shapes=[pltpu.VMEM((2, tq, seq), F32)],
        compiler_params=_params("arbitrary", "arbitrary"),
        name="mla_attn",
    )(q, knt, krt, v)


def _t5_bucket(rel):
    n = jnp.abs(rel)
    large = jnp.full(rel.shape, 8, jnp.int32)
    for t in T5_THRESHOLDS:
        large = large + jnp.where(n >= t, 1, 0)
    return jnp.where(rel > 0, REL_BUCKETS // 2, 0) + jnp.where(n < 8, n, large)


def _diff_split_q(q):
    lane = lax.broadcasted_iota(jnp.int32, q.shape, 1)
    zero = jnp.zeros_like(q)
    return jnp.where(lane < DIFF_QK, q, zero), jnp.where(lane >= DIFF_QK, q, zero)


def _softmax_pv_tiles(s_ref, m_col, v_ref):
    per = ATTN_CK // LANE
    chunks = []
    for c in range(s_ref.shape[0] // per):
        sv = jnp.concatenate([s_ref[c * per + t] for t in range(per)], axis=1)
        chunks.append(jnp.exp2(sv - m_col))
    return _pv_normalised(chunks, v_ref)


def _diff_finish(s1_ref, s2_ref, m1, m2, v_ref, lam, subln_row, lambda_init):
    m1c = jnp.max(m1, axis=-1, keepdims=True)
    m2c = jnp.max(m2, axis=-1, keepdims=True)
    o = _softmax_pv_tiles(s1_ref, m1c, v_ref) - _softmax_pv_tiles(s2_ref, m2c, v_ref) * lam
    return _rms(o, subln_row) * (1.0 - lambda_init)


def _bias_rows(t_neg, t_pos, ws, nw, seq):
    shape = (2 * SUBLANE, seq)
    tile = lax.broadcasted_iota(jnp.int32, shape, 1) // LANE
    cv = jnp.where(tile < ws, t_neg, jnp.where(tile >= ws + nw, t_pos, 0.0))
    hi = cv.astype(BF16).astype(F32)
    row = lax.broadcasted_iota(jnp.int32, shape, 0)
    return jnp.where(row == 0, hi, jnp.where(row == 1, cv - hi, 0.0)).astype(BF16)


def _diff_kernel(pos_sm, tab_sm, dq_ref, dkt_ref, dv_ref, pcol_ref, pch_ref, tabt_ref,
                 lq1_ref, lk1_ref, lq2_ref, lk2_ref, subln_ref, o_ref,
                 sa0_ref, sb0_ref, sa1_ref, sb1_ref, bkt_ref, oslow_ref, *, lambda_init):
    b = pl.program_id(0)
    qi = pl.program_id(1)
    nch, tq, _ = sa0_ref.shape
    nw = tq // LANE + 2
    per = ATTN_CK // LANE

    q_lo = pos_sm[b, qi * tq]
    q_hi = pos_sm[b, qi * tq + tq - 1]
    n_neg = jnp.int32(0)
    n_pos = jnp.int32(0)
    for c in range(nch):
        n_neg = n_neg + (pos_sm[b, c * LANE + LANE - 1] - q_lo <= -T5_SATURATE).astype(jnp.int32)
        n_pos = n_pos + (pos_sm[b, c * LANE] - q_hi >= T5_SATURATE).astype(jnp.int32)
    fits = (nch - n_neg - n_pos) <= nw
    ws = jnp.minimum(n_neg, nch - nw)

    pq = pcol_ref[0]
    lam = (jnp.exp(jnp.sum(lq1_ref[...] * lk1_ref[...], axis=-1, keepdims=True))
           - jnp.exp(jnp.sum(lq2_ref[...] * lk2_ref[...], axis=-1, keepdims=True)) + lambda_init)
    subln_row = subln_ref[...]
    neg_inf = jnp.full((tq, LANE), -jnp.inf, F32)
    slots = ((sa0_ref, sb0_ref), (sa1_ref, sb1_ref))
    pick2 = jnp.where(lax.broadcasted_iota(jnp.int32, (tq, LANE), 1) < 2, 1.0, 0.0).astype(BF16)
    kt_pad = jnp.zeros((LANE - 2 * SUBLANE, ATTN_CK), BF16)

    def gather_bias(tab_b, bkt):
        return jnp.take_along_axis(tab_b, bkt, axis=1, mode="promise_in_bounds")

    @pl.when(fits)
    def _():
        for j in range(nw):
            bkt_ref[j] = _t5_bucket(pch_ref[0, ws + j] - pq)
        ones = jnp.ones((ATTN_CK, LANE), BF16)

        def head_operands(hd):
            t_neg = tab_sm[REL_BUCKETS // 2 - 1, hd] * LOG2E
            t_pos = tab_sm[REL_BUCKETS - 1, hd] * LOG2E
            q1, q2 = _diff_split_q(dq_ref[0, hd])
            return (jnp.concatenate([q1, pick2], axis=1), jnp.concatenate([q2, pick2], axis=1),
                    _bias_rows(t_neg, t_pos, ws, nw, nch * LANE))

        def scores_chunk(hd, c4, q1, q2, rows):
            s1_ref, s2_ref = slots[hd % len(slots)]
            sl = slice(c4 * ATTN_CK, (c4 + 1) * ATTN_CK)
            kt = jnp.concatenate([dkt_ref[0, hd, :, sl], rows[:, sl], kt_pad], axis=0)
            sa = jnp.dot(q1, kt, preferred_element_type=F32)
            sb = jnp.dot(q2, kt, preferred_element_type=F32)
            for t in range(per):
                s1_ref[c4 * per + t] = sa[:, t * LANE:(t + 1) * LANE]
                s2_ref[c4 * per + t] = sb[:, t * LANE:(t + 1) * LANE]

        def window_bias_and_max(hd, part, n_parts, m12):
            s1_ref, s2_ref = slots[hd % len(slots)]
            m1, m2 = m12
            tab_b = jnp.broadcast_to(tabt_ref[hd:hd + 1, :] * LOG2E, (tq, LANE))
            for j in range(part, nw, n_parts):
                c = ws + j
                bias = gather_bias(tab_b, bkt_ref[j])
                va = s1_ref[c] + bias
                vb = s2_ref[c] + bias
                s1_ref[c] = va
                s2_ref[c] = vb
                m1 = jnp.maximum(m1, va)
                m2 = jnp.maximum(m2, vb)
            for k in range(part, nch - nw, n_parts):
                c = jnp.where(k < ws, k, k + nw)
                m1 = jnp.maximum(m1, s1_ref[c])
                m2 = jnp.maximum(m2, s2_ref[c])
            return m1, m2

        def pv_chunk(hd, c4, m_cols, accs):
            v_aug = jnp.concatenate([dv_ref[0, hd, c4 * ATTN_CK:(c4 + 1) * ATTN_CK, :], ones], axis=1)
            out = []
            for s_ref, m_col, acc in zip(slots[hd % len(slots)], m_cols, accs):
                sv = jnp.concatenate([s_ref[c4 * per + t] for t in range(per)], axis=1)
                pv = jnp.dot(jnp.exp2(sv - m_col).astype(BF16), v_aug, preferred_element_type=F32)
                out.append(pv if acc is None else acc + pv)
            return out

        n_parts = nch // per
        col_max = lambda m12: tuple(jnp.max(m, axis=-1, keepdims=True) for m in m12)
        ops = head_operands(0)
        for c4 in range(n_parts):
            scores_chunk(0, c4, *ops)
        m_cols = col_max(window_bias_and_max(0, 0, 1, (neg_inf, neg_inf)))
        for hd in range(DIFF_HEADS):
            nxt = hd + 1
            if nxt < DIFF_HEADS:
                ops = head_operands(nxt)
            accs = [None, None]
            for c4 in range(n_parts):
                if nxt < DIFF_HEADS:
                    scores_chunk(nxt, c4, *ops)
                accs = pv_chunk(hd, c4, m_cols, accs)
            a1, a2 = accs
            o = a1[:, 0:LANE] * (1.0 / a1[:, LANE:2 * LANE]) - (a2[:, 0:LANE] * (1.0 / a2[:, LANE:2 * LANE])) * lam
            o_ref[0, :, hd * DIFF_V:(hd + 1) * DIFF_V] = (
                _rms(o, subln_row) * (1.0 - lambda_init)).astype(o_ref.dtype)
            if nxt < DIFF_HEADS:
                m_cols = col_max(window_bias_and_max(nxt, 0, 1, (neg_inf, neg_inf)))

    @pl.when(jnp.logical_not(fits))
    def _():
        for c in range(nch):
            bkt_ref[c] = _t5_bucket(pch_ref[0, c] - pq)

        def body(hd, carry):
            q1, q2 = _diff_split_q(dq_ref[0, hd])
            tab_b = jnp.broadcast_to(tabt_ref[pl.ds(hd, 1), :] * LOG2E, (tq, LANE))
            m1 = neg_inf
            m2 = neg_inf
            for c4 in range(nch // per):
                kt = dkt_ref[0, hd, :, c4 * ATTN_CK:(c4 + 1) * ATTN_CK]
                sa = jnp.dot(q1, kt, preferred_element_type=F32)
                sb = jnp.dot(q2, kt, preferred_element_type=F32)
                for t in range(per):
                    c = c4 * per + t
                    bias = gather_bias(tab_b, bkt_ref[c])
                    va = sa[:, t * LANE:(t + 1) * LANE] + bias
                    vb = sb[:, t * LANE:(t + 1) * LANE] + bias
                    sa0_ref[c] = va
                    sb0_ref[c] = vb
                    m1 = jnp.maximum(m1, va)
                    m2 = jnp.maximum(m2, vb)
            oslow_ref[hd] = _diff_finish(sa0_ref, sb0_ref, m1, m2, dv_ref.at[0, hd], lam, subln_row,
                                         lambda_init)
            return carry

        lax.fori_loop(0, DIFF_HEADS, body, 0)
        for hd in range(DIFF_HEADS):
            o_ref[0, :, hd * DIFF_V:(hd + 1) * DIFF_V] = oslow_ref[hd].astype(o_ref.dtype)


def _diff_attn(positions, rel_bias, dq, dkt, dv, pos_col, pos_tiles, tab_t, lq1, lk1, lq2, lk2, subln,
               lambda_init):
    bsz, nh, seq, _ = dq.shape
    tq = DIFF_TQ
    nch = seq // LANE
    const2 = lambda b, i, *_: (0, 0)
    once = pl.Buffered(1)
    grid_spec = pltpu.PrefetchScalarGridSpec(
        num_scalar_prefetch=2,
        grid=(bsz, seq // tq),
        in_specs=[
            pl.BlockSpec((1, nh, tq, LANE), lambda b, i, *_: (b, 0, i, 0)),
            pl.BlockSpec((1, nh, LANE, seq), lambda b, i, *_: (b, 0, 0, 0), pipeline_mode=once),
            pl.BlockSpec((1, nh, seq, LANE), lambda b, i, *_: (b, 0, 0, 0), pipeline_mode=once),
            pl.BlockSpec((1, tq, 1), lambda b, i, *_: (b, i, 0)),
            pl.BlockSpec((1, nch, 1, LANE), lambda b, i, *_: (b, 0, 0, 0)),
            pl.BlockSpec(tab_t.shape, const2),
            pl.BlockSpec((1, DIFF_QK), const2),
            pl.BlockSpec((1, DIFF_QK), const2),
            pl.BlockSpec((1, DIFF_QK), const2),
            pl.BlockSpec((1, DIFF_QK), const2),
            pl.BlockSpec((1, DIFF_V), const2),
        ],
        out_specs=pl.BlockSpec((1, tq, nh * DIFF_V), lambda b, i, *_: (b, i, 0)),
        scratch_shapes=[
            *([pltpu.VMEM((nch, tq, LANE), F32)] * 4),
            pltpu.VMEM((nch, tq, LANE), jnp.int32),
            pltpu.VMEM((nh, tq, LANE), F32),
        ],
    )
    return pl.pallas_call(
        partial(_diff_kernel, lambda_init=lambda_init),
        grid_spec=grid_spec,
        out_shape=jax.ShapeDtypeStruct((bsz, seq, nh * DIFF_V), BF16),
        compiler_params=_params("arbitrary", "arbitrary"),
        name="diff_attn",
    )(positions, rel_bias, dq, dkt, dv, pos_col, pos_tiles, tab_t, lq1, lk1, lq2, lk2, subln)


def _out_proj_kernel(om_ref, od_ref, wa_ref, wb_ref, x_ref, g_ref, post_ref, sh_ref, sc_ref, pre_ref,
                     x1_ref, h2_ref):
    y = (jnp.dot(om_ref[...], wa_ref[...], preferred_element_type=F32)
         + jnp.dot(od_ref[...], wb_ref[...], preferred_element_type=F32))
    x1 = x_ref[...] + g_ref[0] * _rms(y, post_ref[...])
    x1_ref[...] = x1
    a_row = pre_ref[...] * (1.0 + sc_ref[0])
    ms = jnp.mean(x1 * x1, axis=-1, keepdims=True)
    h2_ref[0] = ((x1 * lax.rsqrt(ms + NORM_EPS)) * a_row + sh_ref[0]).astype(h2_ref.dtype)


def _out_proj(o_mla, o_diff, wo_a, wo_b, x2, g_a, post_w, sh_f, sc_f, pre_w, seq):
    m, d = x2.shape
    tm = seq // ROW_IL
    ka = o_mla.shape[1]
    kb = o_diff.shape[1]
    bmap3 = lambda i: (i // ROW_IL, 0, 0)
    x1, h2 = pl.pallas_call(
        _out_proj_kernel,
        grid=(m // tm,),
        in_specs=[
            pl.BlockSpec((tm, ka), lambda i: (i, 0)),
            pl.BlockSpec((tm, kb), lambda i: (i, 0)),
            _resident(wo_a.shape),
            _resident(wo_b.shape),
            pl.BlockSpec((tm, d), lambda i: (i, 0)),
            pl.BlockSpec((1, 1, d), bmap3),
            _resident((1, d)),
            pl.BlockSpec((1, 1, d), bmap3),
            pl.BlockSpec((1, 1, d), bmap3),
            _resident((1, d)),
        ],
        out_specs=(pl.BlockSpec((tm, d), lambda i: (i, 0)),
                   pl.BlockSpec((1, tm, d), lambda i: (i // ROW_IL, 0, i % ROW_IL))),
        out_shape=(jax.ShapeDtypeStruct((m, d), F32),
                   jax.ShapeDtypeStruct((m // seq, tm, ROW_IL * d), BF16)),
        compiler_params=_params("arbitrary"),
        name="out_proj",
    )(o_mla, o_diff, wo_a, wo_b, x2, g_a, post_w, sh_f, sc_f, pre_w)
    return x1, h2.reshape(m, d)


def _gelu_tanh(g):
    a = -2.0 * math.sqrt(2.0 / math.pi) * LOG2E
    return g * (1.0 / (1.0 + jnp.exp2(g * (a + (a * 0.044715) * (g * g)))))


def _neighbour_rows(u, step):
    n = u.shape[0]
    row = lax.broadcasted_iota(jnp.int32, (ROW_IL, u.shape[1]), 0)
    if step == -1:
        wrap = jnp.where(row == 0, 0.0, pltpu.roll(u[n - ROW_IL:], 1, 0))
        return jnp.concatenate([wrap, u[:n - ROW_IL]], axis=0)
    wrap = jnp.where(row == ROW_IL - 1, 0.0, pltpu.roll(u[0:ROW_IL], ROW_IL - 1, 0))
    return jnp.concatenate([u[ROW_IL:], wrap], axis=0)


def _ffn_up_kernel(h_ref, wg_ref, wv_ref, cwg_ref, cwv_ref, cbg_ref, cbv_ref, o_ref):
    h = h_ref[...]

    def conv(u, cw_ref, cb_ref):
        return (cb_ref[...] + u * cw_ref[1:2, :]
                + _neighbour_rows(u, -1) * cw_ref[0:1, :] + _neighbour_rows(u, 1) * cw_ref[2:3, :])

    gate = conv(jnp.dot(h, wg_ref[...].astype(BF16), preferred_element_type=F32), cwg_ref, cbg_ref)
    val = conv(jnp.dot(h, wv_ref[...].astype(BF16), preferred_element_type=F32), cwv_ref, cbv_ref)
    o_ref[...] = (_gelu_tanh(gate) * val).astype(o_ref.dtype)


def _ffn_up(h2, w_up, conv_w, conv_b, seq):
    m, d = h2.shape
    f = w_up.shape[1] // 2
    tn = UP_TN
    nj = f // tn
    return pl.pallas_call(
        _ffn_up_kernel,
        grid=(nj, m // seq),
        in_specs=[
            pl.BlockSpec((seq, d), lambda j, i: (i, 0)),
            pl.BlockSpec((d, tn), lambda j, i: (0, j)),
            pl.BlockSpec((d, tn), lambda j, i: (0, j + nj)),
            pl.BlockSpec((3, tn), lambda j, i: (0, j)),
            pl.BlockSpec((3, tn), lambda j, i: (0, j + nj)),
            pl.BlockSpec((1, tn), lambda j, i: (0, j)),
            pl.BlockSpec((1, tn), lambda j, i: (0, j + nj)),
        ],
        out_specs=pl.BlockSpec((seq, tn), lambda j, i: (i, j)),
        out_shape=jax.ShapeDtypeStruct((m, f), BF16),
        compiler_params=_params("arbitrary", "arbitrary"),
        name="ffn_up",
    )(h2, w_up, w_up, conv_w, conv_w, conv_b, conv_b)


def _ffn_down_kernel(a_ref, w_ref, x1_ref, g_ref, post_ref, o_ref, y_ref):
    tm = a_ref.shape[0]
    y = jnp.dot(a_ref[...], w_ref[...], preferred_element_type=F32)
    yn = _rms(y, post_ref[...])
    g = g_ref[0]
    for j in range(y_ref.shape[0]):
        y_ref[j] = yn[:, j * LANE:(j + 1) * LANE]
    for s in range(ROW_IL):
        for j in range(y_ref.shape[0]):
            cs = slice(j * LANE, (j + 1) * LANE)
            rows = y_ref[j, pl.ds(s, tm // ROW_IL, stride=ROW_IL), :]
            o_ref[0, s, :, cs] = x1_ref[0, s, :, cs] + g[:, cs] * rows


def _ffn_down(act, w_down, x1, g_f, post_w, seq):
    m, f = act.shape
    d = w_down.shape[1]
    tm = DOWN_TM
    tpb = seq // tm
    bsz = m // seq
    nat_block = pl.BlockSpec((1, ROW_IL, tm // ROW_IL, d), lambda i: (i // tpb, 0, i % tpb, 0))
    out = pl.pallas_call(
        _ffn_down_kernel,
        grid=(m // tm,),
        in_specs=[
            pl.BlockSpec((tm, f), lambda i: (i, 0)),
            _resident(w_down.shape),
            nat_block,
            pl.BlockSpec((1, 1, d), lambda i: (i // tpb, 0, 0)),
            _resident((1, d)),
        ],
        out_specs=nat_block,
        out_shape=jax.ShapeDtypeStruct((bsz, ROW_IL, seq // ROW_IL, d), F32),
        scratch_shapes=[pltpu.VMEM((d // LANE, tm, LANE), F32)],
        compiler_params=_params("arbitrary"),
        name="ffn_down",
    )(act, w_down, x1.reshape(bsz, ROW_IL, seq // ROW_IL, d), g_f, post_w)
    return out.reshape(m, d)


def _rope_pair_tiles(w_rope):
    half = MLA_ROPE // 2
    t1, t2 = w_rope[:, :half], w_rope[:, half:]
    z = jnp.zeros((w_rope.shape[0], LANE - MLA_ROPE), w_rope.dtype)
    return jnp.concatenate([t1, t2, z], axis=1), jnp.concatenate([t2, t1, z], axis=1)


def _layout_w_in(w_in):
    o_kv = MLA_Q_RANK
    o_kr = o_kv + MLA_KV_RANK
    o_dq = o_kr + MLA_ROPE
    o_dk = o_dq + DIFF_HEADS * 2 * DIFF_QK
    o_dv = o_dk + DIFF_HEADS * 2 * DIFF_QK
    w1 = jnp.concatenate([w_in[:, :o_kr], w_in[:, o_dq:o_dk], w_in[:, o_dv:]], axis=1).astype(BF16)
    ra, rb = _rope_pair_tiles(w_in[:, o_kr:o_dq])
    wt = jnp.concatenate([ra, rb, w_in[:, o_dk:o_dv]], axis=1).T.astype(BF16)
    return w1, wt


def _layout_w_uq(w_uq):
    k = w_uq.shape[0]
    w3 = w_uq.reshape(k, MLA_HEADS, MLA_NOPE + MLA_ROPE)
    blocks = []
    for hd in range(MLA_HEADS):
        ra, rb = _rope_pair_tiles(w3[:, hd, MLA_NOPE:])
        blocks += [w3[:, hd, :MLA_NOPE], ra, rb]
    return jnp.concatenate(blocks, axis=1).astype(BF16)


def _layout_w_ukv(w_ukv):
    k = w_ukv.shape[0]
    w3 = w_ukv.reshape(k, MLA_HEADS, MLA_NOPE + MLA_V)
    wukt = w3[:, :, :MLA_NOPE].reshape(k, MLA_HEADS * MLA_NOPE).T.astype(BF16)
    wuv = w3[:, :, MLA_NOPE:].reshape(k, MLA_HEADS * MLA_V).astype(BF16)
    return wukt, wuv


def _rope_constants():
    half = MLA_ROPE // 2
    inv = 1.0 / (ROPE_THETA ** (np.arange(0, MLA_ROPE, 2, dtype=np.float32) / MLA_ROPE))
    inv = inv.astype(np.float32)
    inv128 = np.zeros((LANE,), np.float32)
    inv128[:half] = inv
    inv128[half:2 * half] = inv
    sgn = np.zeros((LANE,), np.float32)
    sgn[:half] = -1.0
    sgn[half:2 * half] = 1.0
    return (jnp.asarray(inv128.reshape(1, LANE)), jnp.asarray(inv128.reshape(LANE, 1)),
            jnp.asarray(sgn.reshape(1, LANE)), jnp.asarray(sgn.reshape(LANE, 1)))


def kernel(x, c, positions, rel_bias, ada_w, ada_b, attn_pre_norm, attn_post_norm, w_in, q_norm, w_uq,
           kv_norm, w_ukv, lambda_q1, lambda_k1, lambda_q2, lambda_k2, diff_subln, w_o, ffn_pre_norm,
           ffn_post_norm, w_up, conv_w, conv_b, w_down):
    bsz, seq, d = x.shape
    depth = ada_w.shape[0]
    m = bsz * seq
    positions = positions.astype(jnp.int32)
    pos_col = positions.reshape(bsz, seq, 1)
    pos_row = positions.reshape(bsz, 1, seq)
    pos_tiles = positions.reshape(bsz, seq // LANE, 1, LANE)
    inv_row, inv_col, sg_row, sg_col = _rope_constants()
    tab_t = jnp.zeros((DIFF_HEADS, LANE), F32).at[:, :REL_BUCKETS].set(rel_bias.astype(F32).T)
    row = lambda v: v.reshape(1, -1).astype(F32)

    x2 = x.reshape(m, d)
    for l in range(depth):
        lambda_init = 0.8 - 0.6 * math.exp(-0.3 * l)
        mod = _ada_mod(c, ada_w[l], ada_b[l])
        sh_a, sc_a, g_a, sh_f, sc_f, g_f = [t.reshape(bsz, 1, d) for t in jnp.split(mod, N_MOD, axis=-1)]

        w1, wt = _layout_w_in(w_in[l])
        wuq = _layout_w_uq(w_uq[l])
        wukt, wuv = _layout_w_ukv(w_ukv[l])
        q, knt, krt, v, dq, dkt, dv = _in_proj(
            x2, pos_col, pos_row, sh_a, sc_a, row(attn_pre_norm[l]), w1, wt, row(q_norm[l]), wuq,
            row(kv_norm[l]), wukt, wuv, inv_row, inv_col, sg_row, sg_col, bsz, seq)

        o_mla = _mla_attn(q, knt, krt, v)
        o_diff = _diff_attn(positions, rel_bias.astype(F32), dq, dkt, dv, pos_col, pos_tiles, tab_t,
                            row(lambda_q1[l]), row(lambda_k1[l]), row(lambda_q2[l]), row(lambda_k2[l]),
                            row(diff_subln[l]), lambda_init)

        wo = w_o[l].astype(BF16)
        n_mla = MLA_HEADS * MLA_V
        x1, h2 = _out_proj(o_mla.reshape(m, n_mla), o_diff.reshape(m, DIFF_HEADS * DIFF_V),
                           wo[:n_mla], wo[n_mla:], x2, g_a, row(attn_post_norm[l]), sh_f, sc_f,
                           row(ffn_pre_norm[l]), seq)

        act = _ffn_up(h2, w_up[l], conv_w[l].astype(F32), row(conv_b[l]), seq)
        x2 = _ffn_down(act, w_down[l].astype(BF16), x1, g_f, row(ffn_post_norm[l]), seq)
    return x2.reshape(bsz, seq, d)
```

```python
import math
from functools import partial

import jax
import jax.numpy as jnp
import numpy as np
from jax import lax
from jax.experimental import pallas as pl
from jax.experimental.pallas import tpu as pltpu

D_MODEL = 2048
MLA_HEADS = 8
MLA_Q_RANK = 512
MLA_KV_RANK = 256
MLA_NOPE = 128
MLA_ROPE = 64
MLA_V = 128
MLA_SCALE = (MLA_NOPE + MLA_ROPE) ** -0.5
DIFF_HEADS = 8
DIFF_QK = 64
DIFF_V = 128
DIFF_SCALE = DIFF_QK ** -0.5
D_FF = 5632
REL_BUCKETS = 32
ROPE_THETA = 10000.0
NORM_EPS = 1e-6
N_MOD = 6
LOG2E = math.log2(math.e)
T5_SATURATE = 128
T5_THRESHOLDS = (12, 16, 23, 32, 46, 64, 91)

LANE = 128
SUBLANE = 8
VMEM_LIMIT_BYTES = 56 * 1024 * 1024

ADA_TN = 1536
INPROJ_TM = 256
ATTN_TQ = 512
DIFF_TQ = 256
ATTN_CK = 512
OUTPROJ_TM = 512
ROW_IL = 8
UP_TN = 512
DOWN_TM = 256

F32 = jnp.float32
BF16 = jnp.bfloat16


def _params(*semantics):
    return pltpu.CompilerParams(dimension_semantics=semantics, vmem_limit_bytes=VMEM_LIMIT_BYTES)


def _resident(shape):
    nd = len(shape)
    return pl.BlockSpec(shape, lambda *_: (0,) * nd, pipeline_mode=pl.Buffered(1))


def _rms(x, w_row):
    ms = jnp.mean(x * x, axis=-1, keepdims=True)
    return (x * lax.rsqrt(ms + NORM_EPS)) * w_row


def _nt_dot(a, b):
    return lax.dot_general(a, b, (((1,), (1,)), ((), ())), preferred_element_type=F32)


def _ada_kernel(c_ref, w_ref, b_ref, o_ref):
    c = c_ref[...]
    c_act = (c * jax.nn.sigmoid(c)).astype(BF16)
    o_ref[...] = jnp.dot(c_act, w_ref[...].astype(BF16), preferred_element_type=F32) + b_ref[...]


def _ada_mod(c, ada_w, ada_b):
    bsz, d = c.shape
    n = ada_w.shape[1]
    return pl.pallas_call(
        _ada_kernel,
        grid=(n // ADA_TN,),
        in_specs=[
            pl.BlockSpec((bsz, d), lambda j: (0, 0)),
            pl.BlockSpec((d, ADA_TN), lambda j: (0, j)),
            pl.BlockSpec((1, ADA_TN), lambda j: (0, j)),
        ],
        out_specs=pl.BlockSpec((bsz, ADA_TN), lambda j: (0, j)),
        out_shape=jax.ShapeDtypeStruct((bsz, n), F32),
        compiler_params=_params("arbitrary"),
        name="ada_mod",
    )(c, ada_w, ada_b.reshape(1, n))


N_W1 = MLA_Q_RANK + MLA_KV_RANK + 2 * DIFF_HEADS * 2 * DIFF_QK
N_WT = 2 * LANE + DIFF_HEADS * 2 * DIFF_QK
Q_HEAD_COLS = 3 * LANE


def _in_proj_kernel(x_ref, pcol_ref, prow_ref, sh_ref, sc_ref, pre_ref, w1_ref, wt_ref,
                    qn_ref, wuq_ref, kvn_ref, wukt_ref, wuv_ref, invr_ref, invc_ref, sgr_ref, sgc_ref,
                    q_out, knt_out, krt_out, v_out, dq_out, dkt_out, dv_out):
    x = x_ref[...]
    a_row = pre_ref[...] * (1.0 + sc_ref[0])
    ms = jnp.mean(x * x, axis=-1, keepdims=True)
    h = ((x * lax.rsqrt(ms + NORM_EPS)) * a_row + sh_ref[0]).astype(BF16)

    p1 = jnp.dot(h, w1_ref[...], preferred_element_type=F32)
    p2t = _nt_dot(wt_ref[...], h)

    ang = pcol_ref[0].astype(F32) * invr_ref[...]
    cos_r = jnp.cos(ang)
    sin_r = jnp.sin(ang) * sgr_ref[...]
    ang_t = invc_ref[...] * prow_ref[0].astype(F32)
    cos_t = jnp.cos(ang_t)
    sin_t = jnp.sin(ang_t) * sgc_ref[...]

    q_lat = p1[:, 0:MLA_Q_RANK]
    rq = _rms(q_lat, qn_ref[...] * (MLA_SCALE * LOG2E)).astype(BF16)
    qf = jnp.dot(rq, wuq_ref[...], preferred_element_type=F32)
    for hd in range(MLA_HEADS):
        c0 = hd * Q_HEAD_COLS
        q_out[0, hd, :, 0:LANE] = qf[:, c0:c0 + LANE].astype(BF16)
        rope = qf[:, c0 + LANE:c0 + 2 * LANE] * cos_r + qf[:, c0 + 2 * LANE:c0 + 3 * LANE] * sin_r
        q_out[0, hd, :, LANE:2 * LANE] = rope.astype(BF16)

    kv_lat = p1[:, MLA_Q_RANK:MLA_Q_RANK + MLA_KV_RANK]
    rkv = _rms(kv_lat, kvn_ref[...]).astype(BF16)
    knt = _nt_dot(wukt_ref[...], rkv)
    vv = jnp.dot(rkv, wuv_ref[...], preferred_element_type=F32)
    for hd in range(MLA_HEADS):
        knt_out[0, hd] = knt[hd * LANE:(hd + 1) * LANE, :].astype(BF16)
        v_out[0, hd] = vv[:, hd * LANE:(hd + 1) * LANE].astype(BF16)
    krt_out[0] = (p2t[0:LANE, :] * cos_t + p2t[LANE:2 * LANE, :] * sin_t).astype(BF16)

    o_dq = MLA_Q_RANK + MLA_KV_RANK
    o_dv = o_dq + DIFF_HEADS * 2 * DIFF_QK
    for hd in range(DIFF_HEADS):
        dq_out[0, hd] = (p1[:, o_dq + hd * LANE:o_dq + (hd + 1) * LANE] * (DIFF_SCALE * LOG2E)).astype(BF16)
        dv_out[0, hd] = p1[:, o_dv + hd * LANE:o_dv + (hd + 1) * LANE].astype(BF16)
        dkt_out[0, hd] = p2t[2 * LANE + hd * LANE:2 * LANE + (hd + 1) * LANE, :].astype(BF16)


def _in_proj(x2, pos_col, pos_row, sh_a, sc_a, pre_w, w1, wt, q_norm, wuq, kv_norm, wukt, wuv,
             inv_row, inv_col, sg_row, sg_col, bsz, seq):
    tm = INPROJ_TM
    tpb = seq // tm
    d = x2.shape[1]
    bmap3 = lambda i: (i // tpb, 0, 0)
    out_shapes = (
        jax.ShapeDtypeStruct((bsz, MLA_HEADS, seq, 2 * LANE), BF16),
        jax.ShapeDtypeStruct((bsz, MLA_HEADS, LANE, seq), BF16),
        jax.ShapeDtypeStruct((bsz, LANE, seq), BF16),
        jax.ShapeDtypeStruct((bsz, MLA_HEADS, seq, LANE), BF16),
        jax.ShapeDtypeStruct((bsz, DIFF_HEADS, seq, LANE), BF16),
        jax.ShapeDtypeStruct((bsz, DIFF_HEADS, LANE, seq), BF16),
        jax.ShapeDtypeStruct((bsz, DIFF_HEADS, seq, LANE), BF16),
    )
    row_blk = lambda nh, w: pl.BlockSpec((1, nh, tm, w), lambda i: (i // tpb, 0, i % tpb, 0))
    col_blk = lambda nh: pl.BlockSpec((1, nh, LANE, tm), lambda i: (i // tpb, 0, 0, i % tpb))
    return pl.pallas_call(
        _in_proj_kernel,
        grid=(bsz * tpb,),
        in_specs=[
            pl.BlockSpec((tm, d), lambda i: (i, 0)),
            pl.BlockSpec((1, tm, 1), lambda i: (i // tpb, i % tpb, 0)),
            pl.BlockSpec((1, 1, tm), lambda i: (i // tpb, 0, i % tpb)),
            pl.BlockSpec((1, 1, d), bmap3),
            pl.BlockSpec((1, 1, d), bmap3),
            _resident((1, d)),
            _resident(w1.shape),
            _resident(wt.shape),
            _resident(q_norm.shape),
            _resident(wuq.shape),
            _resident(kv_norm.shape),
            _resident(wukt.shape),
            _resident(wuv.shape),
            _resident(inv_row.shape),
            _resident(inv_col.shape),
            _resident(sg_row.shape),
            _resident(sg_col.shape),
        ],
        out_specs=(
            row_blk(MLA_HEADS, 2 * LANE),
            col_blk(MLA_HEADS),
            pl.BlockSpec((1, LANE, tm), lambda i: (i // tpb, 0, i % tpb)),
            row_blk(MLA_HEADS, LANE),
            row_blk(DIFF_HEADS, LANE),
            col_blk(DIFF_HEADS),
            row_blk(DIFF_HEADS, LANE),
        ),
        out_shape=out_shapes,
        compiler_params=_params("arbitrary"),
        name="in_proj",
    )(x2, pos_col, pos_row, sh_a, sc_a, pre_w, w1, wt, q_norm, wuq, kv_norm, wukt, wuv,
      inv_row, inv_col, sg_row, sg_col)


def _lane_tile_reduce(x, op):
    out = x[:, 0:LANE]
    for t in range(1, x.shape[1] // LANE):
        out = op(out, x[:, t * LANE:(t + 1) * LANE])
    return out


def _pv_normalised(chunks, v_ref):
    ones = jnp.ones((ATTN_CK, LANE), BF16)
    acc = None
    for c, p in enumerate(chunks):
        v_aug = jnp.concatenate([v_ref[c * ATTN_CK:(c + 1) * ATTN_CK, :], ones], axis=1)
        pv = jnp.dot(p.astype(BF16), v_aug, preferred_element_type=F32)
        acc = pv if acc is None else acc + pv
    return acc[:, 0:LANE] * (1.0 / acc[:, LANE:2 * LANE])


def _softmax_pv(s_ref, m_col, v_ref, seq):
    return _pv_normalised(
        [jnp.exp2(s_ref[:, c * ATTN_CK:(c + 1) * ATTN_CK] - m_col) for c in range(seq // ATTN_CK)], v_ref)


def _mla_kernel(q_ref, knt_ref, krt_ref, v_ref, o_ref, s_ref):
    seq = s_ref.shape[2]
    for hd in range(MLA_HEADS):
        sc_ref = s_ref.at[hd % 2]
        q = q_ref[0, hd]
        m_part = None
        for c in range(seq // ATTN_CK):
            sl = slice(c * ATTN_CK, (c + 1) * ATTN_CK)
            kt = jnp.concatenate([knt_ref[0, hd, :, sl], krt_ref[0, :, sl]], axis=0)
            s = jnp.dot(q, kt, preferred_element_type=F32)
            sc_ref[:, sl] = s
            mp = _lane_tile_reduce(s, jnp.maximum)
            m_part = mp if m_part is None else jnp.maximum(m_part, mp)
        m_col = jnp.max(m_part, axis=-1, keepdims=True)
        o = _softmax_pv(sc_ref, m_col, v_ref.at[0, hd], seq)
        o_ref[0, :, hd * MLA_V:(hd + 1) * MLA_V] = o.astype(o_ref.dtype)


def _mla_attn(q, knt, krt, v):
    bsz, nh, seq, _ = q.shape
    tq = ATTN_TQ
    return pl.pallas_call(
        _mla_kernel,
        grid=(bsz, seq // tq),
        in_specs=[
            pl.BlockSpec((1, nh, tq, 2 * LANE), lambda b, i: (b, 0, i, 0)),
            pl.BlockSpec((1, nh, LANE, seq), lambda b, i: (b, 0, 0, 0)),
            pl.BlockSpec((1, LANE, seq), lambda b, i: (b, 0, 0)),
            pl.BlockSpec((1, nh, seq, LANE), lambda b, i: (b, 0, 0, 0)),
        ],
        out_specs=pl.BlockSpec((1, tq, nh * MLA_V), lambda b, i: (b, i, 0)),
        out_shape=jax.ShapeDtypeStruct((bsz, seq, nh * MLA_V), BF16),
        scratch_shapes=[pltpu.VMEM((2, tq, seq), F32)],
        compiler_params=_params("arbitrary", "arbitrary"),
        name="mla_attn",
    )(q, knt, krt, v)


def _t5_bucket(rel):
    n = jnp.abs(rel)
    large = jnp.full(rel.shape, 8, jnp.int32)
    for t in T5_THRESHOLDS:
        large = large + jnp.where(n >= t, 1, 0)
    return jnp.where(rel > 0, REL_BUCKETS // 2, 0) + jnp.where(n < 8, n, large)


def _diff_split_q(q):
    lane = lax.broadcasted_iota(jnp.int32, q.shape, 1)
    zero = jnp.zeros_like(q)
    return jnp.where(lane < DIFF_QK, q, zero), jnp.where(lane >= DIFF_QK, q, zero)


def _softmax_pv_tiles(s_ref, m_col, v_ref):
    per = ATTN_CK // LANE
    chunks = []
    for c in range(s_ref.shape[0] // per):
        sv = jnp.concatenate([s_ref[c * per + t] for t in range(per)], axis=1)
        chunks.append(jnp.exp2(sv - m_col))
    return _pv_normalised(chunks, v_ref)


def _diff_finish(s1_ref, s2_ref, m1, m2, v_ref, lam, subln_row, lambda_init):
    m1c = jnp.max(m1, axis=-1, keepdims=True)
    m2c = jnp.max(m2, axis=-1, keepdims=True)
    o = _softmax_pv_tiles(s1_ref, m1c, v_ref) - _softmax_pv_tiles(s2_ref, m2c, v_ref) * lam
    return _rms(o, subln_row) * (1.0 - lambda_init)


def _bias_rows(t_neg, t_pos, ws, nw, seq):
    shape = (2 * SUBLANE, seq)
    tile = lax.broadcasted_iota(jnp.int32, shape, 1) // LANE
    cv = jnp.where(tile < ws, t_neg, jnp.where(tile >= ws + nw, t_pos, 0.0))
    hi = cv.astype(BF16).astype(F32)
    row = lax.broadcasted_iota(jnp.int32, shape, 0)
    return jnp.where(row == 0, hi, jnp.where(row == 1, cv - hi, 0.0)).astype(BF16)


def _diff_kernel(pos_sm, tab_sm, dq_ref, dkt_ref, dv_ref, pcol_ref, pch_ref, tabt_ref,
                 lq1_ref, lk1_ref, lq2_ref, lk2_ref, subln_ref, o_ref,
                 sa0_ref, sb0_ref, sa1_ref, sb1_ref, bkt_ref, oslow_ref, *, lambda_init):
    b = pl.program_id(0)
    qi = pl.program_id(1)
    nch, tq, _ = sa0_ref.shape
    nw = tq // LANE + 2
    per = ATTN_CK // LANE

    q_lo = pos_sm[b, qi * tq]
    q_hi = pos_sm[b, qi * tq + tq - 1]
    n_neg = jnp.int32(0)
    n_pos = jnp.int32(0)
    for c in range(nch):
        n_neg = n_neg + (pos_sm[b, c * LANE + LANE - 1] - q_lo <= -T5_SATURATE).astype(jnp.int32)
        n_pos = n_pos + (pos_sm[b, c * LANE] - q_hi >= T5_SATURATE).astype(jnp.int32)
    fits = (nch - n_neg - n_pos) <= nw
    ws = jnp.minimum(n_neg, nch - nw)

    pq = pcol_ref[0]
    lam = (jnp.exp(jnp.sum(lq1_ref[...] * lk1_ref[...], axis=-1, keepdims=True))
           - jnp.exp(jnp.sum(lq2_ref[...] * lk2_ref[...], axis=-1, keepdims=True)) + lambda_init)
    subln_row = subln_ref[...]
    neg_inf = jnp.full((tq, LANE), -jnp.inf, F32)
    slots = ((sa0_ref, sb0_ref), (sa1_ref, sb1_ref))
    pick2 = jnp.where(lax.broadcasted_iota(jnp.int32, (tq, LANE), 1) < 2, 1.0, 0.0).astype(BF16)
    kt_pad = jnp.zeros((LANE - 2 * SUBLANE, ATTN_CK), BF16)

    def gather_bias(tab_b, bkt):
        return jnp.take_along_axis(tab_b, bkt, axis=1, mode="promise_in_bounds")

    @pl.when(fits)
    def _():
        for j in range(nw):
            bkt_ref[j] = _t5_bucket(pch_ref[0, ws + j] - pq)
        ones = jnp.ones((ATTN_CK, LANE), BF16)

        def head_operands(hd):
            t_neg = tab_sm[REL_BUCKETS // 2 - 1, hd] * LOG2E
            t_pos = tab_sm[REL_BUCKETS - 1, hd] * LOG2E
            q1, q2 = _diff_split_q(dq_ref[0, hd])
            return (jnp.concatenate([q1, pick2], axis=1), jnp.concatenate([q2, pick2], axis=1),
                    _bias_rows(t_neg, t_pos, ws, nw, nch * LANE))

        def scores_chunk(hd, c4, q1, q2, rows):
            s1_ref, s2_ref = slots[hd % len(slots)]
            sl = slice(c4 * ATTN_CK, (c4 + 1) * ATTN_CK)
            kt = jnp.concatenate([dkt_ref[0, hd, :, sl], rows[:, sl], kt_pad], axis=0)
            sa = jnp.dot(q1, kt, preferred_element_type=F32)
            sb = jnp.dot(q2, kt, preferred_element_type=F32)
            for t in range(per):
                s1_ref[c4 * per + t] = sa[:, t * LANE:(t + 1) * LANE]
                s2_ref[c4 * per + t] = sb[:, t * LANE:(t + 1) * LANE]

        def window_bias_and_max(hd, part, n_parts, m12):
            s1_ref, s2_ref = slots[hd % len(slots)]
            m1, m2 = m12
            tab_b = jnp.broadcast_to(tabt_ref[hd:hd + 1, :] * LOG2E, (tq, LANE))
            for j in range(part, nw, n_parts):
                c = ws + j
                bias = gather_bias(tab_b, bkt_ref[j])
                va = s1_ref[c] + bias
                vb = s2_ref[c] + bias
                s1_ref[c] = va
                s2_ref[c] = vb
                m1 = jnp.maximum(m1, va)
                m2 = jnp.maximum(m2, vb)
            for k in range(part, nch - nw, n_parts):
                c = jnp.where(k < ws, k, k + nw)
                m1 = jnp.maximum(m1, s1_ref[c])
                m2 = jnp.maximum(m2, s2_ref[c])
            return m1, m2

        def pv_chunk(hd, c4, m_cols, accs):
            v_aug = jnp.concatenate([dv_ref[0, hd, c4 * ATTN_CK:(c4 + 1) * ATTN_CK, :], ones], axis=1)
            out = []
            for s_ref, m_col, acc in zip(slots[hd % len(slots)], m_cols, accs):
                sv = jnp.concatenate([s_ref[c4 * per + t] for t in range(per)], axis=1)
                pv = jnp.dot(jnp.exp2(sv - m_col).astype(BF16), v_aug, preferred_element_type=F32)
                out.append(pv if acc is None else acc + pv)
            return out

        n_parts = nch // per
        col_max = lambda m12: tuple(jnp.max(m, axis=-1, keepdims=True) for m in m12)
        ops = head_operands(0)
        for c4 in range(n_parts):
            scores_chunk(0, c4, *ops)
        m_cols = col_max(window_bias_and_max(0, 0, 1, (neg_inf, neg_inf)))
        for hd in range(DIFF_HEADS):
            nxt = hd + 1
            if nxt < DIFF_HEADS:
                ops = head_operands(nxt)
            accs = [None, None]
            for c4 in range(n_parts):
                if nxt < DIFF_HEADS:
                    scores_chunk(nxt, c4, *ops)
                accs = pv_chunk(hd, c4, m_cols, accs)
            a1, a2 = accs
            o = a1[:, 0:LANE] * (1.0 / a1[:, LANE:2 * LANE]) - (a2[:, 0:LANE] * (1.0 / a2[:, LANE:2 * LANE])) * lam
            o_ref[0, :, hd * DIFF_V:(hd + 1) * DIFF_V] = (
                _rms(o, subln_row) * (1.0 - lambda_init)).astype(o_ref.dtype)
            if nxt < DIFF_HEADS:
                m_cols = col_max(window_bias_and_max(nxt, 0, 1, (neg_inf, neg_inf)))

    @pl.when(jnp.logical_not(fits))
    def _():
        for c in range(nch):
            bkt_ref[c] = _t5_bucket(pch_ref[0, c] - pq)

        def body(hd, carry):
            q1, q2 = _diff_split_q(dq_ref[0, hd])
            tab_b = jnp.broadcast_to(tabt_ref[pl.ds(hd, 1), :] * LOG2E, (tq, LANE))
            m1 = neg_inf
            m2 = neg_inf
            for c4 in range(nch // per):
                kt = dkt_ref[0, hd, :, c4 * ATTN_CK:(c4 + 1) * ATTN_CK]
                sa = jnp.dot(q1, kt, preferred_element_type=F32)
                sb = jnp.dot(q2, kt, preferred_element_type=F32)
                for t in range(per):
                    c = c4 * per + t
                    bias = gather_bias(tab_b, bkt_ref[c])
                    va = sa[:, t * LANE:(t + 1) * LANE] + bias
                    vb = sb[:, t * LANE:(t + 1) * LANE] + bias
                    sa0_ref[c] = va
                    sb0_ref[c] = vb
                    m1 = jnp.maximum(m1, va)
                    m2 = jnp.maximum(m2, vb)
            oslow_ref[hd] = _diff_finish(sa0_ref, sb0_ref, m1, m2, dv_ref.at[0, hd], lam, subln_row,
                                         lambda_init)
            return carry

        lax.fori_loop(0, DIFF_HEADS, body, 0)
        for hd in range(DIFF_HEADS):
            o_ref[0, :, hd * DIFF_V:(hd + 1) * DIFF_V] = oslow_ref[hd].astype(o_ref.dtype)


def _diff_attn(positions, rel_bias, dq, dkt, dv, pos_col, pos_tiles, tab_t, lq1, lk1, lq2, lk2, subln,
               lambda_init):
    bsz, nh, seq, _ = dq.shape
    tq = DIFF_TQ
    nch = seq // LANE
    const2 = lambda b, i, *_: (0, 0)
    once = pl.Buffered(1)
    grid_spec = pltpu.PrefetchScalarGridSpec(
        num_scalar_prefetch=2,
        grid=(bsz, seq // tq),
        in_specs=[
            pl.BlockSpec((1, nh, tq, LANE), lambda b, i, *_: (b, 0, i, 0)),
            pl.BlockSpec((1, nh, LANE, seq), lambda b, i, *_: (b, 0, 0, 0), pipeline_mode=once),
            pl.BlockSpec((1, nh, seq, LANE), lambda b, i, *_: (b, 0, 0, 0), pipeline_mode=once),
            pl.BlockSpec((1, tq, 1), lambda b, i, *_: (b, i, 0)),
            pl.BlockSpec((1, nch, 1, LANE), lambda b, i, *_: (b, 0, 0, 0)),
            pl.BlockSpec(tab_t.shape, const2),
            pl.BlockSpec((1, DIFF_QK), const2),
            pl.BlockSpec((1, DIFF_QK), const2),
            pl.BlockSpec((1, DIFF_QK), const2),
            pl.BlockSpec((1, DIFF_QK), const2),
            pl.BlockSpec((1, DIFF_V), const2),
        ],
        out_specs=pl.BlockSpec((1, tq, nh * DIFF_V), lambda b, i, *_: (b, i, 0)),
        scratch_shapes=[
            *([pltpu.VMEM((nch, tq, LANE), F32)] * 4),
            pltpu.VMEM((nch, tq, LANE), jnp.int32),
            pltpu.VMEM((nh, tq, LANE), F32),
        ],
    )
    return pl.pallas_call(
        partial(_diff_kernel, lambda_init=lambda_init),
        grid_spec=grid_spec,
        out_shape=jax.ShapeDtypeStruct((bsz, seq, nh * DIFF_V), BF16),
        compiler_params=_params("arbitrary", "arbitrary"),
        name="diff_attn",
    )(positions, rel_bias, dq, dkt, dv, pos_col, pos_tiles, tab_t, lq1, lk1, lq2, lk2, subln)


def _out_proj_kernel(om_ref, od_ref, wa_ref, wb_ref, x_ref, g_ref, post_ref, sh_ref, sc_ref, pre_ref,
                     x1_ref, h2_ref, il_ref):
    _, nph, rows, d = x_ref.shape
    tm = nph * rows
    merge = lambda ref: ref[0].reshape(tm, ref.shape[3])
    y = (jnp.dot(merge(om_ref), wa_ref[...], preferred_element_type=F32)
         + jnp.dot(merge(od_ref), wb_ref[...], preferred_element_type=F32))
    x1 = merge(x_ref) + g_ref[0] * _rms(y, post_ref[...])
    x1_ref[0] = x1.reshape(nph, rows, d)
    a_row = pre_ref[...] * (1.0 + sc_ref[0])
    ms = jnp.mean(x1 * x1, axis=-1, keepdims=True)
    h2 = (x1 * lax.rsqrt(ms + NORM_EPS)) * a_row + sh_ref[0]
    for j in range(il_ref.shape[0]):
        for s in range(nph):
            il_ref[j, pl.ds(s, rows, stride=nph), :] = h2[s * rows:(s + 1) * rows, j * LANE:(j + 1) * LANE]
    for j in range(il_ref.shape[0]):
        h2_ref[:, j * LANE:(j + 1) * LANE] = il_ref[j].astype(h2_ref.dtype)


def _out_proj(o_mla, o_diff, wo_a, wo_b, x2, g_a, post_w, sh_f, sc_f, pre_w, seq):
    m, d = x2.shape
    bsz = m // seq
    tm = OUTPROJ_TM
    rows = tm // ROW_IL
    tpb = seq // tm
    phases = lambda a: a.reshape(bsz, ROW_IL, seq // ROW_IL, a.shape[-1])
    ph_block = lambda w: pl.BlockSpec((1, ROW_IL, rows, w), lambda i: (i // tpb, 0, i % tpb, 0))
    bmap3 = lambda i: (i // tpb, 0, 0)
    x1, h2 = pl.pallas_call(
        _out_proj_kernel,
        grid=(m // tm,),
        in_specs=[
            ph_block(o_mla.shape[1]),
            ph_block(o_diff.shape[1]),
            _resident(wo_a.shape),
            _resident(wo_b.shape),
            ph_block(d),
            pl.BlockSpec((1, 1, d), bmap3),
            _resident((1, d)),
            pl.BlockSpec((1, 1, d), bmap3),
            pl.BlockSpec((1, 1, d), bmap3),
            _resident((1, d)),
        ],
        out_specs=(ph_block(d), pl.BlockSpec((tm, d), lambda i: (i, 0))),
        out_shape=(jax.ShapeDtypeStruct((bsz, ROW_IL, seq // ROW_IL, d), F32),
                   jax.ShapeDtypeStruct((m, d), BF16)),
        scratch_shapes=[pltpu.VMEM((d // LANE, tm, LANE), F32)],
        compiler_params=_params("arbitrary"),
        name="out_proj",
    )(phases(o_mla), phases(o_diff), wo_a, wo_b, phases(x2), g_a, post_w, sh_f, sc_f, pre_w)
    return x1.reshape(m, d), h2


def _gelu_tanh(g):
    a = -2.0 * math.sqrt(2.0 / math.pi) * LOG2E
    return g * (1.0 / (1.0 + jnp.exp2(g * (a + (a * 0.044715) * (g * g)))))


def _neighbour_rows(u, step):
    n = u.shape[0]
    row = lax.broadcasted_iota(jnp.int32, (ROW_IL, u.shape[1]), 0)
    if step == -1:
        wrap = jnp.where(row == 0, 0.0, pltpu.roll(u[n - ROW_IL:], 1, 0))
        return jnp.concatenate([wrap, u[:n - ROW_IL]], axis=0)
    wrap = jnp.where(row == ROW_IL - 1, 0.0, pltpu.roll(u[0:ROW_IL], ROW_IL - 1, 0))
    return jnp.concatenate([u[ROW_IL:], wrap], axis=0)


def _ffn_up_kernel(h_ref, wg_ref, wv_ref, cwg_ref, cwv_ref, cbg_ref, cbv_ref, o_ref):
    h = h_ref[...]

    def conv(u, cw_ref, cb_ref):
        return (cb_ref[...] + u * cw_ref[1:2, :]
                + _neighbour_rows(u, -1) * cw_ref[0:1, :] + _neighbour_rows(u, 1) * cw_ref[2:3, :])

    gate = conv(jnp.dot(h, wg_ref[...].astype(BF16), preferred_element_type=F32), cwg_ref, cbg_ref)
    val = conv(jnp.dot(h, wv_ref[...].astype(BF16), preferred_element_type=F32), cwv_ref, cbv_ref)
    o_ref[...] = (_gelu_tanh(gate) * val).astype(o_ref.dtype)


def _ffn_up(h2, w_up, conv_w, conv_b, seq):
    m, d = h2.shape
    f = w_up.shape[1] // 2
    tn = UP_TN
    nj = f // tn
    return pl.pallas_call(
        _ffn_up_kernel,
        grid=(nj, m // seq),
        in_specs=[
            pl.BlockSpec((seq, d), lambda j, i: (i, 0)),
            pl.BlockSpec((d, tn), lambda j, i: (0, j)),
            pl.BlockSpec((d, tn), lambda j, i: (0, j + nj)),
            pl.BlockSpec((3, tn), lambda j, i: (0, j)),
            pl.BlockSpec((3, tn), lambda j, i: (0, j + nj)),
            pl.BlockSpec((1, tn), lambda j, i: (0, j)),
            pl.BlockSpec((1, tn), lambda j, i: (0, j + nj)),
        ],
        out_specs=pl.BlockSpec((seq, tn), lambda j, i: (i, j)),
        out_shape=jax.ShapeDtypeStruct((m, f), BF16),
        compiler_params=_params("arbitrary", "arbitrary"),
        name="ffn_up",
    )(h2, w_up, w_up, conv_w, conv_w, conv_b, conv_b)


def _ffn_down_kernel(a_ref, w_ref, x1_ref, g_ref, post_ref, o_ref, y_ref):
    tm = a_ref.shape[0]
    y = jnp.dot(a_ref[...], w_ref[...], preferred_element_type=F32)
    yn = _rms(y, post_ref[...])
    g = g_ref[0]
    for j in range(y_ref.shape[0]):
        y_ref[j] = yn[:, j * LANE:(j + 1) * LANE]
    for s in range(ROW_IL):
        for j in range(y_ref.shape[0]):
            cs = slice(j * LANE, (j + 1) * LANE)
            rows = y_ref[j, pl.ds(s, tm // ROW_IL, stride=ROW_IL), :]
            o_ref[0, s, :, cs] = x1_ref[0, s, :, cs] + g[:, cs] * rows


def _ffn_down(act, w_down, x1, g_f, post_w, seq):
    m, f = act.shape
    d = w_down.shape[1]
    tm = DOWN_TM
    tpb = seq // tm
    bsz = m // seq
    nat_block = pl.BlockSpec((1, ROW_IL, tm // ROW_IL, d), lambda i: (i // tpb, 0, i % tpb, 0))
    out = pl.pallas_call(
        _ffn_down_kernel,
        grid=(m // tm,),
        in_specs=[
            pl.BlockSpec((tm, f), lambda i: (i, 0)),
            _resident(w_down.shape),
            nat_block,
            pl.BlockSpec((1, 1, d), lambda i: (i // tpb, 0, 0)),
            _resident((1, d)),
        ],
        out_specs=nat_block,
        out_shape=jax.ShapeDtypeStruct((bsz, ROW_IL, seq // ROW_IL, d), F32),
        scratch_shapes=[pltpu.VMEM((d // LANE, tm, LANE), F32)],
        compiler_params=_params("arbitrary"),
        name="ffn_down",
    )(act, w_down, x1.reshape(bsz, ROW_IL, seq // ROW_IL, d), g_f, post_w)
    return out.reshape(m, d)


def _rope_pair_tiles(w_rope):
    half = MLA_ROPE // 2
    t1, t2 = w_rope[:, :half], w_rope[:, half:]
    z = jnp.zeros((w_rope.shape[0], LANE - MLA_ROPE), w_rope.dtype)
    return jnp.concatenate([t1, t2, z], axis=1), jnp.concatenate([t2, t1, z], axis=1)


def _layout_w_in(w_in):
    o_kv = MLA_Q_RANK
    o_kr = o_kv + MLA_KV_RANK
    o_dq = o_kr + MLA_ROPE
    o_dk = o_dq + DIFF_HEADS * 2 * DIFF_QK
    o_dv = o_dk + DIFF_HEADS * 2 * DIFF_QK
    w1 = jnp.concatenate([w_in[:, :o_kr], w_in[:, o_dq:o_dk], w_in[:, o_dv:]], axis=1).astype(BF16)
    ra, rb = _rope_pair_tiles(w_in[:, o_kr:o_dq])
    wt = jnp.concatenate([ra, rb, w_in[:, o_dk:o_dv]], axis=1).T.astype(BF16)
    return w1, wt


def _layout_w_uq(w_uq):
    k = w_uq.shape[0]
    w3 = w_uq.reshape(k, MLA_HEADS, MLA_NOPE + MLA_ROPE)
    blocks = []
    for hd in range(MLA_HEADS):
        ra, rb = _rope_pair_tiles(w3[:, hd, MLA_NOPE:])
        blocks += [w3[:, hd, :MLA_NOPE], ra, rb]
    return jnp.concatenate(blocks, axis=1).astype(BF16)


def _layout_w_ukv(w_ukv):
    k = w_ukv.shape[0]
    w3 = w_ukv.reshape(k, MLA_HEADS, MLA_NOPE + MLA_V)
    wukt = w3[:, :, :MLA_NOPE].reshape(k, MLA_HEADS * MLA_NOPE).T.astype(BF16)
    wuv = w3[:, :, MLA_NOPE:].reshape(k, MLA_HEADS * MLA_V).astype(BF16)
    return wukt, wuv


def _rope_constants():
    half = MLA_ROPE // 2
    inv = 1.0 / (ROPE_THETA ** (np.arange(0, MLA_ROPE, 2, dtype=np.float32) / MLA_ROPE))
    inv = inv.astype(np.float32)
    inv128 = np.zeros((LANE,), np.float32)
    inv128[:half] = inv
    inv128[half:2 * half] = inv
    sgn = np.zeros((LANE,), np.float32)
    sgn[:half] = -1.0
    sgn[half:2 * half] = 1.0
    return (jnp.asarray(inv128.reshape(1, LANE)), jnp.asarray(inv128.reshape(LANE, 1)),
            jnp.asarray(sgn.reshape(1, LANE)), jnp.asarray(sgn.reshape(LANE, 1)))


def kernel(x, c, positions, rel_bias, ada_w, ada_b, attn_pre_norm, attn_post_norm, w_in, q_norm, w_uq,
           kv_norm, w_ukv, lambda_q1, lambda_k1, lambda_q2, lambda_k2, diff_subln, w_o, ffn_pre_norm,
           ffn_post_norm, w_up, conv_w, conv_b, w_down):
    bsz, seq, d = x.shape
    depth = ada_w.shape[0]
    m = bsz * seq
    positions = positions.astype(jnp.int32)
    pos_col = positions.reshape(bsz, seq, 1)
    pos_row = positions.reshape(bsz, 1, seq)
    pos_tiles = positions.reshape(bsz, seq // LANE, 1, LANE)
    inv_row, inv_col, sg_row, sg_col = _rope_constants()
    tab_t = jnp.zeros((DIFF_HEADS, LANE), F32).at[:, :REL_BUCKETS].set(rel_bias.astype(F32).T)
    row = lambda v: v.reshape(1, -1).astype(F32)

    x2 = x.reshape(m, d)
    for l in range(depth):
        lambda_init = 0.8 - 0.6 * math.exp(-0.3 * l)
        mod = _ada_mod(c, ada_w[l], ada_b[l])
        sh_a, sc_a, g_a, sh_f, sc_f, g_f = [t.reshape(bsz, 1, d) for t in jnp.split(mod, N_MOD, axis=-1)]

        w1, wt = _layout_w_in(w_in[l])
        wuq = _layout_w_uq(w_uq[l])
        wukt, wuv = _layout_w_ukv(w_ukv[l])
        q, knt, krt, v, dq, dkt, dv = _in_proj(
            x2, pos_col, pos_row, sh_a, sc_a, row(attn_pre_norm[l]), w1, wt, row(q_norm[l]), wuq,
            row(kv_norm[l]), wukt, wuv, inv_row, inv_col, sg_row, sg_col, bsz, seq)

        o_mla = _mla_attn(q, knt, krt, v)
        o_diff = _diff_attn(positions, rel_bias.astype(F32), dq, dkt, dv, pos_col, pos_tiles, tab_t,
                            row(lambda_q1[l]), row(lambda_k1[l]), row(lambda_q2[l]), row(lambda_k2[l]),
                            row(diff_subln[l]), lambda_init)

        wo = w_o[l].astype(BF16)
        n_mla = MLA_HEADS * MLA_V
        x1, h2 = _out_proj(o_mla.reshape(m, n_mla), o_diff.reshape(m, DIFF_HEADS * DIFF_V),
                           wo[:n_mla], wo[n_mla:], x2, g_a, row(attn_post_norm[l]), sh_f, sc_f,
                           row(ffn_pre_norm[l]), seq)

        act = _ffn_up(h2, w_up[l], conv_w[l].astype(F32), row(conv_b[l]), seq)
        x2 = _ffn_down(act, w_down[l].astype(BF16), x1, g_f, row(ffn_post_norm[l]), seq)
    return x2.reshape(bsz, seq, d)
```

```python
import math
from functools import partial

import jax
import jax.numpy as jnp
import numpy as np
from jax import lax
from jax.experimental import pallas as pl
from jax.experimental.pallas import tpu as pltpu

D_MODEL = 2048
MLA_HEADS = 8
MLA_Q_RANK = 512
MLA_KV_RANK = 256
MLA_NOPE = 128
MLA_ROPE = 64
MLA_V = 128
MLA_SCALE = (MLA_NOPE + MLA_ROPE) ** -0.5
DIFF_HEADS = 8
DIFF_QK = 64
DIFF_V = 128
DIFF_SCALE = DIFF_QK ** -0.5
D_FF = 5632
REL_BUCKETS = 32
ROPE_THETA = 10000.0
NORM_EPS = 1e-6
N_MOD = 6
LOG2E = math.log2(math.e)
T5_SATURATE = 128
T5_THRESHOLDS = (12, 16, 23, 32, 46, 64, 91)

LANE = 128
SUBLANE = 8
VMEM_LIMIT_BYTES = 56 * 1024 * 1024

ADA_TN = 1536
INPROJ_TM = 512
ATTN_TQ = 512
DIFF_TQ = 256
ATTN_CK = 512
OUTPROJ_TM = 512
ROW_IL = 8
UP_TN = 512
DOWN_TM = 256

F32 = jnp.float32
BF16 = jnp.bfloat16


def _params(*semantics):
    return pltpu.CompilerParams(dimension_semantics=semantics, vmem_limit_bytes=VMEM_LIMIT_BYTES)


def _resident(shape):
    nd = len(shape)
    return pl.BlockSpec(shape, lambda *_: (0,) * nd, pipeline_mode=pl.Buffered(1))


def _rms(x, w_row):
    ms = jnp.mean(x * x, axis=-1, keepdims=True)
    return (x * lax.rsqrt(ms + NORM_EPS)) * w_row


def _nt_dot(a, b):
    return lax.dot_general(a, b, (((1,), (1,)), ((), ())), preferred_element_type=F32)


def _ada_kernel(c_ref, w_ref, b_ref, o_ref):
    c = c_ref[...]
    c_act = (c * jax.nn.sigmoid(c)).astype(BF16)
    o_ref[...] = jnp.dot(c_act, w_ref[...].astype(BF16), preferred_element_type=F32) + b_ref[...]


def _ada_mod(c, ada_w, ada_b):
    bsz, d = c.shape
    n = ada_w.shape[1]
    return pl.pallas_call(
        _ada_kernel,
        grid=(n // ADA_TN,),
        in_specs=[
            pl.BlockSpec((bsz, d), lambda j: (0, 0)),
            pl.BlockSpec((d, ADA_TN), lambda j: (0, j)),
            pl.BlockSpec((1, ADA_TN), lambda j: (0, j)),
        ],
        out_specs=pl.BlockSpec((bsz, ADA_TN), lambda j: (0, j)),
        out_shape=jax.ShapeDtypeStruct((bsz, n), F32),
        compiler_params=_params("arbitrary"),
        name="ada_mod",
    )(c, ada_w, ada_b.reshape(1, n))


N_W1 = MLA_Q_RANK + MLA_KV_RANK + 2 * DIFF_HEADS * 2 * DIFF_QK
N_WT = 2 * LANE + DIFF_HEADS * 2 * DIFF_QK
Q_HEAD_COLS = 3 * LANE


def _in_proj_kernel(x_ref, cr_ref, sr_ref, ct_ref, st_ref, sh_ref, sc_ref, pre_ref, w1_ref, wt_ref,
                    qn_ref, wuq_ref, kvn_ref, wukt_ref, wuv_ref,
                    q_out, knt_out, krt_out, v_out, dq_out, dkt_out, dv_out):
    x = x_ref[...]
    a_row = pre_ref[...] * (1.0 + sc_ref[0])
    ms = jnp.mean(x * x, axis=-1, keepdims=True)
    h = ((x * lax.rsqrt(ms + NORM_EPS)) * a_row + sh_ref[0]).astype(BF16)

    p1 = jnp.dot(h, w1_ref[...], preferred_element_type=F32)
    p2t = _nt_dot(wt_ref[...], h)

    cos_r, sin_r = cr_ref[0], sr_ref[0]
    cos_t, sin_t = ct_ref[0], st_ref[0]

    q_lat = p1[:, 0:MLA_Q_RANK]
    rq = _rms(q_lat, qn_ref[...] * (MLA_SCALE * LOG2E)).astype(BF16)
    qf = jnp.dot(rq, wuq_ref[...], preferred_element_type=F32)
    for hd in range(MLA_HEADS):
        c0 = hd * Q_HEAD_COLS
        q_out[0, hd, :, 0:LANE] = qf[:, c0:c0 + LANE].astype(BF16)
        rope = qf[:, c0 + LANE:c0 + 2 * LANE] * cos_r + qf[:, c0 + 2 * LANE:c0 + 3 * LANE] * sin_r
        q_out[0, hd, :, LANE:2 * LANE] = rope.astype(BF16)

    kv_lat = p1[:, MLA_Q_RANK:MLA_Q_RANK + MLA_KV_RANK]
    rkv = _rms(kv_lat, kvn_ref[...]).astype(BF16)
    knt = _nt_dot(wukt_ref[...], rkv)
    vv = jnp.dot(rkv, wuv_ref[...], preferred_element_type=F32)
    for hd in range(MLA_HEADS):
        knt_out[0, hd] = knt[hd * LANE:(hd + 1) * LANE, :].astype(BF16)
        v_out[0, hd] = vv[:, hd * LANE:(hd + 1) * LANE].astype(BF16)
    krt_out[0] = (p2t[0:LANE, :] * cos_t + p2t[LANE:2 * LANE, :] * sin_t).astype(BF16)

    o_dq = MLA_Q_RANK + MLA_KV_RANK
    o_dv = o_dq + DIFF_HEADS * 2 * DIFF_QK
    for hd in range(DIFF_HEADS):
        dq_out[0, hd] = (p1[:, o_dq + hd * LANE:o_dq + (hd + 1) * LANE] * (DIFF_SCALE * LOG2E)).astype(BF16)
        dv_out[0, hd] = p1[:, o_dv + hd * LANE:o_dv + (hd + 1) * LANE].astype(BF16)
        dkt_out[0, hd] = p2t[2 * LANE + hd * LANE:2 * LANE + (hd + 1) * LANE, :].astype(BF16)


def _in_proj(x2, rope, sh_a, sc_a, pre_w, w1, wt, q_norm, wuq, kv_norm, wukt, wuv, bsz, seq):
    tm = INPROJ_TM
    tpb = seq // tm
    d = x2.shape[1]
    bmap3 = lambda i: (i // tpb, 0, 0)
    out_shapes = (
        jax.ShapeDtypeStruct((bsz, MLA_HEADS, seq, 2 * LANE), BF16),
        jax.ShapeDtypeStruct((bsz, MLA_HEADS, LANE, seq), BF16),
        jax.ShapeDtypeStruct((bsz, LANE, seq), BF16),
        jax.ShapeDtypeStruct((bsz, MLA_HEADS, seq, LANE), BF16),
        jax.ShapeDtypeStruct((bsz, DIFF_HEADS, seq, LANE), BF16),
        jax.ShapeDtypeStruct((bsz, DIFF_HEADS, LANE, seq), BF16),
        jax.ShapeDtypeStruct((bsz, DIFF_HEADS, seq, LANE), BF16),
    )
    row_blk = lambda nh, w: pl.BlockSpec((1, nh, tm, w), lambda i: (i // tpb, 0, i % tpb, 0))
    col_blk = lambda nh: pl.BlockSpec((1, nh, LANE, tm), lambda i: (i // tpb, 0, 0, i % tpb))
    return pl.pallas_call(
        _in_proj_kernel,
        grid=(bsz * tpb,),
        in_specs=[
            pl.BlockSpec((tm, d), lambda i: (i, 0)),
            pl.BlockSpec((1, tm, LANE), lambda i: (i // tpb, i % tpb, 0)),
            pl.BlockSpec((1, tm, LANE), lambda i: (i // tpb, i % tpb, 0)),
            pl.BlockSpec((1, LANE, tm), lambda i: (i // tpb, 0, i % tpb)),
            pl.BlockSpec((1, LANE, tm), lambda i: (i // tpb, 0, i % tpb)),
            pl.BlockSpec((1, 1, d), bmap3),
            pl.BlockSpec((1, 1, d), bmap3),
            _resident((1, d)),
            _resident(w1.shape),
            _resident(wt.shape),
            _resident(q_norm.shape),
            _resident(wuq.shape),
            _resident(kv_norm.shape),
            _resident(wukt.shape),
            _resident(wuv.shape),
        ],
        out_specs=(
            row_blk(MLA_HEADS, 2 * LANE),
            col_blk(MLA_HEADS),
            pl.BlockSpec((1, LANE, tm), lambda i: (i // tpb, 0, i % tpb)),
            row_blk(MLA_HEADS, LANE),
            row_blk(DIFF_HEADS, LANE),
            col_blk(DIFF_HEADS),
            row_blk(DIFF_HEADS, LANE),
        ),
        out_shape=out_shapes,
        compiler_params=_params("arbitrary"),
        name="in_proj",
    )(x2, *rope, sh_a, sc_a, pre_w, w1, wt, q_norm, wuq, kv_norm, wukt, wuv)


def _lane_tile_reduce(x, op):
    out = x[:, 0:LANE]
    for t in range(1, x.shape[1] // LANE):
        out = op(out, x[:, t * LANE:(t + 1) * LANE])
    return out


def _pv_normalised(chunks, v_ref):
    ones = jnp.ones((ATTN_CK, LANE), BF16)
    acc = None
    for c, p in enumerate(chunks):
        v_aug = jnp.concatenate([v_ref[c * ATTN_CK:(c + 1) * ATTN_CK, :], ones], axis=1)
        pv = jnp.dot(p.astype(BF16), v_aug, preferred_element_type=F32)
        acc = pv if acc is None else acc + pv
    return acc[:, 0:LANE] * (1.0 / acc[:, LANE:2 * LANE])


def _softmax_pv(s_ref, m_col, v_ref, seq):
    return _pv_normalised(
        [jnp.exp2(s_ref[:, c * ATTN_CK:(c + 1) * ATTN_CK] - m_col) for c in range(seq // ATTN_CK)], v_ref)


def _mla_kernel(q_ref, knt_ref, krt_ref, v_ref, o_ref, s_ref):
    seq = s_ref.shape[2]
    for hd in range(MLA_HEADS):
        sc_ref = s_ref.at[hd % 2]
        q = q_ref[0, hd]
        m_part = None
        for c in range(seq // ATTN_CK):
            sl = slice(c * ATTN_CK, (c + 1) * ATTN_CK)
            kt = jnp.concatenate([knt_ref[0, hd, :, sl], krt_ref[0, :, sl]], axis=0)
            s = jnp.dot(q, kt, preferred_element_type=F32)
            sc_ref[:, sl] = s
            mp = _lane_tile_reduce(s, jnp.maximum)
            m_part = mp if m_part is None else jnp.maximum(m_part, mp)
        m_col = jnp.max(m_part, axis=-1, keepdims=True)
        o = _softmax_pv(sc_ref, m_col, v_ref.at[0, hd], seq)
        o_ref[0, :, hd * MLA_V:(hd + 1) * MLA_V] = o.astype(o_ref.dtype)


def _mla_attn(q, knt, krt, v):
    bsz, nh, seq, _ = q.shape
    tq = ATTN_TQ
    return pl.pallas_call(
        _mla_kernel,
        grid=(bsz, seq // tq),
        in_specs=[
            pl.BlockSpec((1, nh, tq, 2 * LANE), lambda b, i: (b, 0, i, 0)),
            pl.BlockSpec((1, nh, LANE, seq), lambda b, i: (b, 0, 0, 0)),
            pl.BlockSpec((1, LANE, seq), lambda b, i: (b, 0, 0)),
            pl.BlockSpec((1, nh, seq, LANE), lambda b, i: (b, 0, 0, 0)),
        ],
        out_specs=pl.BlockSpec((1, tq, nh * MLA_V), lambda b, i: (b, i, 0)),
        out_shape=jax.ShapeDtypeStruct((bsz, seq, nh * MLA_V), BF16),
        scratch_shapes=[pltpu.VMEM((2, tq, seq), F32)],
        compiler_params=_params("arbitrary", "arbitrary"),
        name="mla_attn",
    )(q, knt, krt, v)


def _t5_bucket(rel):
    n = jnp.abs(rel)
    large = jnp.full(rel.shape, 8, jnp.int32)
    for t in T5_THRESHOLDS:
        large = large + jnp.where(n >= t, 1, 0)
    return jnp.where(rel > 0, REL_BUCKETS // 2, 0) + jnp.where(n < 8, n, large)


def _diff_split_q(q):
    lane = lax.broadcasted_iota(jnp.int32, q.shape, 1)
    zero = jnp.zeros_like(q)
    return jnp.where(lane < DIFF_QK, q, zero), jnp.where(lane >= DIFF_QK, q, zero)


def _softmax_pv_tiles(s_ref, m_col, v_ref):
    per = ATTN_CK // LANE
    chunks = []
    for c in range(s_ref.shape[0] // per):
        sv = jnp.concatenate([s_ref[c * per + t] for t in range(per)], axis=1)
        chunks.append(jnp.exp2(sv - m_col))
    return _pv_normalised(chunks, v_ref)


def _diff_finish(s1_ref, s2_ref, m1, m2, v_ref, lam, subln_row, lambda_init):
    m1c = jnp.max(m1, axis=-1, keepdims=True)
    m2c = jnp.max(m2, axis=-1, keepdims=True)
    o = _softmax_pv_tiles(s1_ref, m1c, v_ref) - _softmax_pv_tiles(s2_ref, m2c, v_ref) * lam
    return _rms(o, subln_row) * (1.0 - lambda_init)


def _bias_rows(t_neg, t_pos, ws, nw, seq):
    shape = (2 * SUBLANE, seq)
    tile = lax.broadcasted_iota(jnp.int32, shape, 1) // LANE
    cv = jnp.where(tile < ws, t_neg, jnp.where(tile >= ws + nw, t_pos, 0.0))
    hi = cv.astype(BF16).astype(F32)
    row = lax.broadcasted_iota(jnp.int32, shape, 0)
    return jnp.where(row == 0, hi, jnp.where(row == 1, cv - hi, 0.0)).astype(BF16)


def _diff_kernel(pos_sm, tab_sm, dq_ref, dkt_ref, dv_ref, pcol_ref, pch_ref, tabt_ref,
                 lq1_ref, lk1_ref, lq2_ref, lk2_ref, subln_ref, o_ref,
                 sa0_ref, sb0_ref, sa1_ref, sb1_ref, bkt_ref, oslow_ref, *, lambda_init):
    b = pl.program_id(0)
    qi = pl.program_id(1)
    nch, tq, _ = sa0_ref.shape
    nw = tq // LANE + 2
    per = ATTN_CK // LANE

    q_lo = pos_sm[b, qi * tq]
    q_hi = pos_sm[b, qi * tq + tq - 1]
    n_neg = jnp.int32(0)
    n_pos = jnp.int32(0)
    for c in range(nch):
        n_neg = n_neg + (pos_sm[b, c * LANE + LANE - 1] - q_lo <= -T5_SATURATE).astype(jnp.int32)
        n_pos = n_pos + (pos_sm[b, c * LANE] - q_hi >= T5_SATURATE).astype(jnp.int32)
    fits = (nch - n_neg - n_pos) <= nw
    ws = jnp.minimum(n_neg, nch - nw)

    pq = pcol_ref[0]
    lam = (jnp.exp(jnp.sum(lq1_ref[...] * lk1_ref[...], axis=-1, keepdims=True))
           - jnp.exp(jnp.sum(lq2_ref[...] * lk2_ref[...], axis=-1, keepdims=True)) + lambda_init)
    subln_row = subln_ref[...]
    neg_inf = jnp.full((tq, LANE), -jnp.inf, F32)
    slots = ((sa0_ref, sb0_ref), (sa1_ref, sb1_ref))
    pick2 = jnp.where(lax.broadcasted_iota(jnp.int32, (tq, LANE), 1) < 2, 1.0, 0.0).astype(BF16)
    kt_pad = jnp.zeros((LANE - 2 * SUBLANE, ATTN_CK), BF16)

    def gather_bias(tab_b, bkt):
        return jnp.take_along_axis(tab_b, bkt, axis=1, mode="promise_in_bounds")

    @pl.when(fits)
    def _():
        for j in range(nw):
            bkt_ref[j] = _t5_bucket(pch_ref[0, ws + j] - pq)
        ones = jnp.ones((ATTN_CK, LANE), BF16)

        def head_operands(hd):
            t_neg = tab_sm[REL_BUCKETS // 2 - 1, hd] * LOG2E
            t_pos = tab_sm[REL_BUCKETS - 1, hd] * LOG2E
            q1, q2 = _diff_split_q(dq_ref[0, hd])
            return (jnp.concatenate([q1, pick2], axis=1), jnp.concatenate([q2, pick2], axis=1),
                    _bias_rows(t_neg, t_pos, ws, nw, nch * LANE))

        def scores_chunk(hd, c4, q1, q2, rows):
            s1_ref, s2_ref = slots[hd % len(slots)]
            sl = slice(c4 * ATTN_CK, (c4 + 1) * ATTN_CK)
            kt = jnp.concatenate([dkt_ref[0, hd, :, sl], rows[:, sl], kt_pad], axis=0)
            sa = jnp.dot(q1, kt, preferred_element_type=F32)
            sb = jnp.dot(q2, kt, preferred_element_type=F32)
            for t in range(per):
                s1_ref[c4 * per + t] = sa[:, t * LANE:(t + 1) * LANE]
                s2_ref[c4 * per + t] = sb[:, t * LANE:(t + 1) * LANE]

        def window_bias_and_max(hd, part, n_parts, m12):
            s1_ref, s2_ref = slots[hd % len(slots)]
            m1, m2 = m12
            tab_b = jnp.broadcast_to(tabt_ref[hd:hd + 1, :] * LOG2E, (tq, LANE))
            for j in range(part, nw, n_parts):
                c = ws + j
                bias = gather_bias(tab_b, bkt_ref[j])
                va = s1_ref[c] + bias
                vb = s2_ref[c] + bias
                s1_ref[c] = va
                s2_ref[c] = vb
                m1 = jnp.maximum(m1, va)
                m2 = jnp.maximum(m2, vb)
            for k in range(part, nch - nw, n_parts):
                c = jnp.where(k < ws, k, k + nw)
                m1 = jnp.maximum(m1, s1_ref[c])
                m2 = jnp.maximum(m2, s2_ref[c])
            return m1, m2

        def pv_chunk(hd, c4, m_cols, accs):
            v_aug = jnp.concatenate([dv_ref[0, hd, c4 * ATTN_CK:(c4 + 1) * ATTN_CK, :], ones], axis=1)
            out = []
            for s_ref, m_col, acc in zip(slots[hd % len(slots)], m_cols, accs):
                sv = jnp.concatenate([s_ref[c4 * per + t] for t in range(per)], axis=1)
                pv = jnp.dot(jnp.exp2(sv - m_col).astype(BF16), v_aug, preferred_element_type=F32)
                out.append(pv if acc is None else acc + pv)
            return out

        n_parts = nch // per
        col_max = lambda m12: tuple(jnp.max(m, axis=-1, keepdims=True) for m in m12)
        ops = head_operands(0)
        for c4 in range(n_parts):
            scores_chunk(0, c4, *ops)
        m_cols = col_max(window_bias_and_max(0, 0, 1, (neg_inf, neg_inf)))
        for hd in range(DIFF_HEADS):
            nxt = hd + 1
            if nxt < DIFF_HEADS:
                ops = head_operands(nxt)
            accs = [None, None]
            for c4 in range(n_parts):
                if nxt < DIFF_HEADS:
                    scores_chunk(nxt, c4, *ops)
                accs = pv_chunk(hd, c4, m_cols, accs)
            a1, a2 = accs
            o = a1[:, 0:LANE] * (1.0 / a1[:, LANE:2 * LANE]) - (a2[:, 0:LANE] * (1.0 / a2[:, LANE:2 * LANE])) * lam
            o_ref[0, :, hd * DIFF_V:(hd + 1) * DIFF_V] = (
                _rms(o, subln_row) * (1.0 - lambda_init)).astype(o_ref.dtype)
            if nxt < DIFF_HEADS:
                m_cols = col_max(window_bias_and_max(nxt, 0, 1, (neg_inf, neg_inf)))

    @pl.when(jnp.logical_not(fits))
    def _():
        for c in range(nch):
            bkt_ref[c] = _t5_bucket(pch_ref[0, c] - pq)

        def body(hd, carry):
            q1, q2 = _diff_split_q(dq_ref[0, hd])
            tab_b = jnp.broadcast_to(tabt_ref[pl.ds(hd, 1), :] * LOG2E, (tq, LANE))
            m1 = neg_inf
            m2 = neg_inf
            for c4 in range(nch // per):
                kt = dkt_ref[0, hd, :, c4 * ATTN_CK:(c4 + 1) * ATTN_CK]
                sa = jnp.dot(q1, kt, preferred_element_type=F32)
                sb = jnp.dot(q2, kt, preferred_element_type=F32)
                for t in range(per):
                    c = c4 * per + t
                    bias = gather_bias(tab_b, bkt_ref[c])
                    va = sa[:, t * LANE:(t + 1) * LANE] + bias
                    vb = sb[:, t * LANE:(t + 1) * LANE] + bias
                    sa0_ref[c] = va
                    sb0_ref[c] = vb
                    m1 = jnp.maximum(m1, va)
                    m2 = jnp.maximum(m2, vb)
            oslow_ref[hd] = _diff_finish(sa0_ref, sb0_ref, m1, m2, dv_ref.at[0, hd], lam, subln_row,
                                         lambda_init)
            return carry

        lax.fori_loop(0, DIFF_HEADS, body, 0)
        for hd in range(DIFF_HEADS):
            o_ref[0, :, hd * DIFF_V:(hd + 1) * DIFF_V] = oslow_ref[hd].astype(o_ref.dtype)


def _diff_attn(positions, rel_bias, dq, dkt, dv, pos_col, pos_tiles, tab_t, lq1, lk1, lq2, lk2, subln,
               lambda_init):
    bsz, nh, seq, _ = dq.shape
    tq = DIFF_TQ
    nch = seq // LANE
    const2 = lambda b, i, *_: (0, 0)
    once = pl.Buffered(1)
    grid_spec = pltpu.PrefetchScalarGridSpec(
        num_scalar_prefetch=2,
        grid=(bsz, seq // tq),
        in_specs=[
            pl.BlockSpec((1, nh, tq, LANE), lambda b, i, *_: (b, 0, i, 0)),
            pl.BlockSpec((1, nh, LANE, seq), lambda b, i, *_: (b, 0, 0, 0), pipeline_mode=once),
            pl.BlockSpec((1, nh, seq, LANE), lambda b, i, *_: (b, 0, 0, 0), pipeline_mode=once),
            pl.BlockSpec((1, tq, 1), lambda b, i, *_: (b, i, 0)),
            pl.BlockSpec((1, nch, 1, LANE), lambda b, i, *_: (b, 0, 0, 0)),
            pl.BlockSpec(tab_t.shape, const2),
            pl.BlockSpec((1, DIFF_QK), const2),
            pl.BlockSpec((1, DIFF_QK), const2),
            pl.BlockSpec((1, DIFF_QK), const2),
            pl.BlockSpec((1, DIFF_QK), const2),
            pl.BlockSpec((1, DIFF_V), const2),
        ],
        out_specs=pl.BlockSpec((1, tq, nh * DIFF_V), lambda b, i, *_: (b, i, 0)),
        scratch_shapes=[
            *([pltpu.VMEM((nch, tq, LANE), F32)] * 4),
            pltpu.VMEM((nch, tq, LANE), jnp.int32),
            pltpu.VMEM((nh, tq, LANE), F32),
        ],
    )
    return pl.pallas_call(
        partial(_diff_kernel, lambda_init=lambda_init),
        grid_spec=grid_spec,
        out_shape=jax.ShapeDtypeStruct((bsz, seq, nh * DIFF_V), BF16),
        compiler_params=_params("arbitrary", "arbitrary"),
        name="diff_attn",
    )(positions, rel_bias, dq, dkt, dv, pos_col, pos_tiles, tab_t, lq1, lk1, lq2, lk2, subln)


def _out_proj_kernel(om_ref, od_ref, wa_ref, wb_ref, x_ref, g_ref, post_ref, sh_ref, sc_ref, pre_ref,
                     x1_ref, h2_ref, il_ref):
    _, nph, rows, d = x_ref.shape
    tm = nph * rows
    merge = lambda ref: ref[0].reshape(tm, ref.shape[3])
    y = (jnp.dot(merge(om_ref), wa_ref[...], preferred_element_type=F32)
         + jnp.dot(merge(od_ref), wb_ref[...], preferred_element_type=F32))
    x1 = merge(x_ref) + g_ref[0] * _rms(y, post_ref[...])
    x1_ref[0] = x1.reshape(nph, rows, d)
    a_row = pre_ref[...] * (1.0 + sc_ref[0])
    ms = jnp.mean(x1 * x1, axis=-1, keepdims=True)
    h2 = (x1 * lax.rsqrt(ms + NORM_EPS)) * a_row + sh_ref[0]
    for j in range(il_ref.shape[0]):
        for s in range(nph):
            il_ref[j, pl.ds(s, rows, stride=nph), :] = h2[s * rows:(s + 1) * rows, j * LANE:(j + 1) * LANE]
    for j in range(il_ref.shape[0]):
        h2_ref[:, j * LANE:(j + 1) * LANE] = il_ref[j].astype(h2_ref.dtype)


def _out_proj(o_mla, o_diff, wo_a, wo_b, x2, g_a, post_w, sh_f, sc_f, pre_w, seq):
    m, d = x2.shape
    bsz = m // seq
    tm = OUTPROJ_TM
    rows = tm // ROW_IL
    tpb = seq // tm
    phases = lambda a: a.reshape(bsz, ROW_IL, seq // ROW_IL, a.shape[-1])
    ph_block = lambda w: pl.BlockSpec((1, ROW_IL, rows, w), lambda i: (i // tpb, 0, i % tpb, 0))
    bmap3 = lambda i: (i // tpb, 0, 0)
    x1, h2 = pl.pallas_call(
        _out_proj_kernel,
        grid=(m // tm,),
        in_specs=[
            ph_block(o_mla.shape[1]),
            ph_block(o_diff.shape[1]),
            _resident(wo_a.shape),
            _resident(wo_b.shape),
            ph_block(d),
            pl.BlockSpec((1, 1, d), bmap3),
            _resident((1, d)),
            pl.BlockSpec((1, 1, d), bmap3),
            pl.BlockSpec((1, 1, d), bmap3),
            _resident((1, d)),
        ],
        out_specs=(ph_block(d), pl.BlockSpec((tm, d), lambda i: (i, 0))),
        out_shape=(jax.ShapeDtypeStruct((bsz, ROW_IL, seq // ROW_IL, d), F32),
                   jax.ShapeDtypeStruct((m, d), BF16)),
        scratch_shapes=[pltpu.VMEM((d // LANE, tm, LANE), F32)],
        compiler_params=_params("arbitrary"),
        name="out_proj",
    )(phases(o_mla), phases(o_diff), wo_a, wo_b, phases(x2), g_a, post_w, sh_f, sc_f, pre_w)
    return x1.reshape(m, d), h2


def _gelu_tanh(g):
    a = -2.0 * math.sqrt(2.0 / math.pi) * LOG2E
    return g * (1.0 / (1.0 + jnp.exp2(g * (a + (a * 0.044715) * (g * g)))))


def _neighbour_rows(u, step):
    n = u.shape[0]
    row = lax.broadcasted_iota(jnp.int32, (ROW_IL, u.shape[1]), 0)
    if step == -1:
        wrap = jnp.where(row == 0, 0.0, pltpu.roll(u[n - ROW_IL:], 1, 0))
        return jnp.concatenate([wrap, u[:n - ROW_IL]], axis=0)
    wrap = jnp.where(row == ROW_IL - 1, 0.0, pltpu.roll(u[0:ROW_IL], ROW_IL - 1, 0))
    return jnp.concatenate([u[ROW_IL:], wrap], axis=0)


def _ffn_up_kernel(h_ref, wg_ref, wv_ref, cwg_ref, cwv_ref, cbg_ref, cbv_ref, o_ref):
    h = h_ref[...]

    def conv(u, cw_ref, cb_ref):
        return (cb_ref[...] + u * cw_ref[1:2, :]
                + _neighbour_rows(u, -1) * cw_ref[0:1, :] + _neighbour_rows(u, 1) * cw_ref[2:3, :])

    gate = conv(jnp.dot(h, wg_ref[...].astype(BF16), preferred_element_type=F32), cwg_ref, cbg_ref)
    val = conv(jnp.dot(h, wv_ref[...].astype(BF16), preferred_element_type=F32), cwv_ref, cbv_ref)
    o_ref[...] = (_gelu_tanh(gate) * val).astype(o_ref.dtype)


def _ffn_up(h2, w_up, conv_w, conv_b, seq):
    m, d = h2.shape
    f = w_up.shape[1] // 2
    tn = UP_TN
    nj = f // tn
    return pl.pallas_call(
        _ffn_up_kernel,
        grid=(nj, m // seq),
        in_specs=[
            pl.BlockSpec((seq, d), lambda j, i: (i, 0)),
            pl.BlockSpec((d, tn), lambda j, i: (0, j)),
            pl.BlockSpec((d, tn), lambda j, i: (0, j + nj)),
            pl.BlockSpec((3, tn), lambda j, i: (0, j)),
            pl.BlockSpec((3, tn), lambda j, i: (0, j + nj)),
            pl.BlockSpec((1, tn), lambda j, i: (0, j)),
            pl.BlockSpec((1, tn), lambda j, i: (0, j + nj)),
        ],
        out_specs=pl.BlockSpec((seq, tn), lambda j, i: (i, j)),
        out_shape=jax.ShapeDtypeStruct((m, f), BF16),
        compiler_params=_params("arbitrary", "arbitrary"),
        name="ffn_up",
    )(h2, w_up, w_up, conv_w, conv_w, conv_b, conv_b)


def _ffn_down_kernel(a_ref, w_ref, x1_ref, g_ref, post_ref, o_ref, y_ref):
    tm = a_ref.shape[0]
    y = jnp.dot(a_ref[...], w_ref[...], preferred_element_type=F32)
    yn = _rms(y, post_ref[...])
    g = g_ref[0]
    for j in range(y_ref.shape[0]):
        y_ref[j] = yn[:, j * LANE:(j + 1) * LANE]
    for s in range(ROW_IL):
        for j in range(y_ref.shape[0]):
            cs = slice(j * LANE, (j + 1) * LANE)
            rows = y_ref[j, pl.ds(s, tm // ROW_IL, stride=ROW_IL), :]
            o_ref[0, s, :, cs] = x1_ref[0, s, :, cs] + g[:, cs] * rows


def _ffn_down(act, w_down, x1, g_f, post_w, seq):
    m, f = act.shape
    d = w_down.shape[1]
    tm = DOWN_TM
    tpb = seq // tm
    bsz = m // seq
    nat_block = pl.BlockSpec((1, ROW_IL, tm // ROW_IL, d), lambda i: (i // tpb, 0, i % tpb, 0))
    out = pl.pallas_call(
        _ffn_down_kernel,
        grid=(m // tm,),
        in_specs=[
            pl.BlockSpec((tm, f), lambda i: (i, 0)),
            _resident(w_down.shape),
            nat_block,
            pl.BlockSpec((1, 1, d), lambda i: (i // tpb, 0, 0)),
            _resident((1, d)),
        ],
        out_specs=nat_block,
        out_shape=jax.ShapeDtypeStruct((bsz, ROW_IL, seq // ROW_IL, d), F32),
        scratch_shapes=[pltpu.VMEM((d // LANE, tm, LANE), F32)],
        compiler_params=_params("arbitrary"),
        name="ffn_down",
    )(act, w_down, x1.reshape(bsz, ROW_IL, seq // ROW_IL, d), g_f, post_w)
    return out.reshape(m, d)


def _rope_pair_tiles(w_rope):
    half = MLA_ROPE // 2
    t1, t2 = w_rope[:, :half], w_rope[:, half:]
    z = jnp.zeros((w_rope.shape[0], LANE - MLA_ROPE), w_rope.dtype)
    return jnp.concatenate([t1, t2, z], axis=1), jnp.concatenate([t2, t1, z], axis=1)


def _layout_w_in(w_in):
    o_kv = MLA_Q_RANK
    o_kr = o_kv + MLA_KV_RANK
    o_dq = o_kr + MLA_ROPE
    o_dk = o_dq + DIFF_HEADS * 2 * DIFF_QK
    o_dv = o_dk + DIFF_HEADS * 2 * DIFF_QK
    w1 = jnp.concatenate([w_in[:, :o_kr], w_in[:, o_dq:o_dk], w_in[:, o_dv:]], axis=1).astype(BF16)
    ra, rb = _rope_pair_tiles(w_in[:, o_kr:o_dq])
    wt = jnp.concatenate([ra, rb, w_in[:, o_dk:o_dv]], axis=1).T.astype(BF16)
    return w1, wt


def _layout_w_uq(w_uq):
    k = w_uq.shape[0]
    w3 = w_uq.reshape(k, MLA_HEADS, MLA_NOPE + MLA_ROPE)
    blocks = []
    for hd in range(MLA_HEADS):
        ra, rb = _rope_pair_tiles(w3[:, hd, MLA_NOPE:])
        blocks += [w3[:, hd, :MLA_NOPE], ra, rb]
    return jnp.concatenate(blocks, axis=1).astype(BF16)


def _layout_w_ukv(w_ukv):
    k = w_ukv.shape[0]
    w3 = w_ukv.reshape(k, MLA_HEADS, MLA_NOPE + MLA_V)
    wukt = w3[:, :, :MLA_NOPE].reshape(k, MLA_HEADS * MLA_NOPE).T.astype(BF16)
    wuv = w3[:, :, MLA_NOPE:].reshape(k, MLA_HEADS * MLA_V).astype(BF16)
    return wukt, wuv


def _rope_kernel(pos_ref, inv_ref, cr_ref, sr_ref, ct_ref, st_ref):
    ang = inv_ref[...] * pos_ref[0].astype(F32)
    c = jnp.cos(ang)
    s = jnp.sin(ang)
    zeros = jnp.zeros((LANE - MLA_ROPE, ang.shape[1]), F32)
    ct = jnp.concatenate([c, c, zeros], axis=0)
    st = jnp.concatenate([-s, s, zeros], axis=0)
    ct_ref[0] = ct
    st_ref[0] = st
    cr_ref[0] = ct.T
    sr_ref[0] = st.T


def _rope_tables(pos_row):
    bsz, _, seq = pos_row.shape
    inv = 1.0 / (ROPE_THETA ** (np.arange(0, MLA_ROPE, 2, dtype=np.float32) / MLA_ROPE))
    inv_col = jnp.asarray(inv.astype(np.float32).reshape(MLA_ROPE // 2, 1))
    row_tab = jax.ShapeDtypeStruct((bsz, seq, LANE), F32)
    col_tab = jax.ShapeDtypeStruct((bsz, LANE, seq), F32)
    return pl.pallas_call(
        _rope_kernel,
        grid=(bsz,),
        in_specs=[pl.BlockSpec((1, 1, seq), lambda b: (b, 0, 0)),
                  pl.BlockSpec(inv_col.shape, lambda b: (0, 0))],
        out_specs=(pl.BlockSpec((1, seq, LANE), lambda b: (b, 0, 0)),
                   pl.BlockSpec((1, seq, LANE), lambda b: (b, 0, 0)),
                   pl.BlockSpec((1, LANE, seq), lambda b: (b, 0, 0)),
                   pl.BlockSpec((1, LANE, seq), lambda b: (b, 0, 0))),
        out_shape=(row_tab, row_tab, col_tab, col_tab),
        compiler_params=_params("arbitrary"),
        name="rope_tables",
    )(pos_row, inv_col)


def kernel(x, c, positions, rel_bias, ada_w, ada_b, attn_pre_norm, attn_post_norm, w_in, q_norm, w_uq,
           kv_norm, w_ukv, lambda_q1, lambda_k1, lambda_q2, lambda_k2, diff_subln, w_o, ffn_pre_norm,
           ffn_post_norm, w_up, conv_w, conv_b, w_down):
    bsz, seq, d = x.shape
    depth = ada_w.shape[0]
    m = bsz * seq
    positions = positions.astype(jnp.int32)
    pos_col = positions.reshape(bsz, seq, 1)
    pos_row = positions.reshape(bsz, 1, seq)
    pos_tiles = positions.reshape(bsz, seq // LANE, 1, LANE)
    rope = _rope_tables(pos_row)
    tab_t = jnp.zeros((DIFF_HEADS, LANE), F32).at[:, :REL_BUCKETS].set(rel_bias.astype(F32).T)
    row = lambda v: v.reshape(1, -1).astype(F32)

    x2 = x.reshape(m, d)
    for l in range(depth):
        lambda_init = 0.8 - 0.6 * math.exp(-0.3 * l)
        mod = _ada_mod(c, ada_w[l], ada_b[l])
        sh_a, sc_a, g_a, sh_f, sc_f, g_f = [t.reshape(bsz, 1, d) for t in jnp.split(mod, N_MOD, axis=-1)]

        w1, wt = _layout_w_in(w_in[l])
        wuq = _layout_w_uq(w_uq[l])
        wukt, wuv = _layout_w_ukv(w_ukv[l])
        q, knt, krt, v, dq, dkt, dv = _in_proj(
            x2, rope, sh_a, sc_a, row(attn_pre_norm[l]), w1, wt, row(q_norm[l]), wuq,
            row(kv_norm[l]), wukt, wuv, bsz, seq)

        o_mla = _mla_attn(q, knt, krt, v)
        o_diff = _diff_attn(positions, rel_bias.astype(F32), dq, dkt, dv, pos_col, pos_tiles, tab_t,
                            row(lambda_q1[l]), row(lambda_k1[l]), row(lambda_q2[l]), row(lambda_k2[l]),
                            row(diff_subln[l]), lambda_init)

        wo = w_o[l].astype(BF16)
        n_mla = MLA_HEADS * MLA_V
        x1, h2 = _out_proj(o_mla.reshape(m, n_mla), o_diff.reshape(m, DIFF_HEADS * DIFF_V),
                           wo[:n_mla], wo[n_mla:], x2, g_a, row(attn_post_norm[l]), sh_f, sc_f,
                           row(ffn_pre_norm[l]), seq)

        act = _ffn_up(h2, w_up[l], conv_w[l].astype(F32), row(conv_b[l]), seq)
        x2 = _ffn_down(act, w_down[l].astype(BF16), x1, g_f, row(ffn_post_norm[l]), seq)
    return x2.reshape(bsz, seq, d)
```

```python
import math
from functools import partial

import jax
import jax.numpy as jnp
import numpy as np
from jax import lax
from jax.experimental import pallas as pl
from jax.experimental.pallas import tpu as pltpu

D_MODEL = 2048
MLA_HEADS = 8
MLA_Q_RANK = 512
MLA_KV_RANK = 256
MLA_NOPE = 128
MLA_ROPE = 64
MLA_V = 128
MLA_SCALE = (MLA_NOPE + MLA_ROPE) ** -0.5
DIFF_HEADS = 8
DIFF_QK = 64
DIFF_V = 128
DIFF_SCALE = DIFF_QK ** -0.5
D_FF = 5632
REL_BUCKETS = 32
ROPE_THETA = 10000.0
NORM_EPS = 1e-6
N_MOD = 6
LOG2E = math.log2(math.e)
T5_SATURATE = 128
T5_THRESHOLDS = (12, 16, 23, 32, 46, 64, 91)

LANE = 128
SUBLANE = 8
VMEM_LIMIT_BYTES = 56 * 1024 * 1024

ADA_TN = 1536
INPROJ_TM = 512
ATTN_TQ = 512
DIFF_TQ = 256
ATTN_CK = 512
OUTPROJ_TM = 512
ROW_IL = 8
UP_TN = 512
DOWN_TM = 256

F32 = jnp.float32
BF16 = jnp.bfloat16


def _params(*semantics):
    return pltpu.CompilerParams(dimension_semantics=semantics, vmem_limit_bytes=VMEM_LIMIT_BYTES)


def _resident(shape):
    nd = len(shape)
    return pl.BlockSpec(shape, lambda *_: (0,) * nd, pipeline_mode=pl.Buffered(1))


def _rms(x, w_row):
    ms = jnp.mean(x * x, axis=-1, keepdims=True)
    return (x * lax.rsqrt(ms + NORM_EPS)) * w_row


def _nt_dot(a, b):
    return lax.dot_general(a, b, (((1,), (1,)), ((), ())), preferred_element_type=F32)


def _ada_kernel(c_ref, w_ref, b_ref, o_ref):
    c = c_ref[...]
    c_act = (c * jax.nn.sigmoid(c)).astype(BF16)
    o_ref[...] = jnp.dot(c_act, w_ref[...].astype(BF16), preferred_element_type=F32) + b_ref[...]


def _ada_mod(c, ada_w, ada_b):
    bsz, d = c.shape
    n = ada_w.shape[1]
    return pl.pallas_call(
        _ada_kernel,
        grid=(n // ADA_TN,),
        in_specs=[
            pl.BlockSpec((bsz, d), lambda j: (0, 0)),
            pl.BlockSpec((d, ADA_TN), lambda j: (0, j)),
            pl.BlockSpec((1, ADA_TN), lambda j: (0, j)),
        ],
        out_specs=pl.BlockSpec((bsz, ADA_TN), lambda j: (0, j)),
        out_shape=jax.ShapeDtypeStruct((bsz, n), F32),
        compiler_params=_params("arbitrary"),
        name="ada_mod",
    )(c, ada_w, ada_b.reshape(1, n))


N_W1 = MLA_Q_RANK + MLA_KV_RANK + 2 * DIFF_HEADS * 2 * DIFF_QK
N_WT = 2 * LANE + DIFF_HEADS * 2 * DIFF_QK
Q_HEAD_COLS = 3 * LANE


def _in_proj_kernel(x_ref, cr_ref, sr_ref, ct_ref, st_ref, sh_ref, sc_ref, pre_ref, w1_ref, wt_ref,
                    qn_ref, wuq_ref, kvn_ref, wukt_ref, wuv_ref,
                    q_out, knt_out, krt_out, v_out, dq_out, dkt_out, dv_out):
    x = x_ref[...]
    a_row = pre_ref[...] * (1.0 + sc_ref[0])
    ms = jnp.mean(x * x, axis=-1, keepdims=True)
    h = ((x * lax.rsqrt(ms + NORM_EPS)) * a_row + sh_ref[0]).astype(BF16)

    p1 = jnp.dot(h, w1_ref[...], preferred_element_type=F32)
    p2t = _nt_dot(wt_ref[...], h)

    cos_r, sin_r = cr_ref[0], sr_ref[0]
    cos_t, sin_t = ct_ref[0], st_ref[0]

    q_lat = p1[:, 0:MLA_Q_RANK]
    rq = _rms(q_lat, qn_ref[...] * (MLA_SCALE * LOG2E)).astype(BF16)
    qf = jnp.dot(rq, wuq_ref[...], preferred_element_type=F32)
    for hd in range(MLA_HEADS):
        c0 = hd * Q_HEAD_COLS
        q_out[0, hd, :, 0:LANE] = qf[:, c0:c0 + LANE].astype(BF16)
        rope = qf[:, c0 + LANE:c0 + 2 * LANE] * cos_r + qf[:, c0 + 2 * LANE:c0 + 3 * LANE] * sin_r
        q_out[0, hd, :, LANE:2 * LANE] = rope.astype(BF16)

    kv_lat = p1[:, MLA_Q_RANK:MLA_Q_RANK + MLA_KV_RANK]
    rkv = _rms(kv_lat, kvn_ref[...]).astype(BF16)
    knt = _nt_dot(wukt_ref[...], rkv)
    vv = jnp.dot(rkv, wuv_ref[...], preferred_element_type=F32)
    for hd in range(MLA_HEADS):
        knt_out[0, hd] = knt[hd * LANE:(hd + 1) * LANE, :].astype(BF16)
        v_out[0, hd] = vv[:, hd * LANE:(hd + 1) * LANE].astype(BF16)
    krt_out[0] = (p2t[0:LANE, :] * cos_t + p2t[LANE:2 * LANE, :] * sin_t).astype(BF16)

    o_dq = MLA_Q_RANK + MLA_KV_RANK
    o_dv = o_dq + DIFF_HEADS * 2 * DIFF_QK
    for hd in range(DIFF_HEADS):
        dq_out[0, hd] = (p1[:, o_dq + hd * LANE:o_dq + (hd + 1) * LANE] * (DIFF_SCALE * LOG2E)).astype(BF16)
        dv_out[0, hd] = p1[:, o_dv + hd * LANE:o_dv + (hd + 1) * LANE].astype(BF16)
        dkt = p2t[2 * LANE + hd * LANE:2 * LANE + (hd + 1) * LANE, :].astype(BF16)
        for t in range(dkt_out.shape[2]):
            dkt_out[0, hd, t] = dkt[:, t * LANE:(t + 1) * LANE]


def _in_proj(x2, rope, sh_a, sc_a, pre_w, w1, wt, q_norm, wuq, kv_norm, wukt, wuv, bsz, seq):
    tm = INPROJ_TM
    tpb = seq // tm
    d = x2.shape[1]
    bmap3 = lambda i: (i // tpb, 0, 0)
    out_shapes = (
        jax.ShapeDtypeStruct((bsz, MLA_HEADS, seq, 2 * LANE), BF16),
        jax.ShapeDtypeStruct((bsz, MLA_HEADS, LANE, seq), BF16),
        jax.ShapeDtypeStruct((bsz, LANE, seq), BF16),
        jax.ShapeDtypeStruct((bsz, MLA_HEADS, seq, LANE), BF16),
        jax.ShapeDtypeStruct((bsz, DIFF_HEADS, seq, LANE), BF16),
        jax.ShapeDtypeStruct((bsz, DIFF_HEADS, seq // LANE, LANE, LANE), BF16),
        jax.ShapeDtypeStruct((bsz, DIFF_HEADS, seq, LANE), BF16),
    )
    row_blk = lambda nh, w: pl.BlockSpec((1, nh, tm, w), lambda i: (i // tpb, 0, i % tpb, 0))
    col_blk = lambda nh: pl.BlockSpec((1, nh, LANE, tm), lambda i: (i // tpb, 0, 0, i % tpb))
    return pl.pallas_call(
        _in_proj_kernel,
        grid=(bsz * tpb,),
        in_specs=[
            pl.BlockSpec((tm, d), lambda i: (i, 0)),
            pl.BlockSpec((1, tm, LANE), lambda i: (i // tpb, i % tpb, 0)),
            pl.BlockSpec((1, tm, LANE), lambda i: (i // tpb, i % tpb, 0)),
            pl.BlockSpec((1, LANE, tm), lambda i: (i // tpb, 0, i % tpb)),
            pl.BlockSpec((1, LANE, tm), lambda i: (i // tpb, 0, i % tpb)),
            pl.BlockSpec((1, 1, d), bmap3),
            pl.BlockSpec((1, 1, d), bmap3),
            _resident((1, d)),
            _resident(w1.shape),
            _resident(wt.shape),
            _resident(q_norm.shape),
            _resident(wuq.shape),
            _resident(kv_norm.shape),
            _resident(wukt.shape),
            _resident(wuv.shape),
        ],
        out_specs=(
            row_blk(MLA_HEADS, 2 * LANE),
            col_blk(MLA_HEADS),
            pl.BlockSpec((1, LANE, tm), lambda i: (i // tpb, 0, i % tpb)),
            row_blk(MLA_HEADS, LANE),
            row_blk(DIFF_HEADS, LANE),
            pl.BlockSpec((1, DIFF_HEADS, tm // LANE, LANE, LANE), lambda i: (i // tpb, 0, i % tpb, 0, 0)),
            row_blk(DIFF_HEADS, LANE),
        ),
        out_shape=out_shapes,
        compiler_params=_params("arbitrary"),
        name="in_proj",
    )(x2, *rope, sh_a, sc_a, pre_w, w1, wt, q_norm, wuq, kv_norm, wukt, wuv)


def _lane_tile_reduce(x, op):
    out = x[:, 0:LANE]
    for t in range(1, x.shape[1] // LANE):
        out = op(out, x[:, t * LANE:(t + 1) * LANE])
    return out


def _pv_normalised(chunks, v_ref):
    ones = jnp.ones((ATTN_CK, LANE), BF16)
    acc = None
    for c, p in enumerate(chunks):
        v_aug = jnp.concatenate([v_ref[c * ATTN_CK:(c + 1) * ATTN_CK, :], ones], axis=1)
        pv = jnp.dot(p.astype(BF16), v_aug, preferred_element_type=F32)
        acc = pv if acc is None else acc + pv
    return acc[:, 0:LANE] * (1.0 / acc[:, LANE:2 * LANE])


def _softmax_pv(s_ref, m_col, v_ref, seq):
    return _pv_normalised(
        [jnp.exp2(s_ref[:, c * ATTN_CK:(c + 1) * ATTN_CK] - m_col) for c in range(seq // ATTN_CK)], v_ref)


def _mla_kernel(q_ref, knt_ref, krt_ref, v_ref, o_ref, s_ref):
    seq = s_ref.shape[2]
    for hd in range(MLA_HEADS):
        sc_ref = s_ref.at[hd % 2]
        q = q_ref[0, hd]
        m_part = None
        for c in range(seq // ATTN_CK):
            sl = slice(c * ATTN_CK, (c + 1) * ATTN_CK)
            kt = jnp.concatenate([knt_ref[0, hd, :, sl], krt_ref[0, :, sl]], axis=0)
            s = jnp.dot(q, kt, preferred_element_type=F32)
            sc_ref[:, sl] = s
            mp = _lane_tile_reduce(s, jnp.maximum)
            m_part = mp if m_part is None else jnp.maximum(m_part, mp)
        m_col = jnp.max(m_part, axis=-1, keepdims=True)
        o = _softmax_pv(sc_ref, m_col, v_ref.at[0, hd], seq)
        o_ref[0, :, hd * MLA_V:(hd + 1) * MLA_V] = o.astype(o_ref.dtype)


def _mla_attn(q, knt, krt, v):
    bsz, nh, seq, _ = q.shape
    tq = ATTN_TQ
    return pl.pallas_call(
        _mla_kernel,
        grid=(bsz, seq // tq),
        in_specs=[
            pl.BlockSpec((1, nh, tq, 2 * LANE), lambda b, i: (b, 0, i, 0)),
            pl.BlockSpec((1, nh, LANE, seq), lambda b, i: (b, 0, 0, 0)),
            pl.BlockSpec((1, LANE, seq), lambda b, i: (b, 0, 0)),
            pl.BlockSpec((1, nh, seq, LANE), lambda b, i: (b, 0, 0, 0)),
        ],
        out_specs=pl.BlockSpec((1, tq, nh * MLA_V), lambda b, i: (b, i, 0)),
        out_shape=jax.ShapeDtypeStruct((bsz, seq, nh * MLA_V), BF16),
        scratch_shapes=[pltpu.VMEM((2, tq, seq), F32)],
        compiler_params=_params("arbitrary", "arbitrary"),
        name="mla_attn",
    )(q, knt, krt, v)


def _t5_bucket(rel):
    n = jnp.abs(rel)
    large = jnp.full(rel.shape, 8, jnp.int32)
    for t in T5_THRESHOLDS:
        large = large + jnp.where(n >= t, 1, 0)
    return jnp.where(rel > 0, REL_BUCKETS // 2, 0) + jnp.where(n < 8, n, large)


def _diff_split_q(q):
    lane = lax.broadcasted_iota(jnp.int32, q.shape, 1)
    zero = jnp.zeros_like(q)
    return jnp.where(lane < DIFF_QK, q, zero), jnp.where(lane >= DIFF_QK, q, zero)


class _TiledRows:
    def __init__(self, ref):
        self.ref = ref

    def __getitem__(self, idx):
        rows, _ = idx
        tiles = range(rows.start // LANE, rows.stop // LANE)
        return jnp.concatenate([self.ref[t] for t in tiles], axis=0)


def _softmax_pv_tiles(s_ref, m_col, v_ref):
    per = ATTN_CK // LANE
    chunks = []
    for c in range(s_ref.shape[0] // per):
        sv = jnp.concatenate([s_ref[c * per + t] for t in range(per)], axis=1)
        chunks.append(jnp.exp2(sv - m_col))
    return _pv_normalised(chunks, _TiledRows(v_ref))


def _diff_finish(s1_ref, s2_ref, m1, m2, v_ref, lam, subln_row, lambda_init):
    m1c = jnp.max(m1, axis=-1, keepdims=True)
    m2c = jnp.max(m2, axis=-1, keepdims=True)
    o = _softmax_pv_tiles(s1_ref, m1c, v_ref) - _softmax_pv_tiles(s2_ref, m2c, v_ref) * lam
    return _rms(o, subln_row) * (1.0 - lambda_init)


def _bias_rows(t_neg, t_pos, ws, nw, seq):
    shape = (2 * SUBLANE, seq)
    tile = lax.broadcasted_iota(jnp.int32, shape, 1) // LANE
    cv = jnp.where(tile < nw, 0.0, jnp.where(tile + ws >= seq // LANE, t_neg, t_pos))
    hi = cv.astype(BF16).astype(F32)
    row = lax.broadcasted_iota(jnp.int32, shape, 0)
    return jnp.where(row == 0, hi, jnp.where(row == 1, cv - hi, 0.0)).astype(BF16)


def _diff_kernel(pos_sm, tab_sm, dq_ref, dkt_ref, dv_ref, pcol_ref, pch_ref, tabt_ref,
                 lq1_ref, lk1_ref, lq2_ref, lk2_ref, subln_ref, o_ref,
                 sa0_ref, sb0_ref, sa1_ref, sb1_ref, bkt_ref, oslow_ref, *, lambda_init):
    b = pl.program_id(0)
    qi = pl.program_id(1)
    nch, tq, _ = sa0_ref.shape
    nw = tq // LANE + 2
    per = ATTN_CK // LANE

    q_lo = pos_sm[b, qi * tq]
    q_hi = pos_sm[b, qi * tq + tq - 1]
    n_neg = jnp.int32(0)
    n_pos = jnp.int32(0)
    for c in range(nch):
        n_neg = n_neg + (pos_sm[b, c * LANE + LANE - 1] - q_lo <= -T5_SATURATE).astype(jnp.int32)
        n_pos = n_pos + (pos_sm[b, c * LANE] - q_hi >= T5_SATURATE).astype(jnp.int32)
    fits = (nch - n_neg - n_pos) <= nw
    ws = jnp.minimum(n_neg, nch - nw)

    pq = pcol_ref[0]
    lam = (jnp.exp(jnp.sum(lq1_ref[...] * lk1_ref[...], axis=-1, keepdims=True))
           - jnp.exp(jnp.sum(lq2_ref[...] * lk2_ref[...], axis=-1, keepdims=True)) + lambda_init)
    subln_row = subln_ref[...]
    neg_inf = jnp.full((tq, LANE), -jnp.inf, F32)
    slots = ((sa0_ref, sb0_ref), (sa1_ref, sb1_ref))
    pick2 = jnp.where(lax.broadcasted_iota(jnp.int32, (tq, LANE), 1) < 2, 1.0, 0.0).astype(BF16)
    kt_pad = jnp.zeros((LANE - 2 * SUBLANE, ATTN_CK), BF16)

    def gather_bias(tab_b, bkt):
        return jnp.take_along_axis(tab_b, bkt, axis=1, mode="promise_in_bounds")

    @pl.when(fits)
    def _():
        for j in range(nw):
            bkt_ref[j] = _t5_bucket(pch_ref[0, ws + j] - pq)
        ones = jnp.ones((ATTN_CK, LANE), BF16)

        def head_operands(hd):
            t_neg = tab_sm[REL_BUCKETS // 2 - 1, hd] * LOG2E
            t_pos = tab_sm[REL_BUCKETS - 1, hd] * LOG2E
            q1, q2 = _diff_split_q(dq_ref[0, hd])
            return (jnp.concatenate([q1, pick2], axis=1), jnp.concatenate([q2, pick2], axis=1),
                    _bias_rows(t_neg, t_pos, ws, nw, nch * LANE))

        tile_at = [jnp.where(ws + j >= nch, ws + j - nch, ws + j) for j in range(nch)]

        def scores_chunk(hd, c4, q1, q2, rows, m12):
            s1_ref, s2_ref = slots[hd % len(slots)]
            m1, m2 = m12
            sl = slice(c4 * ATTN_CK, (c4 + 1) * ATTN_CK)
            k_tiles = jnp.concatenate([dkt_ref[0, hd, tile_at[c4 * per + t]] for t in range(per)], axis=1)
            kt = jnp.concatenate([k_tiles, rows[:, sl], kt_pad], axis=0)
            sa = jnp.dot(q1, kt, preferred_element_type=F32)
            sb = jnp.dot(q2, kt, preferred_element_type=F32)
            for t in range(per):
                j = c4 * per + t
                va = sa[:, t * LANE:(t + 1) * LANE]
                vb = sb[:, t * LANE:(t + 1) * LANE]
                s1_ref[j] = va
                s2_ref[j] = vb
                if j >= nw:
                    m1 = jnp.maximum(m1, va)
                    m2 = jnp.maximum(m2, vb)
            return m1, m2

        def window_bias_and_max(hd, m12):
            s1_ref, s2_ref = slots[hd % len(slots)]
            m1, m2 = m12
            tab_b = jnp.broadcast_to(tabt_ref[hd:hd + 1, :] * LOG2E, (tq, LANE))
            for j in range(nw):
                bias = gather_bias(tab_b, bkt_ref[j])
                va = s1_ref[j] + bias
                vb = s2_ref[j] + bias
                s1_ref[j] = va
                s2_ref[j] = vb
                m1 = jnp.maximum(m1, va)
                m2 = jnp.maximum(m2, vb)
            return m1, m2

        def pv_chunk(hd, c4, m_cols, accs):
            v_rows = jnp.concatenate([dv_ref[0, hd, tile_at[c4 * per + t]] for t in range(per)], axis=0)
            v_aug = jnp.concatenate([v_rows, ones], axis=1)
            out = []
            for s_ref, m_col, acc in zip(slots[hd % len(slots)], m_cols, accs):
                sv = jnp.concatenate([s_ref[c4 * per + t] for t in range(per)], axis=1)
                pv = jnp.dot(jnp.exp2(sv - m_col).astype(BF16), v_aug, preferred_element_type=F32)
                out.append(pv if acc is None else acc + pv)
            return out

        n_parts = nch // per
        col_max = lambda m12: tuple(jnp.max(m, axis=-1, keepdims=True) for m in m12)
        ops = head_operands(0)
        m12 = (neg_inf, neg_inf)
        for c4 in range(n_parts):
            m12 = scores_chunk(0, c4, *ops, m12)
        m_cols = col_max(window_bias_and_max(0, m12))
        for hd in range(DIFF_HEADS):
            nxt = hd + 1
            if nxt < DIFF_HEADS:
                ops = head_operands(nxt)
            accs = [None, None]
            m12 = (neg_inf, neg_inf)
            for c4 in range(n_parts):
                if nxt < DIFF_HEADS:
                    m12 = scores_chunk(nxt, c4, *ops, m12)
                accs = pv_chunk(hd, c4, m_cols, accs)
            a1, a2 = accs
            o = a1[:, 0:LANE] * (1.0 / a1[:, LANE:2 * LANE]) - (a2[:, 0:LANE] * (1.0 / a2[:, LANE:2 * LANE])) * lam
            o_ref[0, :, hd * DIFF_V:(hd + 1) * DIFF_V] = (
                _rms(o, subln_row) * (1.0 - lambda_init)).astype(o_ref.dtype)
            if nxt < DIFF_HEADS:
                m_cols = col_max(window_bias_and_max(nxt, m12))

    @pl.when(jnp.logical_not(fits))
    def _():
        for c in range(nch):
            bkt_ref[c] = _t5_bucket(pch_ref[0, c] - pq)

        def body(hd, carry):
            q1, q2 = _diff_split_q(dq_ref[0, hd])
            tab_b = jnp.broadcast_to(tabt_ref[pl.ds(hd, 1), :] * LOG2E, (tq, LANE))
            m1 = neg_inf
            m2 = neg_inf
            for c4 in range(nch // per):
                kt = jnp.concatenate([dkt_ref[0, hd, c4 * per + t] for t in range(per)], axis=1)
                sa = jnp.dot(q1, kt, preferred_element_type=F32)
                sb = jnp.dot(q2, kt, preferred_element_type=F32)
                for t in range(per):
                    c = c4 * per + t
                    bias = gather_bias(tab_b, bkt_ref[c])
                    va = sa[:, t * LANE:(t + 1) * LANE] + bias
                    vb = sb[:, t * LANE:(t + 1) * LANE] + bias
                    sa0_ref[c] = va
                    sb0_ref[c] = vb
                    m1 = jnp.maximum(m1, va)
                    m2 = jnp.maximum(m2, vb)
            oslow_ref[hd] = _diff_finish(sa0_ref, sb0_ref, m1, m2, dv_ref.at[0, hd], lam, subln_row,
                                         lambda_init)
            return carry

        lax.fori_loop(0, DIFF_HEADS, body, 0)
        for hd in range(DIFF_HEADS):
            o_ref[0, :, hd * DIFF_V:(hd + 1) * DIFF_V] = oslow_ref[hd].astype(o_ref.dtype)


def _diff_attn(positions, rel_bias, dq, dkt, dv, pos_col, pos_tiles, tab_t, lq1, lk1, lq2, lk2, subln,
               lambda_init):
    bsz, nh, seq, _ = dq.shape
    tq = DIFF_TQ
    nch = seq // LANE
    const2 = lambda b, i, *_: (0, 0)
    once = pl.Buffered(1)
    grid_spec = pltpu.PrefetchScalarGridSpec(
        num_scalar_prefetch=2,
        grid=(bsz, seq // tq),
        in_specs=[
            pl.BlockSpec((1, nh, tq, LANE), lambda b, i, *_: (b, 0, i, 0)),
            pl.BlockSpec((1, nh, nch, LANE, LANE), lambda b, i, *_: (b, 0, 0, 0, 0), pipeline_mode=once),
            pl.BlockSpec((1, nh, nch, LANE, LANE), lambda b, i, *_: (b, 0, 0, 0, 0), pipeline_mode=once),
            pl.BlockSpec((1, tq, 1), lambda b, i, *_: (b, i, 0)),
            pl.BlockSpec((1, nch, 1, LANE), lambda b, i, *_: (b, 0, 0, 0)),
            pl.BlockSpec(tab_t.shape, const2),
            pl.BlockSpec((1, DIFF_QK), const2),
            pl.BlockSpec((1, DIFF_QK), const2),
            pl.BlockSpec((1, DIFF_QK), const2),
            pl.BlockSpec((1, DIFF_QK), const2),
            pl.BlockSpec((1, DIFF_V), const2),
        ],
        out_specs=pl.BlockSpec((1, tq, nh * DIFF_V), lambda b, i, *_: (b, i, 0)),
        scratch_shapes=[
            *([pltpu.VMEM((nch, tq, LANE), F32)] * 4),
            pltpu.VMEM((nch, tq, LANE), jnp.int32),
            pltpu.VMEM((nh, tq, LANE), F32),
        ],
    )
    return pl.pallas_call(
        partial(_diff_kernel, lambda_init=lambda_init),
        grid_spec=grid_spec,
        out_shape=jax.ShapeDtypeStruct((bsz, seq, nh * DIFF_V), BF16),
        compiler_params=_params("arbitrary", "arbitrary"),
        name="diff_attn",
    )(positions, rel_bias, dq, dkt, dv.reshape(bsz, nh, nch, LANE, LANE), pos_col, pos_tiles, tab_t,
      lq1, lk1, lq2, lk2, subln)


def _out_proj_kernel(om_ref, od_ref, wa_ref, wb_ref, x_ref, g_ref, post_ref, sh_ref, sc_ref, pre_ref,
                     x1_ref, h2_ref, il_ref):
    _, nph, rows, d = x_ref.shape
    tm = nph * rows
    merge = lambda ref: ref[0].reshape(tm, ref.shape[3])
    y = (jnp.dot(merge(om_ref), wa_ref[...], preferred_element_type=F32)
         + jnp.dot(merge(od_ref), wb_ref[...], preferred_element_type=F32))
    x1 = merge(x_ref) + g_ref[0] * _rms(y, post_ref[...])
    x1_ref[0] = x1.reshape(nph, rows, d)
    a_row = pre_ref[...] * (1.0 + sc_ref[0])
    ms = jnp.mean(x1 * x1, axis=-1, keepdims=True)
    h2 = (x1 * lax.rsqrt(ms + NORM_EPS)) * a_row + sh_ref[0]
    for j in range(il_ref.shape[0]):
        for s in range(nph):
            il_ref[j, pl.ds(s, rows, stride=nph), :] = h2[s * rows:(s + 1) * rows, j * LANE:(j + 1) * LANE]
    for j in range(il_ref.shape[0]):
        h2_ref[:, j * LANE:(j + 1) * LANE] = il_ref[j].astype(h2_ref.dtype)


def _out_proj(o_mla, o_diff, wo_a, wo_b, x2, g_a, post_w, sh_f, sc_f, pre_w, seq):
    m, d = x2.shape
    bsz = m // seq
    tm = OUTPROJ_TM
    rows = tm // ROW_IL
    tpb = seq // tm
    phases = lambda a: a.reshape(bsz, ROW_IL, seq // ROW_IL, a.shape[-1])
    ph_block = lambda w: pl.BlockSpec((1, ROW_IL, rows, w), lambda i: (i // tpb, 0, i % tpb, 0))
    bmap3 = lambda i: (i // tpb, 0, 0)
    x1, h2 = pl.pallas_call(
        _out_proj_kernel,
        grid=(m // tm,),
        in_specs=[
            ph_block(o_mla.shape[1]),
            ph_block(o_diff.shape[1]),
            _resident(wo_a.shape),
            _resident(wo_b.shape),
            ph_block(d),
            pl.BlockSpec((1, 1, d), bmap3),
            _resident((1, d)),
            pl.BlockSpec((1, 1, d), bmap3),
            pl.BlockSpec((1, 1, d), bmap3),
            _resident((1, d)),
        ],
        out_specs=(ph_block(d), pl.BlockSpec((tm, d), lambda i: (i, 0))),
        out_shape=(jax.ShapeDtypeStruct((bsz, ROW_IL, seq // ROW_IL, d), F32),
                   jax.ShapeDtypeStruct((m, d), BF16)),
        scratch_shapes=[pltpu.VMEM((d // LANE, tm, LANE), F32)],
        compiler_params=_params("arbitrary"),
        name="out_proj",
    )(phases(o_mla), phases(o_diff), wo_a, wo_b, phases(x2), g_a, post_w, sh_f, sc_f, pre_w)
    return x1.reshape(m, d), h2


def _gelu_tanh(g):
    a = -2.0 * math.sqrt(2.0 / math.pi) * LOG2E
    return g * (1.0 / (1.0 + jnp.exp2(g * (a + (a * 0.044715) * (g * g)))))


def _neighbour_rows(u, step):
    n = u.shape[0]
    row = lax.broadcasted_iota(jnp.int32, (ROW_IL, u.shape[1]), 0)
    if step == -1:
        wrap = jnp.where(row == 0, 0.0, pltpu.roll(u[n - ROW_IL:], 1, 0))
        return jnp.concatenate([wrap, u[:n - ROW_IL]], axis=0)
    wrap = jnp.where(row == ROW_IL - 1, 0.0, pltpu.roll(u[0:ROW_IL], ROW_IL - 1, 0))
    return jnp.concatenate([u[ROW_IL:], wrap], axis=0)


def _ffn_up_kernel(h_ref, wg_ref, wv_ref, cwg_ref, cwv_ref, cbg_ref, cbv_ref, o_ref):
    h = h_ref[...]

    def conv(u, cw_ref, cb_ref):
        return (cb_ref[...] + u * cw_ref[1:2, :]
                + _neighbour_rows(u, -1) * cw_ref[0:1, :] + _neighbour_rows(u, 1) * cw_ref[2:3, :])

    gate = conv(jnp.dot(h, wg_ref[...].astype(BF16), preferred_element_type=F32), cwg_ref, cbg_ref)
    val = conv(jnp.dot(h, wv_ref[...].astype(BF16), preferred_element_type=F32), cwv_ref, cbv_ref)
    o_ref[...] = (_gelu_tanh(gate) * val).astype(o_ref.dtype)


def _ffn_up(h2, w_up, conv_w, conv_b, seq):
    m, d = h2.shape
    f = w_up.shape[1] // 2
    tn = UP_TN
    nj = f // tn
    return pl.pallas_call(
        _ffn_up_kernel,
        grid=(nj, m // seq),
        in_specs=[
            pl.BlockSpec((seq, d), lambda j, i: (i, 0)),
            pl.BlockSpec((d, tn), lambda j, i: (0, j)),
            pl.BlockSpec((d, tn), lambda j, i: (0, j + nj)),
            pl.BlockSpec((3, tn), lambda j, i: (0, j)),
            pl.BlockSpec((3, tn), lambda j, i: (0, j + nj)),
            pl.BlockSpec((1, tn), lambda j, i: (0, j)),
            pl.BlockSpec((1, tn), lambda j, i: (0, j + nj)),
        ],
        out_specs=pl.BlockSpec((seq, tn), lambda j, i: (i, j)),
        out_shape=jax.ShapeDtypeStruct((m, f), BF16),
        compiler_params=_params("arbitrary", "arbitrary"),
        name="ffn_up",
    )(h2, w_up, w_up, conv_w, conv_w, conv_b, conv_b)


def _ffn_down_kernel(a_ref, w_ref, x1_ref, g_ref, post_ref, o_ref, y_ref):
    tm = a_ref.shape[0]
    y = jnp.dot(a_ref[...], w_ref[...], preferred_element_type=F32)
    yn = _rms(y, post_ref[...])
    g = g_ref[0]
    for j in range(y_ref.shape[0]):
        y_ref[j] = yn[:, j * LANE:(j + 1) * LANE]
    for s in range(ROW_IL):
        for j in range(y_ref.shape[0]):
            cs = slice(j * LANE, (j + 1) * LANE)
            rows = y_ref[j, pl.ds(s, tm // ROW_IL, stride=ROW_IL), :]
            o_ref[0, s, :, cs] = x1_ref[0, s, :, cs] + g[:, cs] * rows


def _ffn_down(act, w_down, x1, g_f, post_w, seq):
    m, f = act.shape
    d = w_down.shape[1]
    tm = DOWN_TM
    tpb = seq // tm
    bsz = m // seq
    nat_block = pl.BlockSpec((1, ROW_IL, tm // ROW_IL, d), lambda i: (i // tpb, 0, i % tpb, 0))
    out = pl.pallas_call(
        _ffn_down_kernel,
        grid=(m // tm,),
        in_specs=[
            pl.BlockSpec((tm, f), lambda i: (i, 0)),
            _resident(w_down.shape),
            nat_block,
            pl.BlockSpec((1, 1, d), lambda i: (i // tpb, 0, 0)),
            _resident((1, d)),
        ],
        out_specs=nat_block,
        out_shape=jax.ShapeDtypeStruct((bsz, ROW_IL, seq // ROW_IL, d), F32),
        scratch_shapes=[pltpu.VMEM((d // LANE, tm, LANE), F32)],
        compiler_params=_params("arbitrary"),
        name="ffn_down",
    )(act, w_down, x1.reshape(bsz, ROW_IL, seq // ROW_IL, d), g_f, post_w)
    return out.reshape(m, d)


def _rope_pair_tiles(w_rope):
    half = MLA_ROPE // 2
    t1, t2 = w_rope[:, :half], w_rope[:, half:]
    z = jnp.zeros((w_rope.shape[0], LANE - MLA_ROPE), w_rope.dtype)
    return jnp.concatenate([t1, t2, z], axis=1), jnp.concatenate([t2, t1, z], axis=1)


def _layout_w_in(w_in):
    o_kv = MLA_Q_RANK
    o_kr = o_kv + MLA_KV_RANK
    o_dq = o_kr + MLA_ROPE
    o_dk = o_dq + DIFF_HEADS * 2 * DIFF_QK
    o_dv = o_dk + DIFF_HEADS * 2 * DIFF_QK
    w1 = jnp.concatenate([w_in[:, :o_kr], w_in[:, o_dq:o_dk], w_in[:, o_dv:]], axis=1).astype(BF16)
    ra, rb = _rope_pair_tiles(w_in[:, o_kr:o_dq])
    wt = jnp.concatenate([ra, rb, w_in[:, o_dk:o_dv]], axis=1).T.astype(BF16)
    return w1, wt


def _layout_w_uq(w_uq):
    k = w_uq.shape[0]
    w3 = w_uq.reshape(k, MLA_HEADS, MLA_NOPE + MLA_ROPE)
    blocks = []
    for hd in range(MLA_HEADS):
        ra, rb = _rope_pair_tiles(w3[:, hd, MLA_NOPE:])
        blocks += [w3[:, hd, :MLA_NOPE], ra, rb]
    return jnp.concatenate(blocks, axis=1).astype(BF16)


def _layout_w_ukv(w_ukv):
    k = w_ukv.shape[0]
    w3 = w_ukv.reshape(k, MLA_HEADS, MLA_NOPE + MLA_V)
    wukt = w3[:, :, :MLA_NOPE].reshape(k, MLA_HEADS * MLA_NOPE).T.astype(BF16)
    wuv = w3[:, :, MLA_NOPE:].reshape(k, MLA_HEADS * MLA_V).astype(BF16)
    return wukt, wuv


def _rope_kernel(pos_ref, inv_ref, cr_ref, sr_ref, ct_ref, st_ref):
    ang = inv_ref[...] * pos_ref[0].astype(F32)
    c = jnp.cos(ang)
    s = jnp.sin(ang)
    zeros = jnp.zeros((LANE - MLA_ROPE, ang.shape[1]), F32)
    ct = jnp.concatenate([c, c, zeros], axis=0)
    st = jnp.concatenate([-s, s, zeros], axis=0)
    ct_ref[0] = ct
    st_ref[0] = st
    cr_ref[0] = ct.T
    sr_ref[0] = st.T


def _rope_tables(pos_row):
    bsz, _, seq = pos_row.shape
    inv = 1.0 / (ROPE_THETA ** (np.arange(0, MLA_ROPE, 2, dtype=np.float32) / MLA_ROPE))
    inv_col = jnp.asarray(inv.astype(np.float32).reshape(MLA_ROPE // 2, 1))
    row_tab = jax.ShapeDtypeStruct((bsz, seq, LANE), F32)
    col_tab = jax.ShapeDtypeStruct((bsz, LANE, seq), F32)
    return pl.pallas_call(
        _rope_kernel,
        grid=(bsz,),
        in_specs=[pl.BlockSpec((1, 1, seq), lambda b: (b, 0, 0)),
                  pl.BlockSpec(inv_col.shape, lambda b: (0, 0))],
        out_specs=(pl.BlockSpec((1, seq, LANE), lambda b: (b, 0, 0)),
                   pl.BlockSpec((1, seq, LANE), lambda b: (b, 0, 0)),
                   pl.BlockSpec((1, LANE, seq), lambda b: (b, 0, 0)),
                   pl.BlockSpec((1, LANE, seq), lambda b: (b, 0, 0))),
        out_shape=(row_tab, row_tab, col_tab, col_tab),
        compiler_params=_params("arbitrary"),
        name="rope_tables",
    )(pos_row, inv_col)


def kernel(x, c, positions, rel_bias, ada_w, ada_b, attn_pre_norm, attn_post_norm, w_in, q_norm, w_uq,
           kv_norm, w_ukv, lambda_q1, lambda_k1, lambda_q2, lambda_k2, diff_subln, w_o, ffn_pre_norm,
           ffn_post_norm, w_up, conv_w, conv_b, w_down):
    bsz, seq, d = x.shape
    depth = ada_w.shape[0]
    m = bsz * seq
    positions = positions.astype(jnp.int32)
    pos_col = positions.reshape(bsz, seq, 1)
    pos_row = positions.reshape(bsz, 1, seq)
    pos_tiles = positions.reshape(bsz, seq // LANE, 1, LANE)
    rope = _rope_tables(pos_row)
    tab_t = jnp.zeros((DIFF_HEADS, LANE), F32).at[:, :REL_BUCKETS].set(rel_bias.astype(F32).T)
    row = lambda v: v.reshape(1, -1).astype(F32)

    x2 = x.reshape(m, d)
    for l in range(depth):
        lambda_init = 0.8 - 0.6 * math.exp(-0.3 * l)
        mod = _ada_mod(c, ada_w[l], ada_b[l])
        sh_a, sc_a, g_a, sh_f, sc_f, g_f = [t.reshape(bsz, 1, d) for t in jnp.split(mod, N_MOD, axis=-1)]

        w1, wt = _layout_w_in(w_in[l])
        wuq = _layout_w_uq(w_uq[l])
        wukt, wuv = _layout_w_ukv(w_ukv[l])
        q, knt, krt, v, dq, dkt, dv = _in_proj(
            x2, rope, sh_a, sc_a, row(attn_pre_norm[l]), w1, wt, row(q_norm[l]), wuq,
            row(kv_norm[l]), wukt, wuv, bsz, seq)

        o_mla = _mla_attn(q, knt, krt, v)
        o_diff = _diff_attn(positions, rel_bias.astype(F32), dq, dkt, dv, pos_col, pos_tiles, tab_t,
                            row(lambda_q1[l]), row(lambda_k1[l]), row(lambda_q2[l]), row(lambda_k2[l]),
                            row(diff_subln[l]), lambda_init)

        wo = w_o[l].astype(BF16)
        n_mla = MLA_HEADS * MLA_V
        x1, h2 = _out_proj(o_mla.reshape(m, n_mla), o_diff.reshape(m, DIFF_HEADS * DIFF_V),
                           wo[:n_mla], wo[n_mla:], x2, g_a, row(attn_post_norm[l]), sh_f, sc_f,
                           row(ffn_pre_norm[l]), seq)

        act = _ffn_up(h2, w_up[l], conv_w[l].astype(F32), row(conv_b[l]), seq)
        x2 = _ffn_down(act, w_down[l].astype(BF16), x1, g_f, row(ffn_post_norm[l]), seq)
    return x2.reshape(bsz, seq, d)
```

```python
import math
from functools import partial

import jax
import jax.numpy as jnp
import numpy as np
from jax import lax
from jax.experimental import pallas as pl
from jax.experimental.pallas import tpu as pltpu

D_MODEL = 2048
MLA_HEADS = 8
MLA_Q_RANK = 512
MLA_KV_RANK = 256
MLA_NOPE = 128
MLA_ROPE = 64
MLA_V = 128
MLA_SCALE = (MLA_NOPE + MLA_ROPE) ** -0.5
DIFF_HEADS = 8
DIFF_QK = 64
DIFF_V = 128
DIFF_SCALE = DIFF_QK ** -0.5
D_FF = 5632
REL_BUCKETS = 32
ROPE_THETA = 10000.0
NORM_EPS = 1e-6
N_MOD = 6
LOG2E = math.log2(math.e)
T5_SATURATE = 128
T5_THRESHOLDS = (12, 16, 23, 32, 46, 64, 91)

LANE = 128
SUBLANE = 8
VMEM_LIMIT_BYTES = 56 * 1024 * 1024

ADA_TN = 1536
INPROJ_TM = 512
ATTN_TQ = 512
DIFF_TQ = 256
ATTN_CK = 512
OUTPROJ_TM = 512
ROW_IL = 8
UP_TN = 512
DOWN_TM = 256

F32 = jnp.float32
BF16 = jnp.bfloat16


def _params(*semantics):
    return pltpu.CompilerParams(dimension_semantics=semantics, vmem_limit_bytes=VMEM_LIMIT_BYTES)


def _resident(shape):
    nd = len(shape)
    return pl.BlockSpec(shape, lambda *_: (0,) * nd, pipeline_mode=pl.Buffered(1))


def _rms(x, w_row):
    ms = jnp.mean(x * x, axis=-1, keepdims=True)
    return (x * lax.rsqrt(ms + NORM_EPS)) * w_row


def _nt_dot(a, b):
    return lax.dot_general(a, b, (((1,), (1,)), ((), ())), preferred_element_type=F32)


def _ada_kernel(c_ref, w_ref, b_ref, o_ref):
    c = c_ref[...]
    c_act = (c * jax.nn.sigmoid(c)).astype(BF16)
    o_ref[...] = jnp.dot(c_act, w_ref[...].astype(BF16), preferred_element_type=F32) + b_ref[...]


def _ada_mod(c, ada_w, ada_b):
    bsz, d = c.shape
    n = ada_w.shape[1]
    return pl.pallas_call(
        _ada_kernel,
        grid=(n // ADA_TN,),
        in_specs=[
            pl.BlockSpec((bsz, d), lambda j: (0, 0)),
            pl.BlockSpec((d, ADA_TN), lambda j: (0, j)),
            pl.BlockSpec((1, ADA_TN), lambda j: (0, j)),
        ],
        out_specs=pl.BlockSpec((bsz, ADA_TN), lambda j: (0, j)),
        out_shape=jax.ShapeDtypeStruct((bsz, n), F32),
        compiler_params=_params("arbitrary"),
        name="ada_mod",
    )(c, ada_w, ada_b.reshape(1, n))


N_W1 = MLA_Q_RANK + MLA_KV_RANK + 2 * DIFF_HEADS * 2 * DIFF_QK
N_WT = 2 * LANE + DIFF_HEADS * 2 * DIFF_QK
Q_HEAD_COLS = 3 * LANE


def _in_proj_kernel(x_ref, cr_ref, sr_ref, ct_ref, st_ref, sh_ref, sc_ref, pre_ref, w1_ref, wt_ref,
                    qn_ref, wuq_ref, kvn_ref, wukt_ref, wuv_ref,
                    q_out, knt_out, krt_out, v_out, dq_out, dkt_out, dv_out):
    x = x_ref[...]
    a_row = pre_ref[...] * (1.0 + sc_ref[0])
    ms = jnp.mean(x * x, axis=-1, keepdims=True)
    h = ((x * lax.rsqrt(ms + NORM_EPS)) * a_row + sh_ref[0]).astype(BF16)

    p1 = jnp.dot(h, w1_ref[...], preferred_element_type=F32)
    p2t = _nt_dot(wt_ref[...], h)

    cos_r, sin_r = cr_ref[0], sr_ref[0]
    cos_t, sin_t = ct_ref[0], st_ref[0]

    q_lat = p1[:, 0:MLA_Q_RANK]
    rq = _rms(q_lat, qn_ref[...] * (MLA_SCALE * LOG2E)).astype(BF16)
    qf = jnp.dot(rq, wuq_ref[...], preferred_element_type=F32)
    for hd in range(MLA_HEADS):
        c0 = hd * Q_HEAD_COLS
        q_out[0, hd, :, 0:LANE] = qf[:, c0:c0 + LANE].astype(BF16)
        rope = qf[:, c0 + LANE:c0 + 2 * LANE] * cos_r + qf[:, c0 + 2 * LANE:c0 + 3 * LANE] * sin_r
        q_out[0, hd, :, LANE:2 * LANE] = rope.astype(BF16)

    kv_lat = p1[:, MLA_Q_RANK:MLA_Q_RANK + MLA_KV_RANK]
    rkv = _rms(kv_lat, kvn_ref[...]).astype(BF16)
    knt = _nt_dot(wukt_ref[...], rkv)
    vv = jnp.dot(rkv, wuv_ref[...], preferred_element_type=F32)
    for hd in range(MLA_HEADS):
        knt_out[0, hd] = knt[hd * LANE:(hd + 1) * LANE, :].astype(BF16)
        v_out[0, hd] = vv[:, hd * LANE:(hd + 1) * LANE].astype(BF16)
    krt_out[0] = (p2t[0:LANE, :] * cos_t + p2t[LANE:2 * LANE, :] * sin_t).astype(BF16)

    o_dq = MLA_Q_RANK + MLA_KV_RANK
    o_dv = o_dq + DIFF_HEADS * 2 * DIFF_QK
    for hd in range(DIFF_HEADS):
        dq_out[0, hd] = (p1[:, o_dq + hd * LANE:o_dq + (hd + 1) * LANE] * (DIFF_SCALE * LOG2E)).astype(BF16)
        dv_out[0, hd] = p1[:, o_dv + hd * LANE:o_dv + (hd + 1) * LANE].astype(BF16)
        dkt = p2t[2 * LANE + hd * LANE:2 * LANE + (hd + 1) * LANE, :].astype(BF16)
        for t in range(dkt_out.shape[2]):
            dkt_out[0, hd, t] = dkt[:, t * LANE:(t + 1) * LANE]


def _in_proj(x2, rope, sh_a, sc_a, pre_w, w1, wt, q_norm, wuq, kv_norm, wukt, wuv, bsz, seq):
    tm = INPROJ_TM
    tpb = seq // tm
    d = x2.shape[1]
    bmap3 = lambda i: (i // tpb, 0, 0)
    out_shapes = (
        jax.ShapeDtypeStruct((bsz, MLA_HEADS, seq, 2 * LANE), BF16),
        jax.ShapeDtypeStruct((bsz, MLA_HEADS, LANE, seq), BF16),
        jax.ShapeDtypeStruct((bsz, LANE, seq), BF16),
        jax.ShapeDtypeStruct((bsz, MLA_HEADS, seq, LANE), BF16),
        jax.ShapeDtypeStruct((bsz, DIFF_HEADS, seq, LANE), BF16),
        jax.ShapeDtypeStruct((bsz, DIFF_HEADS, seq // LANE, LANE, LANE), BF16),
        jax.ShapeDtypeStruct((bsz, DIFF_HEADS, seq, LANE), BF16),
    )
    row_blk = lambda nh, w: pl.BlockSpec((1, nh, tm, w), lambda i: (i // tpb, 0, i % tpb, 0))
    col_blk = lambda nh: pl.BlockSpec((1, nh, LANE, tm), lambda i: (i // tpb, 0, 0, i % tpb))
    return pl.pallas_call(
        _in_proj_kernel,
        grid=(bsz * tpb,),
        in_specs=[
            pl.BlockSpec((tm, d), lambda i: (i, 0)),
            pl.BlockSpec((1, tm, LANE), lambda i: (i // tpb, i % tpb, 0)),
            pl.BlockSpec((1, tm, LANE), lambda i: (i // tpb, i % tpb, 0)),
            pl.BlockSpec((1, LANE, tm), lambda i: (i // tpb, 0, i % tpb)),
            pl.BlockSpec((1, LANE, tm), lambda i: (i // tpb, 0, i % tpb)),
            pl.BlockSpec((1, 1, d), bmap3),
            pl.BlockSpec((1, 1, d), bmap3),
            _resident((1, d)),
            _resident(w1.shape),
            _resident(wt.shape),
            _resident(q_norm.shape),
            _resident(wuq.shape),
            _resident(kv_norm.shape),
            _resident(wukt.shape),
            _resident(wuv.shape),
        ],
        out_specs=(
            row_blk(MLA_HEADS, 2 * LANE),
            col_blk(MLA_HEADS),
            pl.BlockSpec((1, LANE, tm), lambda i: (i // tpb, 0, i % tpb)),
            row_blk(MLA_HEADS, LANE),
            row_blk(DIFF_HEADS, LANE),
            pl.BlockSpec((1, DIFF_HEADS, tm // LANE, LANE, LANE), lambda i: (i // tpb, 0, i % tpb, 0, 0)),
            row_blk(DIFF_HEADS, LANE),
        ),
        out_shape=out_shapes,
        compiler_params=_params("arbitrary"),
        name="in_proj",
    )(x2, *rope, sh_a, sc_a, pre_w, w1, wt, q_norm, wuq, kv_norm, wukt, wuv)


def _lane_tile_reduce(x, op):
    out = x[:, 0:LANE]
    for t in range(1, x.shape[1] // LANE):
        out = op(out, x[:, t * LANE:(t + 1) * LANE])
    return out


def _pv_normalised(chunks, v_ref):
    ones = jnp.ones((ATTN_CK, LANE), BF16)
    acc = None
    for c, p in enumerate(chunks):
        v_aug = jnp.concatenate([v_ref[c * ATTN_CK:(c + 1) * ATTN_CK, :], ones], axis=1)
        pv = jnp.dot(p.astype(BF16), v_aug, preferred_element_type=F32)
        acc = pv if acc is None else acc + pv
    return acc[:, 0:LANE] * (1.0 / acc[:, LANE:2 * LANE])


def _mla_kernel(q_ref, knt_ref, krt_ref, v_ref, o_ref, s_ref):
    seq = s_ref.shape[2]
    n_chunks = seq // ATTN_CK
    ones = jnp.ones((ATTN_CK, LANE), BF16)

    def scores_chunk(hd, c, m_part):
        sl = slice(c * ATTN_CK, (c + 1) * ATTN_CK)
        kt = jnp.concatenate([knt_ref[0, hd, :, sl], krt_ref[0, :, sl]], axis=0)
        s = jnp.dot(q_ref[0, hd], kt, preferred_element_type=F32)
        s_ref[hd % 2, :, sl] = s
        mp = _lane_tile_reduce(s, jnp.maximum)
        return mp if m_part is None else jnp.maximum(m_part, mp)

    def pv_chunk(hd, c, m_col, acc):
        sl = slice(c * ATTN_CK, (c + 1) * ATTN_CK)
        p = jnp.exp2(s_ref[hd % 2, :, sl] - m_col).astype(BF16)
        pv = jnp.dot(p, jnp.concatenate([v_ref[0, hd, sl, :], ones], axis=1), preferred_element_type=F32)
        return pv if acc is None else acc + pv

    m_part = None
    for c in range(n_chunks):
        m_part = scores_chunk(0, c, m_part)
    for hd in range(MLA_HEADS):
        m_col = jnp.max(m_part, axis=-1, keepdims=True)
        m_part = None
        acc = None
        for c in range(n_chunks):
            if hd + 1 < MLA_HEADS:
                m_part = scores_chunk(hd + 1, c, m_part)
            acc = pv_chunk(hd, c, m_col, acc)
        o = acc[:, 0:LANE] * (1.0 / acc[:, LANE:2 * LANE])
        o_ref[0, :, hd * MLA_V:(hd + 1) * MLA_V] = o.astype(o_ref.dtype)


def _mla_attn(q, knt, krt, v):
    bsz, nh, seq, _ = q.shape
    tq = ATTN_TQ
    return pl.pallas_call(
        _mla_kernel,
        grid=(bsz, seq // tq),
        in_specs=[
            pl.BlockSpec((1, nh, tq, 2 * LANE), lambda b, i: (b, 0, i, 0)),
            pl.BlockSpec((1, nh, LANE, seq), lambda b, i: (b, 0, 0, 0)),
            pl.BlockSpec((1, LANE, seq), lambda b, i: (b, 0, 0)),
            pl.BlockSpec((1, nh, seq, LANE), lambda b, i: (b, 0, 0, 0)),
        ],
        out_specs=pl.BlockSpec((1, tq, nh * MLA_V), lambda b, i: (b, i, 0)),
        out_shape=jax.ShapeDtypeStruct((bsz, seq, nh * MLA_V), BF16),
        scratch_shapes=[pltpu.VMEM((2, tq, seq), F32)],
        compiler_params=_params("arbitrary", "arbitrary"),
        name="mla_attn",
    )(q, knt, krt, v)


def _t5_bucket(rel):
    n = jnp.abs(rel)
    large = jnp.full(rel.shape, 8, jnp.int32)
    for t in T5_THRESHOLDS:
        large = large + jnp.where(n >= t, 1, 0)
    return jnp.where(rel > 0, REL_BUCKETS // 2, 0) + jnp.where(n < 8, n, large)


def _diff_split_q(q):
    lane = lax.broadcasted_iota(jnp.int32, q.shape, 1)
    zero = jnp.zeros_like(q)
    return jnp.where(lane < DIFF_QK, q, zero), jnp.where(lane >= DIFF_QK, q, zero)


class _TiledRows:
    def __init__(self, ref):
        self.ref = ref

    def __getitem__(self, idx):
        rows, _ = idx
        tiles = range(rows.start // LANE, rows.stop // LANE)
        return jnp.concatenate([self.ref[t] for t in tiles], axis=0)


def _softmax_pv_tiles(s_ref, m_col, v_ref):
    per = ATTN_CK // LANE
    chunks = []
    for c in range(s_ref.shape[0] // per):
        sv = jnp.concatenate([s_ref[c * per + t] for t in range(per)], axis=1)
        chunks.append(jnp.exp2(sv - m_col))
    return _pv_normalised(chunks, _TiledRows(v_ref))


def _diff_finish(s1_ref, s2_ref, m1, m2, v_ref, lam, subln_row, lambda_init):
    m1c = jnp.max(m1, axis=-1, keepdims=True)
    m2c = jnp.max(m2, axis=-1, keepdims=True)
    o = _softmax_pv_tiles(s1_ref, m1c, v_ref) - _softmax_pv_tiles(s2_ref, m2c, v_ref) * lam
    return _rms(o, subln_row) * (1.0 - lambda_init)


def _bias_rows(t_neg, t_pos, ws, nw, seq):
    shape = (2 * SUBLANE, seq)
    tile = lax.broadcasted_iota(jnp.int32, shape, 1) // LANE
    cv = jnp.where(tile < nw, 0.0, jnp.where(tile + ws >= seq // LANE, t_neg, t_pos))
    hi = cv.astype(BF16).astype(F32)
    row = lax.broadcasted_iota(jnp.int32, shape, 0)
    return jnp.where(row == 0, hi, jnp.where(row == 1, cv - hi, 0.0)).astype(BF16)


def _diff_kernel(pos_sm, tab_sm, dq_ref, dkt_ref, dv_ref, pcol_ref, pch_ref, tabt_ref,
                 lq1_ref, lk1_ref, lq2_ref, lk2_ref, subln_ref, o_ref,
                 sa0_ref, sb0_ref, sa1_ref, sb1_ref, bkt_ref, oslow_ref, *, lambda_init):
    b = pl.program_id(0)
    qi = pl.program_id(1)
    nch, tq, _ = sa0_ref.shape
    nw = tq // LANE + 2
    per = ATTN_CK // LANE

    q_lo = pos_sm[b, qi * tq]
    q_hi = pos_sm[b, qi * tq + tq - 1]
    n_neg = jnp.int32(0)
    n_pos = jnp.int32(0)
    for c in range(nch):
        n_neg = n_neg + (pos_sm[b, c * LANE + LANE - 1] - q_lo <= -T5_SATURATE).astype(jnp.int32)
        n_pos = n_pos + (pos_sm[b, c * LANE] - q_hi >= T5_SATURATE).astype(jnp.int32)
    fits = (nch - n_neg - n_pos) <= nw
    ws = jnp.minimum(n_neg, nch - nw)

    pq = pcol_ref[0]
    lam = (jnp.exp(jnp.sum(lq1_ref[...] * lk1_ref[...], axis=-1, keepdims=True))
           - jnp.exp(jnp.sum(lq2_ref[...] * lk2_ref[...], axis=-1, keepdims=True)) + lambda_init)
    subln_row = subln_ref[...]
    neg_inf = jnp.full((tq, LANE), -jnp.inf, F32)
    slots = ((sa0_ref, sb0_ref), (sa1_ref, sb1_ref))
    pick2 = jnp.where(lax.broadcasted_iota(jnp.int32, (tq, LANE), 1) < 2, 1.0, 0.0).astype(BF16)
    kt_pad = jnp.zeros((LANE - 2 * SUBLANE, ATTN_CK), BF16)

    def gather_bias(tab_b, bkt):
        return jnp.take_along_axis(tab_b, bkt, axis=1, mode="promise_in_bounds")

    @pl.when(fits)
    def _():
        for j in range(nw):
            bkt_ref[j] = _t5_bucket(pch_ref[0, ws + j] - pq)
        ones = jnp.ones((ATTN_CK, LANE), BF16)

        def head_operands(hd):
            t_neg = tab_sm[REL_BUCKETS // 2 - 1, hd] * LOG2E
            t_pos = tab_sm[REL_BUCKETS - 1, hd] * LOG2E
            q1, q2 = _diff_split_q(dq_ref[0, hd])
            return (jnp.concatenate([q1, pick2], axis=1), jnp.concatenate([q2, pick2], axis=1),
                    _bias_rows(t_neg, t_pos, ws, nw, nch * LANE))

        tile_at = [jnp.where(ws + j >= nch, ws + j - nch, ws + j) for j in range(nch)]

        def scores_chunk(hd, c4, q1, q2, rows, m12):
            s1_ref, s2_ref = slots[hd % len(slots)]
            m1, m2 = m12
            sl = slice(c4 * ATTN_CK, (c4 + 1) * ATTN_CK)
            k_tiles = jnp.concatenate([dkt_ref[0, hd, tile_at[c4 * per + t]] for t in range(per)], axis=1)
            kt = jnp.concatenate([k_tiles, rows[:, sl], kt_pad], axis=0)
            sa = jnp.dot(q1, kt, preferred_element_type=F32)
            sb = jnp.dot(q2, kt, preferred_element_type=F32)
            for t in range(per):
                j = c4 * per + t
                va = sa[:, t * LANE:(t + 1) * LANE]
                vb = sb[:, t * LANE:(t + 1) * LANE]
                s1_ref[j] = va
                s2_ref[j] = vb
                if j >= nw:
                    m1 = jnp.maximum(m1, va)
                    m2 = jnp.maximum(m2, vb)
            return m1, m2

        def window_bias_and_max(hd, m12):
            s1_ref, s2_ref = slots[hd % len(slots)]
            m1, m2 = m12
            tab_b = jnp.broadcast_to(tabt_ref[hd:hd + 1, :] * LOG2E, (tq, LANE))
            for j in range(nw):
                bias = gather_bias(tab_b, bkt_ref[j])
                va = s1_ref[j] + bias
                vb = s2_ref[j] + bias
                s1_ref[j] = va
                s2_ref[j] = vb
                m1 = jnp.maximum(m1, va)
                m2 = jnp.maximum(m2, vb)
            return m1, m2

        def pv_chunk(hd, c4, m_cols, accs):
            v_rows = jnp.concatenate([dv_ref[0, hd, tile_at[c4 * per + t]] for t in range(per)], axis=0)
            v_aug = jnp.concatenate([v_rows, ones], axis=1)
            out = []
            for s_ref, m_col, acc in zip(slots[hd % len(slots)], m_cols, accs):
                sv = jnp.concatenate([s_ref[c4 * per + t] for t in range(per)], axis=1)
                pv = jnp.dot(jnp.exp2(sv - m_col).astype(BF16), v_aug, preferred_element_type=F32)
                out.append(pv if acc is None else acc + pv)
            return out

        n_parts = nch // per
        col_max = lambda m12: tuple(jnp.max(m, axis=-1, keepdims=True) for m in m12)
        ops = head_operands(0)
        m12 = (neg_inf, neg_inf)
        for c4 in range(n_parts):
            m12 = scores_chunk(0, c4, *ops, m12)
        m_cols = col_max(window_bias_and_max(0, m12))
        for hd in range(DIFF_HEADS):
            nxt = hd + 1
            if nxt < DIFF_HEADS:
                ops = head_operands(nxt)
            accs = [None, None]
            m12 = (neg_inf, neg_inf)
            for c4 in range(n_parts):
                if nxt < DIFF_HEADS:
                    m12 = scores_chunk(nxt, c4, *ops, m12)
                accs = pv_chunk(hd, c4, m_cols, accs)
            a1, a2 = accs
            o = a1[:, 0:LANE] * (1.0 / a1[:, LANE:2 * LANE]) - (a2[:, 0:LANE] * (1.0 / a2[:, LANE:2 * LANE])) * lam
            o_ref[0, :, hd * DIFF_V:(hd + 1) * DIFF_V] = (
                _rms(o, subln_row) * (1.0 - lambda_init)).astype(o_ref.dtype)
            if nxt < DIFF_HEADS:
                m_cols = col_max(window_bias_and_max(nxt, m12))

    @pl.when(jnp.logical_not(fits))
    def _():
        for c in range(nch):
            bkt_ref[c] = _t5_bucket(pch_ref[0, c] - pq)

        def body(hd, carry):
            q1, q2 = _diff_split_q(dq_ref[0, hd])
            tab_b = jnp.broadcast_to(tabt_ref[pl.ds(hd, 1), :] * LOG2E, (tq, LANE))
            m1 = neg_inf
            m2 = neg_inf
            for c4 in range(nch // per):
                kt = jnp.concatenate([dkt_ref[0, hd, c4 * per + t] for t in range(per)], axis=1)
                sa = jnp.dot(q1, kt, preferred_element_type=F32)
                sb = jnp.dot(q2, kt, preferred_element_type=F32)
                for t in range(per):
                    c = c4 * per + t
                    bias = gather_bias(tab_b, bkt_ref[c])
                    va = sa[:, t * LANE:(t + 1) * LANE] + bias
                    vb = sb[:, t * LANE:(t + 1) * LANE] + bias
                    sa0_ref[c] = va
                    sb0_ref[c] = vb
                    m1 = jnp.maximum(m1, va)
                    m2 = jnp.maximum(m2, vb)
            oslow_ref[hd] = _diff_finish(sa0_ref, sb0_ref, m1, m2, dv_ref.at[0, hd], lam, subln_row,
                                         lambda_init)
            return carry

        lax.fori_loop(0, DIFF_HEADS, body, 0)
        for hd in range(DIFF_HEADS):
            o_ref[0, :, hd * DIFF_V:(hd + 1) * DIFF_V] = oslow_ref[hd].astype(o_ref.dtype)


def _diff_attn(positions, rel_bias, dq, dkt, dv, pos_col, pos_tiles, tab_t, lq1, lk1, lq2, lk2, subln,
               lambda_init):
    bsz, nh, seq, _ = dq.shape
    tq = DIFF_TQ
    nch = seq // LANE
    const2 = lambda b, i, *_: (0, 0)
    once = pl.Buffered(1)
    grid_spec = pltpu.PrefetchScalarGridSpec(
        num_scalar_prefetch=2,
        grid=(bsz, seq // tq),
        in_specs=[
            pl.BlockSpec((1, nh, tq, LANE), lambda b, i, *_: (b, 0, i, 0)),
            pl.BlockSpec((1, nh, nch, LANE, LANE), lambda b, i, *_: (b, 0, 0, 0, 0), pipeline_mode=once),
            pl.BlockSpec((1, nh, nch, LANE, LANE), lambda b, i, *_: (b, 0, 0, 0, 0), pipeline_mode=once),
            pl.BlockSpec((1, tq, 1), lambda b, i, *_: (b, i, 0)),
            pl.BlockSpec((1, nch, 1, LANE), lambda b, i, *_: (b, 0, 0, 0)),
            pl.BlockSpec(tab_t.shape, const2),
            pl.BlockSpec((1, DIFF_QK), const2),
            pl.BlockSpec((1, DIFF_QK), const2),
            pl.BlockSpec((1, DIFF_QK), const2),
            pl.BlockSpec((1, DIFF_QK), const2),
            pl.BlockSpec((1, DIFF_V), const2),
        ],
        out_specs=pl.BlockSpec((1, tq, nh * DIFF_V), lambda b, i, *_: (b, i, 0)),
        scratch_shapes=[
            *([pltpu.VMEM((nch, tq, LANE), F32)] * 4),
            pltpu.VMEM((nch, tq, LANE), jnp.int32),
            pltpu.VMEM((nh, tq, LANE), F32),
        ],
    )
    return pl.pallas_call(
        partial(_diff_kernel, lambda_init=lambda_init),
        grid_spec=grid_spec,
        out_shape=jax.ShapeDtypeStruct((bsz, seq, nh * DIFF_V), BF16),
        compiler_params=_params("arbitrary", "arbitrary"),
        name="diff_attn",
    )(positions, rel_bias, dq, dkt, dv.reshape(bsz, nh, nch, LANE, LANE), pos_col, pos_tiles, tab_t,
      lq1, lk1, lq2, lk2, subln)


def _out_proj_kernel(om_ref, od_ref, wa_ref, wb_ref, x_ref, g_ref, post_ref, sh_ref, sc_ref, pre_ref,
                     x1_ref, h2_ref, il_ref):
    _, nph, rows, d = x_ref.shape
    tm = nph * rows
    merge = lambda ref: ref[0].reshape(tm, ref.shape[3])
    y = (jnp.dot(merge(om_ref), wa_ref[...], preferred_element_type=F32)
         + jnp.dot(merge(od_ref), wb_ref[...], preferred_element_type=F32))
    x1 = merge(x_ref) + g_ref[0] * _rms(y, post_ref[...])
    x1_ref[0] = x1.reshape(nph, rows, d)
    a_row = pre_ref[...] * (1.0 + sc_ref[0])
    ms = jnp.mean(x1 * x1, axis=-1, keepdims=True)
    h2 = (x1 * lax.rsqrt(ms + NORM_EPS)) * a_row + sh_ref[0]
    for j in range(il_ref.shape[0]):
        for s in range(nph):
            il_ref[j, pl.ds(s, rows, stride=nph), :] = h2[s * rows:(s + 1) * rows, j * LANE:(j + 1) * LANE]
    for j in range(il_ref.shape[0]):
        h2_ref[:, j * LANE:(j + 1) * LANE] = il_ref[j].astype(h2_ref.dtype)


def _out_proj(o_mla, o_diff, wo_a, wo_b, x2, g_a, post_w, sh_f, sc_f, pre_w, seq):
    m, d = x2.shape
    bsz = m // seq
    tm = OUTPROJ_TM
    rows = tm // ROW_IL
    tpb = seq // tm
    phases = lambda a: a.reshape(bsz, ROW_IL, seq // ROW_IL, a.shape[-1])
    ph_block = lambda w: pl.BlockSpec((1, ROW_IL, rows, w), lambda i: (i // tpb, 0, i % tpb, 0))
    bmap3 = lambda i: (i // tpb, 0, 0)
    x1, h2 = pl.pallas_call(
        _out_proj_kernel,
        grid=(m // tm,),
        in_specs=[
            ph_block(o_mla.shape[1]),
            ph_block(o_diff.shape[1]),
            _resident(wo_a.shape),
            _resident(wo_b.shape),
            ph_block(d),
            pl.BlockSpec((1, 1, d), bmap3),
            _resident((1, d)),
            pl.BlockSpec((1, 1, d), bmap3),
            pl.BlockSpec((1, 1, d), bmap3),
            _resident((1, d)),
        ],
        out_specs=(ph_block(d), pl.BlockSpec((tm, d), lambda i: (i, 0))),
        out_shape=(jax.ShapeDtypeStruct((bsz, ROW_IL, seq // ROW_IL, d), F32),
                   jax.ShapeDtypeStruct((m, d), BF16)),
        scratch_shapes=[pltpu.VMEM((d // LANE, tm, LANE), F32)],
        compiler_params=_params("arbitrary"),
        name="out_proj",
    )(phases(o_mla), phases(o_diff), wo_a, wo_b, phases(x2), g_a, post_w, sh_f, sc_f, pre_w)
    return x1.reshape(m, d), h2


def _gelu_tanh(g):
    a = -2.0 * math.sqrt(2.0 / math.pi) * LOG2E
    return g * (1.0 / (1.0 + jnp.exp2(g * (a + (a * 0.044715) * (g * g)))))


def _neighbour_rows(u, step):
    n = u.shape[0]
    row = lax.broadcasted_iota(jnp.int32, (ROW_IL, u.shape[1]), 0)
    if step == -1:
        wrap = jnp.where(row == 0, 0.0, pltpu.roll(u[n - ROW_IL:], 1, 0))
        return jnp.concatenate([wrap, u[:n - ROW_IL]], axis=0)
    wrap = jnp.where(row == ROW_IL - 1, 0.0, pltpu.roll(u[0:ROW_IL], ROW_IL - 1, 0))
    return jnp.concatenate([u[ROW_IL:], wrap], axis=0)


def _ffn_up_kernel(h_ref, wg_ref, wv_ref, cwg_ref, cwv_ref, cbg_ref, cbv_ref, o_ref):
    h = h_ref[...]

    def conv(u, cw_ref, cb_ref):
        return (cb_ref[...] + u * cw_ref[1:2, :]
                + _neighbour_rows(u, -1) * cw_ref[0:1, :] + _neighbour_rows(u, 1) * cw_ref[2:3, :])

    gate = conv(jnp.dot(h, wg_ref[...].astype(BF16), preferred_element_type=F32), cwg_ref, cbg_ref)
    val = conv(jnp.dot(h, wv_ref[...].astype(BF16), preferred_element_type=F32), cwv_ref, cbv_ref)
    o_ref[...] = (_gelu_tanh(gate) * val).astype(o_ref.dtype)


def _ffn_up(h2, w_up, conv_w, conv_b, seq):
    m, d = h2.shape
    f = w_up.shape[1] // 2
    tn = UP_TN
    nj = f // tn
    return pl.pallas_call(
        _ffn_up_kernel,
        grid=(nj, m // seq),
        in_specs=[
            pl.BlockSpec((seq, d), lambda j, i: (i, 0)),
            pl.BlockSpec((d, tn), lambda j, i: (0, j)),
            pl.BlockSpec((d, tn), lambda j, i: (0, j + nj)),
            pl.BlockSpec((3, tn), lambda j, i: (0, j)),
            pl.BlockSpec((3, tn), lambda j, i: (0, j + nj)),
            pl.BlockSpec((1, tn), lambda j, i: (0, j)),
            pl.BlockSpec((1, tn), lambda j, i: (0, j + nj)),
        ],
        out_specs=pl.BlockSpec((seq, tn), lambda j, i: (i, j)),
        out_shape=jax.ShapeDtypeStruct((m, f), BF16),
        compiler_params=_params("arbitrary", "arbitrary"),
        name="ffn_up",
    )(h2, w_up, w_up, conv_w, conv_w, conv_b, conv_b)


def _ffn_down_kernel(a_ref, w_ref, x1_ref, g_ref, post_ref, o_ref, y_ref):
    tm = a_ref.shape[0]
    y = jnp.dot(a_ref[...], w_ref[...], preferred_element_type=F32)
    yn = _rms(y, post_ref[...])
    g = g_ref[0]
    for j in range(y_ref.shape[0]):
        y_ref[j] = yn[:, j * LANE:(j + 1) * LANE]
    for s in range(ROW_IL):
        for j in range(y_ref.shape[0]):
            cs = slice(j * LANE, (j + 1) * LANE)
            rows = y_ref[j, pl.ds(s, tm // ROW_IL, stride=ROW_IL), :]
            o_ref[0, s, :, cs] = x1_ref[0, s, :, cs] + g[:, cs] * rows


def _ffn_down(act, w_down, x1, g_f, post_w, seq):
    m, f = act.shape
    d = w_down.shape[1]
    tm = DOWN_TM
    tpb = seq // tm
    bsz = m // seq
    nat_block = pl.BlockSpec((1, ROW_IL, tm // ROW_IL, d), lambda i: (i // tpb, 0, i % tpb, 0))
    out = pl.pallas_call(
        _ffn_down_kernel,
        grid=(m // tm,),
        in_specs=[
            pl.BlockSpec((tm, f), lambda i: (i, 0)),
            _resident(w_down.shape),
            nat_block,
            pl.BlockSpec((1, 1, d), lambda i: (i // tpb, 0, 0)),
            _resident((1, d)),
        ],
        out_specs=nat_block,
        out_shape=jax.ShapeDtypeStruct((bsz, ROW_IL, seq // ROW_IL, d), F32),
        scratch_shapes=[pltpu.VMEM((d // LANE, tm, LANE), F32)],
        compiler_params=_params("arbitrary"),
        name="ffn_down",
    )(act, w_down, x1.reshape(bsz, ROW_IL, seq // ROW_IL, d), g_f, post_w)
    return out.reshape(m, d)


def _rope_pair_tiles(w_rope):
    half = MLA_ROPE // 2
    t1, t2 = w_rope[:, :half], w_rope[:, half:]
    z = jnp.zeros((w_rope.shape[0], LANE - MLA_ROPE), w_rope.dtype)
    return jnp.concatenate([t1, t2, z], axis=1), jnp.concatenate([t2, t1, z], axis=1)


def _layout_w_in(w_in):
    o_kv = MLA_Q_RANK
    o_kr = o_kv + MLA_KV_RANK
    o_dq = o_kr + MLA_ROPE
    o_dk = o_dq + DIFF_HEADS * 2 * DIFF_QK
    o_dv = o_dk + DIFF_HEADS * 2 * DIFF_QK
    w1 = jnp.concatenate([w_in[:, :o_kr], w_in[:, o_dq:o_dk], w_in[:, o_dv:]], axis=1).astype(BF16)
    ra, rb = _rope_pair_tiles(w_in[:, o_kr:o_dq])
    wt = jnp.concatenate([ra, rb, w_in[:, o_dk:o_dv]], axis=1).T.astype(BF16)
    return w1, wt


def _layout_w_uq(w_uq):
    k = w_uq.shape[0]
    w3 = w_uq.reshape(k, MLA_HEADS, MLA_NOPE + MLA_ROPE)
    blocks = []
    for hd in range(MLA_HEADS):
        ra, rb = _rope_pair_tiles(w3[:, hd, MLA_NOPE:])
        blocks += [w3[:, hd, :MLA_NOPE], ra, rb]
    return jnp.concatenate(blocks, axis=1).astype(BF16)


def _layout_w_ukv(w_ukv):
    k = w_ukv.shape[0]
    w3 = w_ukv.reshape(k, MLA_HEADS, MLA_NOPE + MLA_V)
    wukt = w3[:, :, :MLA_NOPE].reshape(k, MLA_HEADS * MLA_NOPE).T.astype(BF16)
    wuv = w3[:, :, MLA_NOPE:].reshape(k, MLA_HEADS * MLA_V).astype(BF16)
    return wukt, wuv


def _rope_kernel(pos_ref, inv_ref, cr_ref, sr_ref, ct_ref, st_ref):
    ang = inv_ref[...] * pos_ref[0].astype(F32)
    c = jnp.cos(ang)
    s = jnp.sin(ang)
    zeros = jnp.zeros((LANE - MLA_ROPE, ang.shape[1]), F32)
    ct = jnp.concatenate([c, c, zeros], axis=0)
    st = jnp.concatenate([-s, s, zeros], axis=0)
    ct_ref[0] = ct
    st_ref[0] = st
    cr_ref[0] = ct.T
    sr_ref[0] = st.T


def _rope_tables(pos_row):
    bsz, _, seq = pos_row.shape
    inv = 1.0 / (ROPE_THETA ** (np.arange(0, MLA_ROPE, 2, dtype=np.float32) / MLA_ROPE))
    inv_col = jnp.asarray(inv.astype(np.float32).reshape(MLA_ROPE // 2, 1))
    row_tab = jax.ShapeDtypeStruct((bsz, seq, LANE), F32)
    col_tab = jax.ShapeDtypeStruct((bsz, LANE, seq), F32)
    return pl.pallas_call(
        _rope_kernel,
        grid=(bsz,),
        in_specs=[pl.BlockSpec((1, 1, seq), lambda b: (b, 0, 0)),
                  pl.BlockSpec(inv_col.shape, lambda b: (0, 0))],
        out_specs=(pl.BlockSpec((1, seq, LANE), lambda b: (b, 0, 0)),
                   pl.BlockSpec((1, seq, LANE), lambda b: (b, 0, 0)),
                   pl.BlockSpec((1, LANE, seq), lambda b: (b, 0, 0)),
                   pl.BlockSpec((1, LANE, seq), lambda b: (b, 0, 0))),
        out_shape=(row_tab, row_tab, col_tab, col_tab),
        compiler_params=_params("arbitrary"),
        name="rope_tables",
    )(pos_row, inv_col)


def kernel(x, c, positions, rel_bias, ada_w, ada_b, attn_pre_norm, attn_post_norm, w_in, q_norm, w_uq,
           kv_norm, w_ukv, lambda_q1, lambda_k1, lambda_q2, lambda_k2, diff_subln, w_o, ffn_pre_norm,
           ffn_post_norm, w_up, conv_w, conv_b, w_down):
    bsz, seq, d = x.shape
    depth = ada_w.shape[0]
    m = bsz * seq
    positions = positions.astype(jnp.int32)
    pos_col = positions.reshape(bsz, seq, 1)
    pos_row = positions.reshape(bsz, 1, seq)
    pos_tiles = positions.reshape(bsz, seq // LANE, 1, LANE)
    rope = _rope_tables(pos_row)
    tab_t = jnp.zeros((DIFF_HEADS, LANE), F32).at[:, :REL_BUCKETS].set(rel_bias.astype(F32).T)
    row = lambda v: v.reshape(1, -1).astype(F32)

    x2 = x.reshape(m, d)
    for l in range(depth):
        lambda_init = 0.8 - 0.6 * math.exp(-0.3 * l)
        mod = _ada_mod(c, ada_w[l], ada_b[l])
        sh_a, sc_a, g_a, sh_f, sc_f, g_f = [t.reshape(bsz, 1, d) for t in jnp.split(mod, N_MOD, axis=-1)]

        w1, wt = _layout_w_in(w_in[l])
        wuq = _layout_w_uq(w_uq[l])
        wukt, wuv = _layout_w_ukv(w_ukv[l])
        q, knt, krt, v, dq, dkt, dv = _in_proj(
            x2, rope, sh_a, sc_a, row(attn_pre_norm[l]), w1, wt, row(q_norm[l]), wuq,
            row(kv_norm[l]), wukt, wuv, bsz, seq)

        o_mla = _mla_attn(q, knt, krt, v)
        o_diff = _diff_attn(positions, rel_bias.astype(F32), dq, dkt, dv, pos_col, pos_tiles, tab_t,
                            row(lambda_q1[l]), row(lambda_k1[l]), row(lambda_q2[l]), row(lambda_k2[l]),
                            row(diff_subln[l]), lambda_init)

        wo = w_o[l].astype(BF16)
        n_mla = MLA_HEADS * MLA_V
        x1, h2 = _out_proj(o_mla.reshape(m, n_mla), o_diff.reshape(m, DIFF_HEADS * DIFF_V),
                           wo[:n_mla], wo[n_mla:], x2, g_a, row(attn_post_norm[l]), sh_f, sc_f,
                           row(ffn_pre_norm[l]), seq)

        act = _ffn_up(h2, w_up[l], conv_w[l].astype(F32), row(conv_b[l]), seq)
        x2 = _ffn_down(act, w_down[l].astype(BF16), x1, g_f, row(ffn_post_norm[l]), seq)
    return x2.reshape(bsz, seq, d)
```

```python
import math
from functools import partial

import jax
import jax.numpy as jnp
import numpy as np
from jax import lax
from jax.experimental import pallas as pl
from jax.experimental.pallas import tpu as pltpu

MLA_HEADS = 8
MLA_Q_RANK = 512
MLA_KV_RANK = 256
MLA_NOPE = 128
MLA_ROPE = 64
MLA_V = 128
MLA_SCALE = (MLA_NOPE + MLA_ROPE) ** -0.5
DIFF_HEADS = 8
DIFF_QK = 64
DIFF_V = 128
DIFF_SCALE = DIFF_QK ** -0.5
REL_BUCKETS = 32
ROPE_THETA = 10000.0
NORM_EPS = 1e-6
N_MOD = 6
LOG2E = math.log2(math.e)
T5_SATURATE = 128
T5_THRESHOLDS = (12, 16, 23, 32, 46, 64, 91)

LANE = 128
SUBLANE = 8
VMEM_LIMIT_BYTES = 56 * 1024 * 1024

ADA_TN = 1536
INPROJ_TM = 512
ATTN_TQ = 512
DIFF_TQ = 256
ATTN_CK = 512
OUTPROJ_TM = 512
ROW_IL = 8
UP_TN = 512
DOWN_TM = 256

F32 = jnp.float32
BF16 = jnp.bfloat16


def _params(*semantics):
    return pltpu.CompilerParams(dimension_semantics=semantics, vmem_limit_bytes=VMEM_LIMIT_BYTES)


def _resident(shape):
    nd = len(shape)
    return pl.BlockSpec(shape, lambda *_: (0,) * nd, pipeline_mode=pl.Buffered(1))


def _rms(x, w_row):
    ms = jnp.mean(x * x, axis=-1, keepdims=True)
    return (x * lax.rsqrt(ms + NORM_EPS)) * w_row


def _nt_dot(a, b):
    return lax.dot_general(a, b, (((1,), (1,)), ((), ())), preferred_element_type=F32)


def _ada_kernel(c_ref, w_ref, b_ref, o_ref):
    c = c_ref[...]
    c_act = (c * jax.nn.sigmoid(c)).astype(BF16)
    o_ref[...] = jnp.dot(c_act, w_ref[...].astype(BF16), preferred_element_type=F32) + b_ref[...]


def _ada_mod(c, ada_w, ada_b):
    bsz, d = c.shape
    n = ada_w.shape[1]
    return pl.pallas_call(
        _ada_kernel,
        grid=(n // ADA_TN,),
        in_specs=[
            pl.BlockSpec((bsz, d), lambda j: (0, 0)),
            pl.BlockSpec((d, ADA_TN), lambda j: (0, j)),
            pl.BlockSpec((1, ADA_TN), lambda j: (0, j)),
        ],
        out_specs=pl.BlockSpec((bsz, ADA_TN), lambda j: (0, j)),
        out_shape=jax.ShapeDtypeStruct((bsz, n), F32),
        compiler_params=_params("arbitrary"),
        name="ada_mod",
    )(c, ada_w, ada_b.reshape(1, n))


N_W1 = MLA_Q_RANK + MLA_KV_RANK + 2 * DIFF_HEADS * 2 * DIFF_QK
N_WT = 2 * LANE + DIFF_HEADS * 2 * DIFF_QK
Q_HEAD_COLS = 3 * LANE


def _in_proj_kernel(x_ref, cr_ref, sr_ref, ct_ref, st_ref, sh_ref, sc_ref, pre_ref, w1_ref, wt_ref,
                    qn_ref, wuq_ref, kvn_ref, wukt_ref, wuv_ref,
                    q_out, knt_out, krt_out, v_out, dq_out, dkt_out, dv_out):
    x = x_ref[...]
    a_row = pre_ref[...] * (1.0 + sc_ref[0])
    ms = jnp.mean(x * x, axis=-1, keepdims=True)
    h = ((x * lax.rsqrt(ms + NORM_EPS)) * a_row + sh_ref[0]).astype(BF16)

    p1 = jnp.dot(h, w1_ref[...], preferred_element_type=F32)
    p2t = _nt_dot(wt_ref[...], h)

    cos_r, sin_r = cr_ref[0], sr_ref[0]
    cos_t, sin_t = ct_ref[0], st_ref[0]

    q_lat = p1[:, 0:MLA_Q_RANK]
    rq = _rms(q_lat, qn_ref[...] * (MLA_SCALE * LOG2E)).astype(BF16)
    qf = jnp.dot(rq, wuq_ref[...], preferred_element_type=F32)
    for hd in range(MLA_HEADS):
        c0 = hd * Q_HEAD_COLS
        q_out[0, hd, :, 0:LANE] = qf[:, c0:c0 + LANE].astype(BF16)
        rope = qf[:, c0 + LANE:c0 + 2 * LANE] * cos_r + qf[:, c0 + 2 * LANE:c0 + 3 * LANE] * sin_r
        q_out[0, hd, :, LANE:2 * LANE] = rope.astype(BF16)

    kv_lat = p1[:, MLA_Q_RANK:MLA_Q_RANK + MLA_KV_RANK]
    rkv = _rms(kv_lat, kvn_ref[...]).astype(BF16)
    knt = _nt_dot(wukt_ref[...], rkv)
    vv = jnp.dot(rkv, wuv_ref[...], preferred_element_type=F32)
    for hd in range(MLA_HEADS):
        knt_out[0, hd] = knt[hd * LANE:(hd + 1) * LANE, :].astype(BF16)
        v_out[0, hd] = vv[:, hd * LANE:(hd + 1) * LANE].astype(BF16)
    krt_out[0] = (p2t[0:LANE, :] * cos_t + p2t[LANE:2 * LANE, :] * sin_t).astype(BF16)

    o_dq = MLA_Q_RANK + MLA_KV_RANK
    o_dv = o_dq + DIFF_HEADS * 2 * DIFF_QK
    for hd in range(DIFF_HEADS):
        dq_out[0, hd] = (p1[:, o_dq + hd * LANE:o_dq + (hd + 1) * LANE] * (DIFF_SCALE * LOG2E)).astype(BF16)
        dv_out[0, hd] = p1[:, o_dv + hd * LANE:o_dv + (hd + 1) * LANE].astype(BF16)
        dkt = p2t[2 * LANE + hd * LANE:2 * LANE + (hd + 1) * LANE, :].astype(BF16)
        for t in range(dkt_out.shape[2]):
            dkt_out[0, hd, t] = dkt[:, t * LANE:(t + 1) * LANE]


def _in_proj(x2, rope, sh_a, sc_a, pre_w, w1, wt, q_norm, wuq, kv_norm, wukt, wuv, bsz, seq):
    tm = INPROJ_TM
    tpb = seq // tm
    d = x2.shape[1]
    bmap3 = lambda i: (i // tpb, 0, 0)
    out_shapes = (
        jax.ShapeDtypeStruct((bsz, MLA_HEADS, seq, 2 * LANE), BF16),
        jax.ShapeDtypeStruct((bsz, MLA_HEADS, LANE, seq), BF16),
        jax.ShapeDtypeStruct((bsz, LANE, seq), BF16),
        jax.ShapeDtypeStruct((bsz, MLA_HEADS, seq, LANE), BF16),
        jax.ShapeDtypeStruct((bsz, DIFF_HEADS, seq, LANE), BF16),
        jax.ShapeDtypeStruct((bsz, DIFF_HEADS, seq // LANE, LANE, LANE), BF16),
        jax.ShapeDtypeStruct((bsz, DIFF_HEADS, seq, LANE), BF16),
    )
    row_blk = lambda nh, w: pl.BlockSpec((1, nh, tm, w), lambda i: (i // tpb, 0, i % tpb, 0))
    col_blk = lambda nh: pl.BlockSpec((1, nh, LANE, tm), lambda i: (i // tpb, 0, 0, i % tpb))
    return pl.pallas_call(
        _in_proj_kernel,
        grid=(bsz * tpb,),
        in_specs=[
            pl.BlockSpec((tm, d), lambda i: (i, 0)),
            pl.BlockSpec((1, tm, LANE), lambda i: (i // tpb, i % tpb, 0)),
            pl.BlockSpec((1, tm, LANE), lambda i: (i // tpb, i % tpb, 0)),
            pl.BlockSpec((1, LANE, tm), lambda i: (i // tpb, 0, i % tpb)),
            pl.BlockSpec((1, LANE, tm), lambda i: (i // tpb, 0, i % tpb)),
            pl.BlockSpec((1, 1, d), bmap3),
            pl.BlockSpec((1, 1, d), bmap3),
            _resident((1, d)),
            _resident(w1.shape),
            _resident(wt.shape),
            _resident(q_norm.shape),
            _resident(wuq.shape),
            _resident(kv_norm.shape),
            _resident(wukt.shape),
            _resident(wuv.shape),
        ],
        out_specs=(
            row_blk(MLA_HEADS, 2 * LANE),
            col_blk(MLA_HEADS),
            pl.BlockSpec((1, LANE, tm), lambda i: (i // tpb, 0, i % tpb)),
            row_blk(MLA_HEADS, LANE),
            row_blk(DIFF_HEADS, LANE),
            pl.BlockSpec((1, DIFF_HEADS, tm // LANE, LANE, LANE), lambda i: (i // tpb, 0, i % tpb, 0, 0)),
            row_blk(DIFF_HEADS, LANE),
        ),
        out_shape=out_shapes,
        compiler_params=_params("arbitrary"),
        name="in_proj",
    )(x2, *rope, sh_a, sc_a, pre_w, w1, wt, q_norm, wuq, kv_norm, wukt, wuv)


def _lane_tile_reduce(x, op):
    out = x[:, 0:LANE]
    for t in range(1, x.shape[1] // LANE):
        out = op(out, x[:, t * LANE:(t + 1) * LANE])
    return out


def _pv_normalised(chunks, v_ref):
    ones = jnp.ones((ATTN_CK, LANE), BF16)
    acc = None
    for c, p in enumerate(chunks):
        v_aug = jnp.concatenate([v_ref[c * ATTN_CK:(c + 1) * ATTN_CK, :], ones], axis=1)
        pv = jnp.dot(p.astype(BF16), v_aug, preferred_element_type=F32)
        acc = pv if acc is None else acc + pv
    return acc[:, 0:LANE] * (1.0 / acc[:, LANE:2 * LANE])


def _mla_kernel(q_ref, knt_ref, krt_ref, v_ref, o_ref, s_ref):
    seq = s_ref.shape[2]
    n_chunks = seq // ATTN_CK
    ones = jnp.ones((ATTN_CK, LANE), BF16)

    def scores_chunk(hd, c, m_part):
        sl = slice(c * ATTN_CK, (c + 1) * ATTN_CK)
        kt = jnp.concatenate([knt_ref[0, hd, :, sl], krt_ref[0, :, sl]], axis=0)
        s = jnp.dot(q_ref[0, hd], kt, preferred_element_type=F32)
        s_ref[hd % 2, :, sl] = s
        mp = _lane_tile_reduce(s, jnp.maximum)
        return mp if m_part is None else jnp.maximum(m_part, mp)

    def pv_chunk(hd, c, m_col, acc):
        sl = slice(c * ATTN_CK, (c + 1) * ATTN_CK)
        p = jnp.exp2(s_ref[hd % 2, :, sl] - m_col).astype(BF16)
        pv = jnp.dot(p, jnp.concatenate([v_ref[0, hd, sl, :], ones], axis=1), preferred_element_type=F32)
        return pv if acc is None else acc + pv

    m_part = None
    for c in range(n_chunks):
        m_part = scores_chunk(0, c, m_part)
    for hd in range(MLA_HEADS):
        m_col = jnp.max(m_part, axis=-1, keepdims=True)
        m_part = None
        acc = None
        for c in range(n_chunks):
            if hd + 1 < MLA_HEADS:
                m_part = scores_chunk(hd + 1, c, m_part)
            acc = pv_chunk(hd, c, m_col, acc)
        o = acc[:, 0:LANE] * (1.0 / acc[:, LANE:2 * LANE])
        o_ref[0, :, hd * MLA_V:(hd + 1) * MLA_V] = o.astype(o_ref.dtype)


def _mla_attn(q, knt, krt, v):
    bsz, nh, seq, _ = q.shape
    tq = ATTN_TQ
    return pl.pallas_call(
        _mla_kernel,
        grid=(bsz, seq // tq),
        in_specs=[
            pl.BlockSpec((1, nh, tq, 2 * LANE), lambda b, i: (b, 0, i, 0)),
            pl.BlockSpec((1, nh, LANE, seq), lambda b, i: (b, 0, 0, 0)),
            pl.BlockSpec((1, LANE, seq), lambda b, i: (b, 0, 0)),
            pl.BlockSpec((1, nh, seq, LANE), lambda b, i: (b, 0, 0, 0)),
        ],
        out_specs=pl.BlockSpec((1, tq, nh * MLA_V), lambda b, i: (b, i, 0)),
        out_shape=jax.ShapeDtypeStruct((bsz, seq, nh * MLA_V), BF16),
        scratch_shapes=[pltpu.VMEM((2, tq, seq), F32)],
        compiler_params=_params("arbitrary", "arbitrary"),
        name="mla_attn",
    )(q, knt, krt, v)


def _t5_bucket(rel):
    n = jnp.abs(rel)
    large = jnp.full(rel.shape, 8, jnp.int32)
    for t in T5_THRESHOLDS:
        large = large + jnp.where(n >= t, 1, 0)
    return jnp.where(rel > 0, REL_BUCKETS // 2, 0) + jnp.where(n < 8, n, large)


def _diff_split_q(q):
    lane = lax.broadcasted_iota(jnp.int32, q.shape, 1)
    zero = jnp.zeros_like(q)
    return jnp.where(lane < DIFF_QK, q, zero), jnp.where(lane >= DIFF_QK, q, zero)


class _TiledRows:
    def __init__(self, ref):
        self.ref = ref

    def __getitem__(self, idx):
        rows, _ = idx
        tiles = range(rows.start // LANE, rows.stop // LANE)
        return jnp.concatenate([self.ref[t] for t in tiles], axis=0)


def _softmax_pv_tiles(s_ref, m_col, v_ref):
    per = ATTN_CK // LANE
    chunks = []
    for c in range(s_ref.shape[0] // per):
        sv = jnp.concatenate([s_ref[c * per + t] for t in range(per)], axis=1)
        chunks.append(jnp.exp2(sv - m_col))
    return _pv_normalised(chunks, _TiledRows(v_ref))


def _diff_finish(s1_ref, s2_ref, m1, m2, v_ref, lam, subln_row, lambda_init):
    m1c = jnp.max(m1, axis=-1, keepdims=True)
    m2c = jnp.max(m2, axis=-1, keepdims=True)
    o = _softmax_pv_tiles(s1_ref, m1c, v_ref) - _softmax_pv_tiles(s2_ref, m2c, v_ref) * lam
    return _rms(o, subln_row) * (1.0 - lambda_init)


def _bias_rows(t_neg, t_pos, ws, nw, seq):
    shape = (2 * SUBLANE, seq)
    tile = lax.broadcasted_iota(jnp.int32, shape, 1) // LANE
    cv = jnp.where(tile < nw, 0.0, jnp.where(tile + ws >= seq // LANE, t_neg, t_pos))
    hi = cv.astype(BF16).astype(F32)
    row = lax.broadcasted_iota(jnp.int32, shape, 0)
    return jnp.where(row == 0, hi, jnp.where(row == 1, cv - hi, 0.0)).astype(BF16)


def _diff_kernel(pos_sm, tab_sm, dq_ref, dkt_ref, dv_ref, pcol_ref, pch_ref, tabt_ref,
                 lq1_ref, lk1_ref, lq2_ref, lk2_ref, subln_ref, o_ref,
                 sa0_ref, sb0_ref, sa1_ref, sb1_ref, bkt_ref, oslow_ref, *, lambda_init):
    b = pl.program_id(0)
    qi = pl.program_id(1)
    nch, tq, _ = sa0_ref.shape
    nw = tq // LANE + 2
    per = ATTN_CK // LANE

    q_lo = pos_sm[b, qi * tq]
    q_hi = pos_sm[b, qi * tq + tq - 1]
    n_neg = jnp.int32(0)
    n_pos = jnp.int32(0)
    for c in range(nch):
        n_neg = n_neg + (pos_sm[b, c * LANE + LANE - 1] - q_lo <= -T5_SATURATE).astype(jnp.int32)
        n_pos = n_pos + (pos_sm[b, c * LANE] - q_hi >= T5_SATURATE).astype(jnp.int32)
    fits = (nch - n_neg - n_pos) <= nw
    ws = jnp.minimum(n_neg, nch - nw)

    pq = pcol_ref[0]
    lam = (jnp.exp(jnp.sum(lq1_ref[...] * lk1_ref[...], axis=-1, keepdims=True))
           - jnp.exp(jnp.sum(lq2_ref[...] * lk2_ref[...], axis=-1, keepdims=True)) + lambda_init)
    subln_row = subln_ref[...]
    neg_inf = jnp.full((tq, LANE), -jnp.inf, F32)
    slots = ((sa0_ref, sb0_ref), (sa1_ref, sb1_ref))
    pick2 = jnp.where(lax.broadcasted_iota(jnp.int32, (tq, LANE), 1) < 2, 1.0, 0.0).astype(BF16)
    kt_pad = jnp.zeros((LANE - 2 * SUBLANE, ATTN_CK), BF16)

    def gather_bias(tab_b, bkt):
        return jnp.take_along_axis(tab_b, bkt, axis=1, mode="promise_in_bounds")

    @pl.when(fits)
    def _():
        for j in range(nw):
            bkt_ref[j] = _t5_bucket(pch_ref[0, ws + j] - pq)
        ones = jnp.ones((ATTN_CK, LANE), BF16)

        def head_operands(hd):
            t_neg = tab_sm[REL_BUCKETS // 2 - 1, hd] * LOG2E
            t_pos = tab_sm[REL_BUCKETS - 1, hd] * LOG2E
            q1, q2 = _diff_split_q(dq_ref[0, hd])
            return (jnp.concatenate([q1, pick2], axis=1), jnp.concatenate([q2, pick2], axis=1),
                    _bias_rows(t_neg, t_pos, ws, nw, nch * LANE))

        tile_at = [jnp.where(ws + j >= nch, ws + j - nch, ws + j) for j in range(nch)]

        def scores_chunk(hd, c4, q1, q2, rows, m12):
            s1_ref, s2_ref = slots[hd % len(slots)]
            m1, m2 = m12
            sl = slice(c4 * ATTN_CK, (c4 + 1) * ATTN_CK)
            k_tiles = jnp.concatenate([dkt_ref[0, hd, tile_at[c4 * per + t]] for t in range(per)], axis=1)
            kt = jnp.concatenate([k_tiles, rows[:, sl], kt_pad], axis=0)
            sa = jnp.dot(q1, kt, preferred_element_type=F32)
            sb = jnp.dot(q2, kt, preferred_element_type=F32)
            for t in range(per):
                j = c4 * per + t
                va = sa[:, t * LANE:(t + 1) * LANE]
                vb = sb[:, t * LANE:(t + 1) * LANE]
                s1_ref[j] = va
                s2_ref[j] = vb
                if j >= nw:
                    m1 = jnp.maximum(m1, va)
                    m2 = jnp.maximum(m2, vb)
            return m1, m2

        def window_bias_and_max(hd, m12):
            s1_ref, s2_ref = slots[hd % len(slots)]
            m1, m2 = m12
            tab_b = jnp.broadcast_to(tabt_ref[hd:hd + 1, :] * LOG2E, (tq, LANE))
            for j in range(nw):
                bias = gather_bias(tab_b, bkt_ref[j])
                va = s1_ref[j] + bias
                vb = s2_ref[j] + bias
                s1_ref[j] = va
                s2_ref[j] = vb
                m1 = jnp.maximum(m1, va)
                m2 = jnp.maximum(m2, vb)
            return m1, m2

        def pv_chunk(hd, c4, m_cols, accs):
            v_rows = jnp.concatenate([dv_ref[0, hd, tile_at[c4 * per + t]] for t in range(per)], axis=0)
            v_aug = jnp.concatenate([v_rows, ones], axis=1)
            out = []
            for s_ref, m_col, acc in zip(slots[hd % len(slots)], m_cols, accs):
                sv = jnp.concatenate([s_ref[c4 * per + t] for t in range(per)], axis=1)
                pv = jnp.dot(jnp.exp2(sv - m_col).astype(BF16), v_aug, preferred_element_type=F32)
                out.append(pv if acc is None else acc + pv)
            return out

        n_parts = nch // per
        col_max = lambda m12: tuple(jnp.max(m, axis=-1, keepdims=True) for m in m12)
        ops = head_operands(0)
        m12 = (neg_inf, neg_inf)
        for c4 in range(n_parts):
            m12 = scores_chunk(0, c4, *ops, m12)
        m_cols = col_max(window_bias_and_max(0, m12))
        for hd in range(DIFF_HEADS):
            nxt = hd + 1
            if nxt < DIFF_HEADS:
                ops = head_operands(nxt)
            accs = [None, None]
            m12 = (neg_inf, neg_inf)
            for c4 in range(n_parts):
                if nxt < DIFF_HEADS:
                    m12 = scores_chunk(nxt, c4, *ops, m12)
                accs = pv_chunk(hd, c4, m_cols, accs)
            a1, a2 = accs
            o = a1[:, 0:LANE] * (1.0 / a1[:, LANE:2 * LANE]) - (a2[:, 0:LANE] * (1.0 / a2[:, LANE:2 * LANE])) * lam
            o_ref[0, :, hd * DIFF_V:(hd + 1) * DIFF_V] = (
                _rms(o, subln_row) * (1.0 - lambda_init)).astype(o_ref.dtype)
            if nxt < DIFF_HEADS:
                m_cols = col_max(window_bias_and_max(nxt, m12))

    @pl.when(jnp.logical_not(fits))
    def _():
        for c in range(nch):
            bkt_ref[c] = _t5_bucket(pch_ref[0, c] - pq)

        def body(hd, carry):
            q1, q2 = _diff_split_q(dq_ref[0, hd])
            tab_b = jnp.broadcast_to(tabt_ref[pl.ds(hd, 1), :] * LOG2E, (tq, LANE))
            m1 = neg_inf
            m2 = neg_inf
            for c4 in range(nch // per):
                kt = jnp.concatenate([dkt_ref[0, hd, c4 * per + t] for t in range(per)], axis=1)
                sa = jnp.dot(q1, kt, preferred_element_type=F32)
                sb = jnp.dot(q2, kt, preferred_element_type=F32)
                for t in range(per):
                    c = c4 * per + t
                    bias = gather_bias(tab_b, bkt_ref[c])
                    va = sa[:, t * LANE:(t + 1) * LANE] + bias
                    vb = sb[:, t * LANE:(t + 1) * LANE] + bias
                    sa0_ref[c] = va
                    sb0_ref[c] = vb
                    m1 = jnp.maximum(m1, va)
                    m2 = jnp.maximum(m2, vb)
            oslow_ref[hd] = _diff_finish(sa0_ref, sb0_ref, m1, m2, dv_ref.at[0, hd], lam, subln_row,
                                         lambda_init)
            return carry

        lax.fori_loop(0, DIFF_HEADS, body, 0)
        for hd in range(DIFF_HEADS):
            o_ref[0, :, hd * DIFF_V:(hd + 1) * DIFF_V] = oslow_ref[hd].astype(o_ref.dtype)


def _diff_attn(positions, rel_bias, dq, dkt, dv, pos_col, pos_tiles, tab_t, lq1, lk1, lq2, lk2, subln,
               lambda_init):
    bsz, nh, seq, _ = dq.shape
    tq = DIFF_TQ
    nch = seq // LANE
    const2 = lambda b, i, *_: (0, 0)
    once = pl.Buffered(1)
    grid_spec = pltpu.PrefetchScalarGridSpec(
        num_scalar_prefetch=2,
        grid=(bsz, seq // tq),
        in_specs=[
            pl.BlockSpec((1, nh, tq, LANE), lambda b, i, *_: (b, 0, i, 0)),
            pl.BlockSpec((1, nh, nch, LANE, LANE), lambda b, i, *_: (b, 0, 0, 0, 0), pipeline_mode=once),
            pl.BlockSpec((1, nh, nch, LANE, LANE), lambda b, i, *_: (b, 0, 0, 0, 0), pipeline_mode=once),
            pl.BlockSpec((1, tq, 1), lambda b, i, *_: (b, i, 0)),
            pl.BlockSpec((1, nch, 1, LANE), lambda b, i, *_: (b, 0, 0, 0)),
            pl.BlockSpec(tab_t.shape, const2),
            pl.BlockSpec((1, DIFF_QK), const2),
            pl.BlockSpec((1, DIFF_QK), const2),
            pl.BlockSpec((1, DIFF_QK), const2),
            pl.BlockSpec((1, DIFF_QK), const2),
            pl.BlockSpec((1, DIFF_V), const2),
        ],
        out_specs=pl.BlockSpec((1, tq, nh * DIFF_V), lambda b, i, *_: (b, i, 0)),
        scratch_shapes=[
            *([pltpu.VMEM((nch, tq, LANE), F32)] * 4),
            pltpu.VMEM((nch, tq, LANE), jnp.int32),
            pltpu.VMEM((nh, tq, LANE), F32),
        ],
    )
    return pl.pallas_call(
        partial(_diff_kernel, lambda_init=lambda_init),
        grid_spec=grid_spec,
        out_shape=jax.ShapeDtypeStruct((bsz, seq, nh * DIFF_V), BF16),
        compiler_params=_params("arbitrary", "arbitrary"),
        name="diff_attn",
    )(positions, rel_bias, dq, dkt, dv.reshape(bsz, nh, nch, LANE, LANE), pos_col, pos_tiles, tab_t,
      lq1, lk1, lq2, lk2, subln)


def _out_proj_kernel(om_ref, od_ref, wa_ref, wb_ref, x_ref, g_ref, post_ref, sh_ref, sc_ref, pre_ref,
                     x1_ref, h2_ref, il_ref):
    _, nph, rows, d = x_ref.shape
    tm = nph * rows
    merge = lambda ref: ref[0].reshape(tm, ref.shape[3])
    y = (jnp.dot(merge(om_ref), wa_ref[...], preferred_element_type=F32)
         + jnp.dot(merge(od_ref), wb_ref[...], preferred_element_type=F32))
    x1 = merge(x_ref) + g_ref[0] * _rms(y, post_ref[...])
    x1_ref[0] = x1.reshape(nph, rows, d)
    a_row = pre_ref[...] * (1.0 + sc_ref[0])
    ms = jnp.mean(x1 * x1, axis=-1, keepdims=True)
    h2 = (x1 * lax.rsqrt(ms + NORM_EPS)) * a_row + sh_ref[0]
    for j in range(il_ref.shape[0]):
        for s in range(nph):
            il_ref[j, pl.ds(s, rows, stride=nph), :] = h2[s * rows:(s + 1) * rows, j * LANE:(j + 1) * LANE]
    for j in range(il_ref.shape[0]):
        h2_ref[:, j * LANE:(j + 1) * LANE] = il_ref[j].astype(h2_ref.dtype)


def _out_proj(o_mla, o_diff, wo_a, wo_b, x2, g_a, post_w, sh_f, sc_f, pre_w, seq):
    m, d = x2.shape
    bsz = m // seq
    tm = OUTPROJ_TM
    rows = tm // ROW_IL
    tpb = seq // tm
    phases = lambda a: a.reshape(bsz, ROW_IL, seq // ROW_IL, a.shape[-1])
    ph_block = lambda w: pl.BlockSpec((1, ROW_IL, rows, w), lambda i: (i // tpb, 0, i % tpb, 0))
    bmap3 = lambda i: (i // tpb, 0, 0)
    x1, h2 = pl.pallas_call(
        _out_proj_kernel,
        grid=(m // tm,),
        in_specs=[
            ph_block(o_mla.shape[1]),
            ph_block(o_diff.shape[1]),
            _resident(wo_a.shape),
            _resident(wo_b.shape),
            ph_block(d),
            pl.BlockSpec((1, 1, d), bmap3),
            _resident((1, d)),
            pl.BlockSpec((1, 1, d), bmap3),
            pl.BlockSpec((1, 1, d), bmap3),
            _resident((1, d)),
        ],
        out_specs=(ph_block(d), pl.BlockSpec((tm, d), lambda i: (i, 0))),
        out_shape=(jax.ShapeDtypeStruct((bsz, ROW_IL, seq // ROW_IL, d), F32),
                   jax.ShapeDtypeStruct((m, d), BF16)),
        scratch_shapes=[pltpu.VMEM((d // LANE, tm, LANE), F32)],
        compiler_params=_params("arbitrary"),
        name="out_proj",
    )(phases(o_mla), phases(o_diff), wo_a, wo_b, phases(x2), g_a, post_w, sh_f, sc_f, pre_w)
    return x1.reshape(m, d), h2


def _gelu_tanh(g):
    a = -2.0 * math.sqrt(2.0 / math.pi) * LOG2E
    return g * (1.0 / (1.0 + jnp.exp2(g * (a + (a * 0.044715) * (g * g)))))


def _neighbour_rows(u, step):
    n = u.shape[0]
    row = lax.broadcasted_iota(jnp.int32, (ROW_IL, u.shape[1]), 0)
    if step == -1:
        wrap = jnp.where(row == 0, 0.0, pltpu.roll(u[n - ROW_IL:], 1, 0))
        return jnp.concatenate([wrap, u[:n - ROW_IL]], axis=0)
    wrap = jnp.where(row == ROW_IL - 1, 0.0, pltpu.roll(u[0:ROW_IL], ROW_IL - 1, 0))
    return jnp.concatenate([u[ROW_IL:], wrap], axis=0)


def _ffn_up_kernel(h_ref, wg_ref, wv_ref, cwg_ref, cwv_ref, cbg_ref, cbv_ref, o_ref):
    h = h_ref[...]

    def conv(u, cw_ref, cb_ref):
        return (cb_ref[...] + u * cw_ref[1:2, :]
                + _neighbour_rows(u, -1) * cw_ref[0:1, :] + _neighbour_rows(u, 1) * cw_ref[2:3, :])

    gate = conv(jnp.dot(h, wg_ref[...].astype(BF16), preferred_element_type=F32), cwg_ref, cbg_ref)
    val = conv(jnp.dot(h, wv_ref[...].astype(BF16), preferred_element_type=F32), cwv_ref, cbv_ref)
    o_ref[...] = (_gelu_tanh(gate) * val).astype(o_ref.dtype)


def _ffn_up(h2, w_up, conv_w, conv_b, seq):
    m, d = h2.shape
    f = w_up.shape[1] // 2
    tn = UP_TN
    nj = f // tn
    return pl.pallas_call(
        _ffn_up_kernel,
        grid=(nj, m // seq),
        in_specs=[
            pl.BlockSpec((seq, d), lambda j, i: (i, 0)),
            pl.BlockSpec((d, tn), lambda j, i: (0, j)),
            pl.BlockSpec((d, tn), lambda j, i: (0, j + nj)),
            pl.BlockSpec((3, tn), lambda j, i: (0, j)),
            pl.BlockSpec((3, tn), lambda j, i: (0, j + nj)),
            pl.BlockSpec((1, tn), lambda j, i: (0, j)),
            pl.BlockSpec((1, tn), lambda j, i: (0, j + nj)),
        ],
        out_specs=pl.BlockSpec((seq, tn), lambda j, i: (i, j)),
        out_shape=jax.ShapeDtypeStruct((m, f), BF16),
        compiler_params=_params("arbitrary", "arbitrary"),
        name="ffn_up",
    )(h2, w_up, w_up, conv_w, conv_w, conv_b, conv_b)


def _ffn_down_kernel(a_ref, w_ref, x1_ref, g_ref, post_ref, o_ref, y_ref):
    tm = a_ref.shape[0]
    y = jnp.dot(a_ref[...], w_ref[...], preferred_element_type=F32)
    yn = _rms(y, post_ref[...])
    g = g_ref[0]
    for j in range(y_ref.shape[0]):
        y_ref[j] = yn[:, j * LANE:(j + 1) * LANE]
    for s in range(ROW_IL):
        for j in range(y_ref.shape[0]):
            cs = slice(j * LANE, (j + 1) * LANE)
            rows = y_ref[j, pl.ds(s, tm // ROW_IL, stride=ROW_IL), :]
            o_ref[0, s, :, cs] = x1_ref[0, s, :, cs] + g[:, cs] * rows


def _ffn_down(act, w_down, x1, g_f, post_w, seq):
    m, f = act.shape
    d = w_down.shape[1]
    tm = DOWN_TM
    tpb = seq // tm
    bsz = m // seq
    nat_block = pl.BlockSpec((1, ROW_IL, tm // ROW_IL, d), lambda i: (i // tpb, 0, i % tpb, 0))
    out = pl.pallas_call(
        _ffn_down_kernel,
        grid=(m // tm,),
        in_specs=[
            pl.BlockSpec((tm, f), lambda i: (i, 0)),
            _resident(w_down.shape),
            nat_block,
            pl.BlockSpec((1, 1, d), lambda i: (i // tpb, 0, 0)),
            _resident((1, d)),
        ],
        out_specs=nat_block,
        out_shape=jax.ShapeDtypeStruct((bsz, ROW_IL, seq // ROW_IL, d), F32),
        scratch_shapes=[pltpu.VMEM((d // LANE, tm, LANE), F32)],
        compiler_params=_params("arbitrary"),
        name="ffn_down",
    )(act, w_down, x1.reshape(bsz, ROW_IL, seq // ROW_IL, d), g_f, post_w)
    return out.reshape(m, d)


def _rope_pair_tiles(w_rope):
    half = MLA_ROPE // 2
    t1, t2 = w_rope[:, :half], w_rope[:, half:]
    z = jnp.zeros((w_rope.shape[0], LANE - MLA_ROPE), w_rope.dtype)
    return jnp.concatenate([t1, t2, z], axis=1), jnp.concatenate([t2, t1, z], axis=1)


def _layout_w_in(w_in):
    o_kv = MLA_Q_RANK
    o_kr = o_kv + MLA_KV_RANK
    o_dq = o_kr + MLA_ROPE
    o_dk = o_dq + DIFF_HEADS * 2 * DIFF_QK
    o_dv = o_dk + DIFF_HEADS * 2 * DIFF_QK
    w1 = jnp.concatenate([w_in[:, :o_kr], w_in[:, o_dq:o_dk], w_in[:, o_dv:]], axis=1).astype(BF16)
    ra, rb = _rope_pair_tiles(w_in[:, o_kr:o_dq])
    wt = jnp.concatenate([ra, rb, w_in[:, o_dk:o_dv]], axis=1).T.astype(BF16)
    return w1, wt


def _layout_w_uq(w_uq):
    k = w_uq.shape[0]
    w3 = w_uq.reshape(k, MLA_HEADS, MLA_NOPE + MLA_ROPE)
    blocks = []
    for hd in range(MLA_HEADS):
        ra, rb = _rope_pair_tiles(w3[:, hd, MLA_NOPE:])
        blocks += [w3[:, hd, :MLA_NOPE], ra, rb]
    return jnp.concatenate(blocks, axis=1).astype(BF16)


def _layout_w_ukv(w_ukv):
    k = w_ukv.shape[0]
    w3 = w_ukv.reshape(k, MLA_HEADS, MLA_NOPE + MLA_V)
    wukt = w3[:, :, :MLA_NOPE].reshape(k, MLA_HEADS * MLA_NOPE).T.astype(BF16)
    wuv = w3[:, :, MLA_NOPE:].reshape(k, MLA_HEADS * MLA_V).astype(BF16)
    return wukt, wuv


def _rope_kernel(pos_ref, inv_ref, cr_ref, sr_ref, ct_ref, st_ref):
    ang = inv_ref[...] * pos_ref[0].astype(F32)
    c = jnp.cos(ang)
    s = jnp.sin(ang)
    zeros = jnp.zeros((LANE - MLA_ROPE, ang.shape[1]), F32)
    ct = jnp.concatenate([c, c, zeros], axis=0)
    st = jnp.concatenate([-s, s, zeros], axis=0)
    ct_ref[0] = ct
    st_ref[0] = st
    cr_ref[0] = ct.T
    sr_ref[0] = st.T


def _rope_tables(pos_row):
    bsz, _, seq = pos_row.shape
    inv = 1.0 / (ROPE_THETA ** (np.arange(0, MLA_ROPE, 2, dtype=np.float32) / MLA_ROPE))
    inv_col = jnp.asarray(inv.astype(np.float32).reshape(MLA_ROPE // 2, 1))
    row_tab = jax.ShapeDtypeStruct((bsz, seq, LANE), F32)
    col_tab = jax.ShapeDtypeStruct((bsz, LANE, seq), F32)
    return pl.pallas_call(
        _rope_kernel,
        grid=(bsz,),
        in_specs=[pl.BlockSpec((1, 1, seq), lambda b: (b, 0, 0)),
                  pl.BlockSpec(inv_col.shape, lambda b: (0, 0))],
        out_specs=(pl.BlockSpec((1, seq, LANE), lambda b: (b, 0, 0)),
                   pl.BlockSpec((1, seq, LANE), lambda b: (b, 0, 0)),
                   pl.BlockSpec((1, LANE, seq), lambda b: (b, 0, 0)),
                   pl.BlockSpec((1, LANE, seq), lambda b: (b, 0, 0))),
        out_shape=(row_tab, row_tab, col_tab, col_tab),
        compiler_params=_params("arbitrary"),
        name="rope_tables",
    )(pos_row, inv_col)


def kernel(x, c, positions, rel_bias, ada_w, ada_b, attn_pre_norm, attn_post_norm, w_in, q_norm, w_uq,
           kv_norm, w_ukv, lambda_q1, lambda_k1, lambda_q2, lambda_k2, diff_subln, w_o, ffn_pre_norm,
           ffn_post_norm, w_up, conv_w, conv_b, w_down):
    bsz, seq, d = x.shape
    depth = ada_w.shape[0]
    m = bsz * seq
    positions = positions.astype(jnp.int32)
    pos_col = positions.reshape(bsz, seq, 1)
    pos_row = positions.reshape(bsz, 1, seq)
    pos_tiles = positions.reshape(bsz, seq // LANE, 1, LANE)
    rope = _rope_tables(pos_row)
    tab_t = jnp.zeros((DIFF_HEADS, LANE), F32).at[:, :REL_BUCKETS].set(rel_bias.astype(F32).T)
    row = lambda v: v.reshape(1, -1).astype(F32)

    x2 = x.reshape(m, d)
    for l in range(depth):
        lambda_init = 0.8 - 0.6 * math.exp(-0.3 * l)
        mod = _ada_mod(c, ada_w[l], ada_b[l])
        sh_a, sc_a, g_a, sh_f, sc_f, g_f = [t.reshape(bsz, 1, d) for t in jnp.split(mod, N_MOD, axis=-1)]

        w1, wt = _layout_w_in(w_in[l])
        wuq = _layout_w_uq(w_uq[l])
        wukt, wuv = _layout_w_ukv(w_ukv[l])
        q, knt, krt, v, dq, dkt, dv = _in_proj(
            x2, rope, sh_a, sc_a, row(attn_pre_norm[l]), w1, wt, row(q_norm[l]), wuq,
            row(kv_norm[l]), wukt, wuv, bsz, seq)

        o_mla = _mla_attn(q, knt, krt, v)
        o_diff = _diff_attn(positions, rel_bias.astype(F32), dq, dkt, dv, pos_col, pos_tiles, tab_t,
                            row(lambda_q1[l]), row(lambda_k1[l]), row(lambda_q2[l]), row(lambda_k2[l]),
                            row(diff_subln[l]), lambda_init)

        wo = w_o[l].astype(BF16)
        n_mla = MLA_HEADS * MLA_V
        x1, h2 = _out_proj(o_mla.reshape(m, n_mla), o_diff.reshape(m, DIFF_HEADS * DIFF_V),
                           wo[:n_mla], wo[n_mla:], x2, g_a, row(attn_post_norm[l]), sh_f, sc_f,
                           row(ffn_pre_norm[l]), seq)

        act = _ffn_up(h2, w_up[l], conv_w[l].astype(F32), row(conv_b[l]), seq)
        x2 = _ffn_down(act, w_down[l].astype(BF16), x1, g_f, row(ffn_post_norm[l]), seq)
    return x2.reshape(bsz, seq, d)
```

```python
import math
from functools import partial

import jax
import jax.numpy as jnp
import numpy as np
from jax import lax
from jax.experimental import pallas as pl
from jax.experimental.pallas import tpu as pltpu

MLA_HEADS = 8
MLA_Q_RANK = 512
MLA_KV_RANK = 256
MLA_NOPE = 128
MLA_ROPE = 64
MLA_V = 128
MLA_SCALE = (MLA_NOPE + MLA_ROPE) ** -0.5
DIFF_HEADS = 8
DIFF_QK = 64
DIFF_V = 128
DIFF_SCALE = DIFF_QK ** -0.5
REL_BUCKETS = 32
ROPE_THETA = 10000.0
NORM_EPS = 1e-6
N_MOD = 6
LOG2E = math.log2(math.e)
T5_SATURATE = 128
F32_MANT_BITS = 23
F32_EXP_MASK = 0xFF
F32_EXP_BIAS = 127
SQRT2_MANT = 0x3504F4

LANE = 128
SUBLANE = 8
VMEM_LIMIT_BYTES = 56 * 1024 * 1024

ADA_TN = 1536
INPROJ_TM = 512
ATTN_TQ = 512
DIFF_TQ = 256
ATTN_CK = 512
OUTPROJ_TM = 512
ROW_IL = 8
UP_TN = 512
DOWN_TM = 256

F32 = jnp.float32
BF16 = jnp.bfloat16


def _params(*semantics):
    return pltpu.CompilerParams(dimension_semantics=semantics, vmem_limit_bytes=VMEM_LIMIT_BYTES)


def _resident(shape):
    nd = len(shape)
    return pl.BlockSpec(shape, lambda *_: (0,) * nd, pipeline_mode=pl.Buffered(1))


def _rms(x, w_row):
    ms = jnp.mean(x * x, axis=-1, keepdims=True)
    return (x * lax.rsqrt(ms + NORM_EPS)) * w_row


def _nt_dot(a, b):
    return lax.dot_general(a, b, (((1,), (1,)), ((), ())), preferred_element_type=F32)


def _ada_kernel(c_ref, w_ref, b_ref, o_ref):
    c = c_ref[...]
    c_act = (c * jax.nn.sigmoid(c)).astype(BF16)
    o_ref[...] = jnp.dot(c_act, w_ref[...].astype(BF16), preferred_element_type=F32) + b_ref[...]


def _ada_mod(c, ada_w, ada_b):
    bsz, d = c.shape
    n = ada_w.shape[1]
    return pl.pallas_call(
        _ada_kernel,
        grid=(n // ADA_TN,),
        in_specs=[
            pl.BlockSpec((bsz, d), lambda j: (0, 0)),
            pl.BlockSpec((d, ADA_TN), lambda j: (0, j)),
            pl.BlockSpec((1, ADA_TN), lambda j: (0, j)),
        ],
        out_specs=pl.BlockSpec((bsz, ADA_TN), lambda j: (0, j)),
        out_shape=jax.ShapeDtypeStruct((bsz, n), F32),
        compiler_params=_params("arbitrary"),
        name="ada_mod",
    )(c, ada_w, ada_b.reshape(1, n))


N_W1 = MLA_Q_RANK + MLA_KV_RANK + 2 * DIFF_HEADS * 2 * DIFF_QK
N_WT = 2 * LANE + DIFF_HEADS * 2 * DIFF_QK
Q_HEAD_COLS = 3 * LANE


def _in_proj_kernel(x_ref, cr_ref, sr_ref, ct_ref, st_ref, sh_ref, sc_ref, pre_ref, w1_ref, wt_ref,
                    qn_ref, wuq_ref, kvn_ref, wukt_ref, wuv_ref,
                    q_out, knt_out, krt_out, v_out, dq_out, dkt_out, dv_out):
    x = x_ref[...]
    a_row = pre_ref[...] * (1.0 + sc_ref[0])
    ms = jnp.mean(x * x, axis=-1, keepdims=True)
    h = ((x * lax.rsqrt(ms + NORM_EPS)) * a_row + sh_ref[0]).astype(BF16)

    p1 = jnp.dot(h, w1_ref[...], preferred_element_type=F32)
    p2t = _nt_dot(wt_ref[...], h)

    cos_r, sin_r = cr_ref[0], sr_ref[0]
    cos_t, sin_t = ct_ref[0], st_ref[0]

    q_lat = p1[:, 0:MLA_Q_RANK]
    rq = _rms(q_lat, qn_ref[...] * (MLA_SCALE * LOG2E)).astype(BF16)
    qf = jnp.dot(rq, wuq_ref[...], preferred_element_type=F32)
    for hd in range(MLA_HEADS):
        c0 = hd * Q_HEAD_COLS
        q_out[0, hd, :, 0:LANE] = qf[:, c0:c0 + LANE].astype(BF16)
        rope = qf[:, c0 + LANE:c0 + 2 * LANE] * cos_r + qf[:, c0 + 2 * LANE:c0 + 3 * LANE] * sin_r
        q_out[0, hd, :, LANE:2 * LANE] = rope.astype(BF16)

    kv_lat = p1[:, MLA_Q_RANK:MLA_Q_RANK + MLA_KV_RANK]
    rkv = _rms(kv_lat, kvn_ref[...]).astype(BF16)
    knt = _nt_dot(wukt_ref[...], rkv)
    vv = jnp.dot(rkv, wuv_ref[...], preferred_element_type=F32)
    for hd in range(MLA_HEADS):
        knt_out[0, hd] = knt[hd * LANE:(hd + 1) * LANE, :].astype(BF16)
        v_out[0, hd] = vv[:, hd * LANE:(hd + 1) * LANE].astype(BF16)
    krt_out[0] = (p2t[0:LANE, :] * cos_t + p2t[LANE:2 * LANE, :] * sin_t).astype(BF16)

    o_dq = MLA_Q_RANK + MLA_KV_RANK
    o_dv = o_dq + DIFF_HEADS * 2 * DIFF_QK
    for hd in range(DIFF_HEADS):
        dq_out[0, hd] = (p1[:, o_dq + hd * LANE:o_dq + (hd + 1) * LANE] * (DIFF_SCALE * LOG2E)).astype(BF16)
        dv_out[0, hd] = p1[:, o_dv + hd * LANE:o_dv + (hd + 1) * LANE].astype(BF16)
        dkt = p2t[2 * LANE + hd * LANE:2 * LANE + (hd + 1) * LANE, :].astype(BF16)
        for t in range(dkt_out.shape[2]):
            dkt_out[0, hd, t] = dkt[:, t * LANE:(t + 1) * LANE]


def _in_proj(x2, rope, sh_a, sc_a, pre_w, w1, wt, q_norm, wuq, kv_norm, wukt, wuv, bsz, seq):
    tm = INPROJ_TM
    tpb = seq // tm
    d = x2.shape[1]
    bmap3 = lambda i: (i // tpb, 0, 0)
    out_shapes = (
        jax.ShapeDtypeStruct((bsz, MLA_HEADS, seq, 2 * LANE), BF16),
        jax.ShapeDtypeStruct((bsz, MLA_HEADS, LANE, seq), BF16),
        jax.ShapeDtypeStruct((bsz, LANE, seq), BF16),
        jax.ShapeDtypeStruct((bsz, MLA_HEADS, seq, LANE), BF16),
        jax.ShapeDtypeStruct((bsz, DIFF_HEADS, seq, LANE), BF16),
        jax.ShapeDtypeStruct((bsz, DIFF_HEADS, seq // LANE, LANE, LANE), BF16),
        jax.ShapeDtypeStruct((bsz, DIFF_HEADS, seq, LANE), BF16),
    )
    row_blk = lambda nh, w: pl.BlockSpec((1, nh, tm, w), lambda i: (i // tpb, 0, i % tpb, 0))
    col_blk = lambda nh: pl.BlockSpec((1, nh, LANE, tm), lambda i: (i // tpb, 0, 0, i % tpb))
    return pl.pallas_call(
        _in_proj_kernel,
        grid=(bsz * tpb,),
        in_specs=[
            pl.BlockSpec((tm, d), lambda i: (i, 0)),
            pl.BlockSpec((1, tm, LANE), lambda i: (i // tpb, i % tpb, 0)),
            pl.BlockSpec((1, tm, LANE), lambda i: (i // tpb, i % tpb, 0)),
            pl.BlockSpec((1, LANE, tm), lambda i: (i // tpb, 0, i % tpb)),
            pl.BlockSpec((1, LANE, tm), lambda i: (i // tpb, 0, i % tpb)),
            pl.BlockSpec((1, 1, d), bmap3),
            pl.BlockSpec((1, 1, d), bmap3),
            _resident((1, d)),
            _resident(w1.shape),
            _resident(wt.shape),
            _resident(q_norm.shape),
            _resident(wuq.shape),
            _resident(kv_norm.shape),
            _resident(wukt.shape),
            _resident(wuv.shape),
        ],
        out_specs=(
            row_blk(MLA_HEADS, 2 * LANE),
            col_blk(MLA_HEADS),
            pl.BlockSpec((1, LANE, tm), lambda i: (i // tpb, 0, i % tpb)),
            row_blk(MLA_HEADS, LANE),
            row_blk(DIFF_HEADS, LANE),
            pl.BlockSpec((1, DIFF_HEADS, tm // LANE, LANE, LANE), lambda i: (i // tpb, 0, i % tpb, 0, 0)),
            row_blk(DIFF_HEADS, LANE),
        ),
        out_shape=out_shapes,
        compiler_params=_params("arbitrary"),
        name="in_proj",
    )(x2, *rope, sh_a, sc_a, pre_w, w1, wt, q_norm, wuq, kv_norm, wukt, wuv)


def _lane_tile_reduce(x, op):
    out = x[:, 0:LANE]
    for t in range(1, x.shape[1] // LANE):
        out = op(out, x[:, t * LANE:(t + 1) * LANE])
    return out


def _pv_normalised(chunks, v_ref):
    ones = jnp.ones((ATTN_CK, LANE), BF16)
    acc = None
    for c, p in enumerate(chunks):
        v_aug = jnp.concatenate([v_ref[c * ATTN_CK:(c + 1) * ATTN_CK, :], ones], axis=1)
        pv = jnp.dot(p.astype(BF16), v_aug, preferred_element_type=F32)
        acc = pv if acc is None else acc + pv
    return acc[:, 0:LANE] * (1.0 / acc[:, LANE:2 * LANE])


def _mla_kernel(q_ref, knt_ref, krt_ref, v_ref, o_ref, s_ref):
    seq = s_ref.shape[2]
    n_chunks = seq // ATTN_CK
    ones = jnp.ones((ATTN_CK, LANE), BF16)

    def scores_chunk(hd, c, m_part):
        sl = slice(c * ATTN_CK, (c + 1) * ATTN_CK)
        kt = jnp.concatenate([knt_ref[0, hd, :, sl], krt_ref[0, :, sl]], axis=0)
        s = jnp.dot(q_ref[0, hd], kt, preferred_element_type=F32)
        s_ref[hd % 2, :, sl] = s
        mp = _lane_tile_reduce(s, jnp.maximum)
        return mp if m_part is None else jnp.maximum(m_part, mp)

    def pv_chunk(hd, c, m_col, acc):
        sl = slice(c * ATTN_CK, (c + 1) * ATTN_CK)
        p = jnp.exp2(s_ref[hd % 2, :, sl] - m_col).astype(BF16)
        pv = jnp.dot(p, jnp.concatenate([v_ref[0, hd, sl, :], ones], axis=1), preferred_element_type=F32)
        return pv if acc is None else acc + pv

    m_part = None
    for c in range(n_chunks):
        m_part = scores_chunk(0, c, m_part)
    for hd in range(MLA_HEADS):
        m_col = jnp.max(m_part, axis=-1, keepdims=True)
        m_part = None
        acc = None
        for c in range(n_chunks):
            if hd + 1 < MLA_HEADS:
                m_part = scores_chunk(hd + 1, c, m_part)
            acc = pv_chunk(hd, c, m_col, acc)
        o = acc[:, 0:LANE] * (1.0 / acc[:, LANE:2 * LANE])
        o_ref[0, :, hd * MLA_V:(hd + 1) * MLA_V] = o.astype(o_ref.dtype)


def _mla_attn(q, knt, krt, v):
    bsz, nh, seq, _ = q.shape
    tq = ATTN_TQ
    return pl.pallas_call(
        _mla_kernel,
        grid=(bsz, seq // tq),
        in_specs=[
            pl.BlockSpec((1, nh, tq, 2 * LANE), lambda b, i: (b, 0, i, 0)),
            pl.BlockSpec((1, nh, LANE, seq), lambda b, i: (b, 0, 0, 0)),
            pl.BlockSpec((1, LANE, seq), lambda b, i: (b, 0, 0)),
            pl.BlockSpec((1, nh, seq, LANE), lambda b, i: (b, 0, 0, 0)),
        ],
        out_specs=pl.BlockSpec((1, tq, nh * MLA_V), lambda b, i: (b, i, 0)),
        out_shape=jax.ShapeDtypeStruct((bsz, seq, nh * MLA_V), BF16),
        scratch_shapes=[pltpu.VMEM((2, tq, seq), F32)],
        compiler_params=_params("arbitrary", "arbitrary"),
        name="mla_attn",
    )(q, knt, krt, v)


def _t5_bucket(rel):
    n = jnp.abs(rel)
    bits = lax.bitcast_convert_type(n.astype(F32), jnp.int32)
    twice_exp = jnp.bitwise_and(lax.shift_right_logical(bits, F32_MANT_BITS - 1), 2 * F32_EXP_MASK)
    upper_half = jnp.where(jnp.bitwise_and(bits, (1 << F32_MANT_BITS) - 1) >= SQRT2_MANT, 1, 0)
    large = jnp.minimum(twice_exp - (2 * F32_EXP_BIAS - 2) + upper_half, REL_BUCKETS // 2 - 1)
    return jnp.where(rel > 0, REL_BUCKETS // 2, 0) + jnp.where(n < REL_BUCKETS // 4, n, large)


def _diff_split_q(q):
    lane = lax.broadcasted_iota(jnp.int32, q.shape, 1)
    zero = jnp.zeros_like(q)
    return jnp.where(lane < DIFF_QK, q, zero), jnp.where(lane >= DIFF_QK, q, zero)


class _TiledRows:
    def __init__(self, ref):
        self.ref = ref

    def __getitem__(self, idx):
        rows, _ = idx
        tiles = range(rows.start // LANE, rows.stop // LANE)
        return jnp.concatenate([self.ref[t] for t in tiles], axis=0)


def _softmax_pv_tiles(s_ref, m_col, v_ref):
    per = ATTN_CK // LANE
    chunks = []
    for c in range(s_ref.shape[0] // per):
        sv = jnp.concatenate([s_ref[c * per + t] for t in range(per)], axis=1)
        chunks.append(jnp.exp2(sv - m_col))
    return _pv_normalised(chunks, _TiledRows(v_ref))


def _diff_finish(s1_ref, s2_ref, m1, m2, v_ref, lam, subln_row, lambda_init):
    m1c = jnp.max(m1, axis=-1, keepdims=True)
    m2c = jnp.max(m2, axis=-1, keepdims=True)
    o = _softmax_pv_tiles(s1_ref, m1c, v_ref) - _softmax_pv_tiles(s2_ref, m2c, v_ref) * lam
    return _rms(o, subln_row) * (1.0 - lambda_init)


def _bias_rows(t_neg, t_pos, ws, nw, seq):
    shape = (2 * SUBLANE, seq)
    tile = lax.broadcasted_iota(jnp.int32, shape, 1) // LANE
    cv = jnp.where(tile < nw, 0.0, jnp.where(tile + ws >= seq // LANE, t_neg, t_pos))
    hi = cv.astype(BF16).astype(F32)
    row = lax.broadcasted_iota(jnp.int32, shape, 0)
    return jnp.where(row == 0, hi, jnp.where(row == 1, cv - hi, 0.0)).astype(BF16)


def _diff_kernel(pos_sm, tab_sm, dq_ref, dkt_ref, dv_ref, pcol_ref, pch_ref, tabt_ref,
                 lq1_ref, lk1_ref, lq2_ref, lk2_ref, subln_ref, o_ref,
                 sa0_ref, sb0_ref, sa1_ref, sb1_ref, bkt_ref, oslow_ref, *, lambda_init):
    b = pl.program_id(0)
    qi = pl.program_id(1)
    nch, tq, _ = sa0_ref.shape
    nw = tq // LANE + 2
    per = ATTN_CK // LANE

    q_lo = pos_sm[b, qi * tq]
    q_hi = pos_sm[b, qi * tq + tq - 1]
    n_neg = jnp.int32(0)
    n_pos = jnp.int32(0)
    for c in range(nch):
        n_neg = n_neg + (pos_sm[b, c * LANE + LANE - 1] - q_lo <= -T5_SATURATE).astype(jnp.int32)
        n_pos = n_pos + (pos_sm[b, c * LANE] - q_hi >= T5_SATURATE).astype(jnp.int32)
    fits = (nch - n_neg - n_pos) <= nw
    ws = jnp.minimum(n_neg, nch - nw)

    pq = pcol_ref[0]
    lam = (jnp.exp(jnp.sum(lq1_ref[...] * lk1_ref[...], axis=-1, keepdims=True))
           - jnp.exp(jnp.sum(lq2_ref[...] * lk2_ref[...], axis=-1, keepdims=True)) + lambda_init)
    subln_row = subln_ref[...]
    neg_inf = jnp.full((tq, LANE), -jnp.inf, F32)
    slots = ((sa0_ref, sb0_ref), (sa1_ref, sb1_ref))
    pick2 = jnp.where(lax.broadcasted_iota(jnp.int32, (tq, LANE), 1) < 2, 1.0, 0.0).astype(BF16)
    kt_pad = jnp.zeros((LANE - 2 * SUBLANE, ATTN_CK), BF16)

    def gather_bias(tab_b, bkt):
        return jnp.take_along_axis(tab_b, bkt, axis=1, mode="promise_in_bounds")

    @pl.when(fits)
    def _():
        for j in range(nw):
            bkt_ref[j] = _t5_bucket(pch_ref[0, ws + j] - pq)
        ones = jnp.ones((ATTN_CK, LANE), BF16)

        def head_operands(hd):
            t_neg = tab_sm[REL_BUCKETS // 2 - 1, hd] * LOG2E
            t_pos = tab_sm[REL_BUCKETS - 1, hd] * LOG2E
            q1, q2 = _diff_split_q(dq_ref[0, hd])
            return (jnp.concatenate([q1, pick2], axis=1), jnp.concatenate([q2, pick2], axis=1),
                    _bias_rows(t_neg, t_pos, ws, nw, nch * LANE))

        tile_at = [jnp.where(ws + j >= nch, ws + j - nch, ws + j) for j in range(nch)]

        def scores_chunk(hd, c4, q1, q2, rows, m12):
            s1_ref, s2_ref = slots[hd % len(slots)]
            m1, m2 = m12
            sl = slice(c4 * ATTN_CK, (c4 + 1) * ATTN_CK)
            k_tiles = jnp.concatenate([dkt_ref[0, hd, tile_at[c4 * per + t]] for t in range(per)], axis=1)
            kt = jnp.concatenate([k_tiles, rows[:, sl], kt_pad], axis=0)
            sa = jnp.dot(q1, kt, preferred_element_type=F32)
            sb = jnp.dot(q2, kt, preferred_element_type=F32)
            for t in range(per):
                j = c4 * per + t
                va = sa[:, t * LANE:(t + 1) * LANE]
                vb = sb[:, t * LANE:(t + 1) * LANE]
                s1_ref[j] = va
                s2_ref[j] = vb
                if j >= nw:
                    m1 = jnp.maximum(m1, va)
                    m2 = jnp.maximum(m2, vb)
            return m1, m2

        def window_bias_and_max(hd, m12):
            s1_ref, s2_ref = slots[hd % len(slots)]
            m1, m2 = m12
            tab_b = jnp.broadcast_to(tabt_ref[hd:hd + 1, :] * LOG2E, (tq, LANE))
            for j in range(nw):
                bias = gather_bias(tab_b, bkt_ref[j])
                va = s1_ref[j] + bias
                vb = s2_ref[j] + bias
                s1_ref[j] = va
                s2_ref[j] = vb
                m1 = jnp.maximum(m1, va)
                m2 = jnp.maximum(m2, vb)
            return m1, m2

        def pv_chunk(hd, c4, m_cols, accs):
            v_rows = jnp.concatenate([dv_ref[0, hd, tile_at[c4 * per + t]] for t in range(per)], axis=0)
            v_aug = jnp.concatenate([v_rows, ones], axis=1)
            out = []
            for s_ref, m_col, acc in zip(slots[hd % len(slots)], m_cols, accs):
                sv = jnp.concatenate([s_ref[c4 * per + t] for t in range(per)], axis=1)
                pv = jnp.dot(jnp.exp2(sv - m_col).astype(BF16), v_aug, preferred_element_type=F32)
                out.append(pv if acc is None else acc + pv)
            return out

        n_parts = nch // per
        col_max = lambda m12: tuple(jnp.max(m, axis=-1, keepdims=True) for m in m12)
        ops = head_operands(0)
        m12 = (neg_inf, neg_inf)
        for c4 in range(n_parts):
            m12 = scores_chunk(0, c4, *ops, m12)
        m_cols = col_max(window_bias_and_max(0, m12))
        for hd in range(DIFF_HEADS):
            nxt = hd + 1
            if nxt < DIFF_HEADS:
                ops = head_operands(nxt)
            accs = [None, None]
            m12 = (neg_inf, neg_inf)
            for c4 in range(n_parts):
                if nxt < DIFF_HEADS:
                    m12 = scores_chunk(nxt, c4, *ops, m12)
                accs = pv_chunk(hd, c4, m_cols, accs)
            a1, a2 = accs
            o = a1[:, 0:LANE] * (1.0 / a1[:, LANE:2 * LANE]) - (a2[:, 0:LANE] * (1.0 / a2[:, LANE:2 * LANE])) * lam
            o_ref[0, :, hd * DIFF_V:(hd + 1) * DIFF_V] = (
                _rms(o, subln_row) * (1.0 - lambda_init)).astype(o_ref.dtype)
            if nxt < DIFF_HEADS:
                m_cols = col_max(window_bias_and_max(nxt, m12))

    @pl.when(jnp.logical_not(fits))
    def _():
        for c in range(nch):
            bkt_ref[c] = _t5_bucket(pch_ref[0, c] - pq)

        def body(hd, carry):
            q1, q2 = _diff_split_q(dq_ref[0, hd])
            tab_b = jnp.broadcast_to(tabt_ref[pl.ds(hd, 1), :] * LOG2E, (tq, LANE))
            m1 = neg_inf
            m2 = neg_inf
            for c4 in range(nch // per):
                kt = jnp.concatenate([dkt_ref[0, hd, c4 * per + t] for t in range(per)], axis=1)
                sa = jnp.dot(q1, kt, preferred_element_type=F32)
                sb = jnp.dot(q2, kt, preferred_element_type=F32)
                for t in range(per):
                    c = c4 * per + t
                    bias = gather_bias(tab_b, bkt_ref[c])
                    va = sa[:, t * LANE:(t + 1) * LANE] + bias
                    vb = sb[:, t * LANE:(t + 1) * LANE] + bias
                    sa0_ref[c] = va
                    sb0_ref[c] = vb
                    m1 = jnp.maximum(m1, va)
                    m2 = jnp.maximum(m2, vb)
            oslow_ref[hd] = _diff_finish(sa0_ref, sb0_ref, m1, m2, dv_ref.at[0, hd], lam, subln_row,
                                         lambda_init)
            return carry

        lax.fori_loop(0, DIFF_HEADS, body, 0)
        for hd in range(DIFF_HEADS):
            o_ref[0, :, hd * DIFF_V:(hd + 1) * DIFF_V] = oslow_ref[hd].astype(o_ref.dtype)


def _diff_attn(positions, rel_bias, dq, dkt, dv, pos_col, pos_tiles, tab_t, lq1, lk1, lq2, lk2, subln,
               lambda_init):
    bsz, nh, seq, _ = dq.shape
    tq = DIFF_TQ
    nch = seq // LANE
    const2 = lambda b, i, *_: (0, 0)
    once = pl.Buffered(1)
    grid_spec = pltpu.PrefetchScalarGridSpec(
        num_scalar_prefetch=2,
        grid=(bsz, seq // tq),
        in_specs=[
            pl.BlockSpec((1, nh, tq, LANE), lambda b, i, *_: (b, 0, i, 0)),
            pl.BlockSpec((1, nh, nch, LANE, LANE), lambda b, i, *_: (b, 0, 0, 0, 0), pipeline_mode=once),
            pl.BlockSpec((1, nh, nch, LANE, LANE), lambda b, i, *_: (b, 0, 0, 0, 0), pipeline_mode=once),
            pl.BlockSpec((1, tq, 1), lambda b, i, *_: (b, i, 0)),
            pl.BlockSpec((1, nch, 1, LANE), lambda b, i, *_: (b, 0, 0, 0)),
            pl.BlockSpec(tab_t.shape, const2),
            pl.BlockSpec((1, DIFF_QK), const2),
            pl.BlockSpec((1, DIFF_QK), const2),
            pl.BlockSpec((1, DIFF_QK), const2),
            pl.BlockSpec((1, DIFF_QK), const2),
            pl.BlockSpec((1, DIFF_V), const2),
        ],
        out_specs=pl.BlockSpec((1, tq, nh * DIFF_V), lambda b, i, *_: (b, i, 0)),
        scratch_shapes=[
            *([pltpu.VMEM((nch, tq, LANE), F32)] * 4),
            pltpu.VMEM((nch, tq, LANE), jnp.int32),
            pltpu.VMEM((nh, tq, LANE), F32),
        ],
    )
    return pl.pallas_call(
        partial(_diff_kernel, lambda_init=lambda_init),
        grid_spec=grid_spec,
        out_shape=jax.ShapeDtypeStruct((bsz, seq, nh * DIFF_V), BF16),
        compiler_params=_params("arbitrary", "arbitrary"),
        name="diff_attn",
    )(positions, rel_bias, dq, dkt, dv.reshape(bsz, nh, nch, LANE, LANE), pos_col, pos_tiles, tab_t,
      lq1, lk1, lq2, lk2, subln)


def _out_proj_kernel(om_ref, od_ref, wa_ref, wb_ref, x_ref, g_ref, post_ref, sh_ref, sc_ref, pre_ref,
                     x1_ref, h2_ref, il_ref):
    _, nph, rows, d = x_ref.shape
    tm = nph * rows
    merge = lambda ref: ref[0].reshape(tm, ref.shape[3])
    y = (jnp.dot(merge(om_ref), wa_ref[...], preferred_element_type=F32)
         + jnp.dot(merge(od_ref), wb_ref[...], preferred_element_type=F32))
    x1 = merge(x_ref) + g_ref[0] * _rms(y, post_ref[...])
    x1_ref[0] = x1.reshape(nph, rows, d)
    a_row = pre_ref[...] * (1.0 + sc_ref[0])
    ms = jnp.mean(x1 * x1, axis=-1, keepdims=True)
    h2 = (x1 * lax.rsqrt(ms + NORM_EPS)) * a_row + sh_ref[0]
    for j in range(il_ref.shape[0]):
        for s in range(nph):
            il_ref[j, pl.ds(s, rows, stride=nph), :] = h2[s * rows:(s + 1) * rows, j * LANE:(j + 1) * LANE]
    for j in range(il_ref.shape[0]):
        h2_ref[:, j * LANE:(j + 1) * LANE] = il_ref[j].astype(h2_ref.dtype)


def _out_proj(o_mla, o_diff, wo_a, wo_b, x2, g_a, post_w, sh_f, sc_f, pre_w, seq):
    m, d = x2.shape
    bsz = m // seq
    tm = OUTPROJ_TM
    rows = tm // ROW_IL
    tpb = seq // tm
    phases = lambda a: a.reshape(bsz, ROW_IL, seq // ROW_IL, a.shape[-1])
    ph_block = lambda w: pl.BlockSpec((1, ROW_IL, rows, w), lambda i: (i // tpb, 0, i % tpb, 0))
    bmap3 = lambda i: (i // tpb, 0, 0)
    x1, h2 = pl.pallas_call(
        _out_proj_kernel,
        grid=(m // tm,),
        in_specs=[
            ph_block(o_mla.shape[1]),
            ph_block(o_diff.shape[1]),
            _resident(wo_a.shape),
            _resident(wo_b.shape),
            ph_block(d),
            pl.BlockSpec((1, 1, d), bmap3),
            _resident((1, d)),
            pl.BlockSpec((1, 1, d), bmap3),
            pl.BlockSpec((1, 1, d), bmap3),
            _resident((1, d)),
        ],
        out_specs=(ph_block(d), pl.BlockSpec((tm, d), lambda i: (i, 0))),
        out_shape=(jax.ShapeDtypeStruct((bsz, ROW_IL, seq // ROW_IL, d), F32),
                   jax.ShapeDtypeStruct((m, d), BF16)),
        scratch_shapes=[pltpu.VMEM((d // LANE, tm, LANE), F32)],
        compiler_params=_params("arbitrary"),
        name="out_proj",
    )(phases(o_mla), phases(o_diff), wo_a, wo_b, phases(x2), g_a, post_w, sh_f, sc_f, pre_w)
    return x1.reshape(m, d), h2


def _gelu_tanh(g):
    a = -2.0 * math.sqrt(2.0 / math.pi) * LOG2E
    return g * (1.0 / (1.0 + jnp.exp2(g * (a + (a * 0.044715) * (g * g)))))


def _neighbour_rows(u, step):
    n = u.shape[0]
    row = lax.broadcasted_iota(jnp.int32, (ROW_IL, u.shape[1]), 0)
    if step == -1:
        wrap = jnp.where(row == 0, 0.0, pltpu.roll(u[n - ROW_IL:], 1, 0))
        return jnp.concatenate([wrap, u[:n - ROW_IL]], axis=0)
    wrap = jnp.where(row == ROW_IL - 1, 0.0, pltpu.roll(u[0:ROW_IL], ROW_IL - 1, 0))
    return jnp.concatenate([u[ROW_IL:], wrap], axis=0)


def _ffn_up_kernel(h_ref, wg_ref, wv_ref, cwg_ref, cwv_ref, cbg_ref, cbv_ref, o_ref):
    h = h_ref[...]

    def conv(u, cw_ref, cb_ref):
        return (cb_ref[...] + u * cw_ref[1:2, :]
                + _neighbour_rows(u, -1) * cw_ref[0:1, :] + _neighbour_rows(u, 1) * cw_ref[2:3, :])

    gate = conv(jnp.dot(h, wg_ref[...].astype(BF16), preferred_element_type=F32), cwg_ref, cbg_ref)
    val = conv(jnp.dot(h, wv_ref[...].astype(BF16), preferred_element_type=F32), cwv_ref, cbv_ref)
    o_ref[...] = (_gelu_tanh(gate) * val).astype(o_ref.dtype)


def _ffn_up(h2, w_up, conv_w, conv_b, seq):
    m, d = h2.shape
    f = w_up.shape[1] // 2
    tn = UP_TN
    nj = f // tn
    return pl.pallas_call(
        _ffn_up_kernel,
        grid=(nj, m // seq),
        in_specs=[
            pl.BlockSpec((seq, d), lambda j, i: (i, 0)),
            pl.BlockSpec((d, tn), lambda j, i: (0, j)),
            pl.BlockSpec((d, tn), lambda j, i: (0, j + nj)),
            pl.BlockSpec((3, tn), lambda j, i: (0, j)),
            pl.BlockSpec((3, tn), lambda j, i: (0, j + nj)),
            pl.BlockSpec((1, tn), lambda j, i: (0, j)),
            pl.BlockSpec((1, tn), lambda j, i: (0, j + nj)),
        ],
        out_specs=pl.BlockSpec((seq, tn), lambda j, i: (i, j)),
        out_shape=jax.ShapeDtypeStruct((m, f), BF16),
        compiler_params=_params("arbitrary", "arbitrary"),
        name="ffn_up",
    )(h2, w_up, w_up, conv_w, conv_w, conv_b, conv_b)


def _ffn_down_kernel(a_ref, w_ref, x1_ref, g_ref, post_ref, o_ref, y_ref):
    tm = a_ref.shape[0]
    y = jnp.dot(a_ref[...], w_ref[...], preferred_element_type=F32)
    yn = _rms(y, post_ref[...])
    g = g_ref[0]
    for j in range(y_ref.shape[0]):
        y_ref[j] = yn[:, j * LANE:(j + 1) * LANE]
    for s in range(ROW_IL):
        for j in range(y_ref.shape[0]):
            cs = slice(j * LANE, (j + 1) * LANE)
            rows = y_ref[j, pl.ds(s, tm // ROW_IL, stride=ROW_IL), :]
            o_ref[0, s, :, cs] = x1_ref[0, s, :, cs] + g[:, cs] * rows


def _ffn_down(act, w_down, x1, g_f, post_w, seq):
    m, f = act.shape
    d = w_down.shape[1]
    tm = DOWN_TM
    tpb = seq // tm
    bsz = m // seq
    nat_block = pl.BlockSpec((1, ROW_IL, tm // ROW_IL, d), lambda i: (i // tpb, 0, i % tpb, 0))
    out = pl.pallas_call(
        _ffn_down_kernel,
        grid=(m // tm,),
        in_specs=[
            pl.BlockSpec((tm, f), lambda i: (i, 0)),
            _resident(w_down.shape),
            nat_block,
            pl.BlockSpec((1, 1, d), lambda i: (i // tpb, 0, 0)),
            _resident((1, d)),
        ],
        out_specs=nat_block,
        out_shape=jax.ShapeDtypeStruct((bsz, ROW_IL, seq // ROW_IL, d), F32),
        scratch_shapes=[pltpu.VMEM((d // LANE, tm, LANE), F32)],
        compiler_params=_params("arbitrary"),
        name="ffn_down",
    )(act, w_down, x1.reshape(bsz, ROW_IL, seq // ROW_IL, d), g_f, post_w)
    return out.reshape(m, d)


def _rope_pair_tiles(w_rope):
    half = MLA_ROPE // 2
    t1, t2 = w_rope[:, :half], w_rope[:, half:]
    z = jnp.zeros((w_rope.shape[0], LANE - MLA_ROPE), w_rope.dtype)
    return jnp.concatenate([t1, t2, z], axis=1), jnp.concatenate([t2, t1, z], axis=1)


def _layout_w_in(w_in):
    o_kv = MLA_Q_RANK
    o_kr = o_kv + MLA_KV_RANK
    o_dq = o_kr + MLA_ROPE
    o_dk = o_dq + DIFF_HEADS * 2 * DIFF_QK
    o_dv = o_dk + DIFF_HEADS * 2 * DIFF_QK
    w1 = jnp.concatenate([w_in[:, :o_kr], w_in[:, o_dq:o_dk], w_in[:, o_dv:]], axis=1).astype(BF16)
    ra, rb = _rope_pair_tiles(w_in[:, o_kr:o_dq])
    wt = jnp.concatenate([ra, rb, w_in[:, o_dk:o_dv]], axis=1).T.astype(BF16)
    return w1, wt


def _layout_w_uq(w_uq):
    k = w_uq.shape[0]
    w3 = w_uq.reshape(k, MLA_HEADS, MLA_NOPE + MLA_ROPE)
    blocks = []
    for hd in range(MLA_HEADS):
        ra, rb = _rope_pair_tiles(w3[:, hd, MLA_NOPE:])
        blocks += [w3[:, hd, :MLA_NOPE], ra, rb]
    return jnp.concatenate(blocks, axis=1).astype(BF16)


def _layout_w_ukv(w_ukv):
    k = w_ukv.shape[0]
    w3 = w_ukv.reshape(k, MLA_HEADS, MLA_NOPE + MLA_V)
    wukt = w3[:, :, :MLA_NOPE].reshape(k, MLA_HEADS * MLA_NOPE).T.astype(BF16)
    wuv = w3[:, :, MLA_NOPE:].reshape(k, MLA_HEADS * MLA_V).astype(BF16)
    return wukt, wuv


def _rope_kernel(pos_ref, inv_ref, cr_ref, sr_ref, ct_ref, st_ref):
    ang = inv_ref[...] * pos_ref[0].astype(F32)
    c = jnp.cos(ang)
    s = jnp.sin(ang)
    zeros = jnp.zeros((LANE - MLA_ROPE, ang.shape[1]), F32)
    ct = jnp.concatenate([c, c, zeros], axis=0)
    st = jnp.concatenate([-s, s, zeros], axis=0)
    ct_ref[0] = ct
    st_ref[0] = st
    cr_ref[0] = ct.T
    sr_ref[0] = st.T


def _rope_tables(pos_row):
    bsz, _, seq = pos_row.shape
    inv = 1.0 / (ROPE_THETA ** (np.arange(0, MLA_ROPE, 2, dtype=np.float32) / MLA_ROPE))
    inv_col = jnp.asarray(inv.astype(np.float32).reshape(MLA_ROPE // 2, 1))
    row_tab = jax.ShapeDtypeStruct((bsz, seq, LANE), F32)
    col_tab = jax.ShapeDtypeStruct((bsz, LANE, seq), F32)
    return pl.pallas_call(
        _rope_kernel,
        grid=(bsz,),
        in_specs=[pl.BlockSpec((1, 1, seq), lambda b: (b, 0, 0)),
                  pl.BlockSpec(inv_col.shape, lambda b: (0, 0))],
        out_specs=(pl.BlockSpec((1, seq, LANE), lambda b: (b, 0, 0)),
                   pl.BlockSpec((1, seq, LANE), lambda b: (b, 0, 0)),
                   pl.BlockSpec((1, LANE, seq), lambda b: (b, 0, 0)),
                   pl.BlockSpec((1, LANE, seq), lambda b: (b, 0, 0))),
        out_shape=(row_tab, row_tab, col_tab, col_tab),
        compiler_params=_params("arbitrary"),
        name="rope_tables",
    )(pos_row, inv_col)


def kernel(x, c, positions, rel_bias, ada_w, ada_b, attn_pre_norm, attn_post_norm, w_in, q_norm, w_uq,
           kv_norm, w_ukv, lambda_q1, lambda_k1, lambda_q2, lambda_k2, diff_subln, w_o, ffn_pre_norm,
           ffn_post_norm, w_up, conv_w, conv_b, w_down):
    bsz, seq, d = x.shape
    depth = ada_w.shape[0]
    m = bsz * seq
    positions = positions.astype(jnp.int32)
    pos_col = positions.reshape(bsz, seq, 1)
    pos_row = positions.reshape(bsz, 1, seq)
    pos_tiles = positions.reshape(bsz, seq // LANE, 1, LANE)
    rope = _rope_tables(pos_row)
    tab_t = jnp.zeros((DIFF_HEADS, LANE), F32).at[:, :REL_BUCKETS].set(rel_bias.astype(F32).T)
    row = lambda v: v.reshape(1, -1).astype(F32)

    x2 = x.reshape(m, d)
    for l in range(depth):
        lambda_init = 0.8 - 0.6 * math.exp(-0.3 * l)
        mod = _ada_mod(c, ada_w[l], ada_b[l])
        sh_a, sc_a, g_a, sh_f, sc_f, g_f = [t.reshape(bsz, 1, d) for t in jnp.split(mod, N_MOD, axis=-1)]

        w1, wt = _layout_w_in(w_in[l])
        wuq = _layout_w_uq(w_uq[l])
        wukt, wuv = _layout_w_ukv(w_ukv[l])
        q, knt, krt, v, dq, dkt, dv = _in_proj(
            x2, rope, sh_a, sc_a, row(attn_pre_norm[l]), w1, wt, row(q_norm[l]), wuq,
            row(kv_norm[l]), wukt, wuv, bsz, seq)

        o_mla = _mla_attn(q, knt, krt, v)
        o_diff = _diff_attn(positions, rel_bias.astype(F32), dq, dkt, dv, pos_col, pos_tiles, tab_t,
                            row(lambda_q1[l]), row(lambda_k1[l]), row(lambda_q2[l]), row(lambda_k2[l]),
                            row(diff_subln[l]), lambda_init)

        wo = w_o[l].astype(BF16)
        n_mla = MLA_HEADS * MLA_V
        x1, h2 = _out_proj(o_mla.reshape(m, n_mla), o_diff.reshape(m, DIFF_HEADS * DIFF_V),
                           wo[:n_mla], wo[n_mla:], x2, g_a, row(attn_post_norm[l]), sh_f, sc_f,
                           row(ffn_pre_norm[l]), seq)

        act = _ffn_up(h2, w_up[l], conv_w[l].astype(F32), row(conv_b[l]), seq)
        x2 = _ffn_down(act, w_down[l].astype(BF16), x1, g_f, row(ffn_post_norm[l]), seq)
    return x2.reshape(bsz, seq, d)
```

```python
import math
from functools import partial

import jax
import jax.numpy as jnp
import numpy as np
from jax import lax
from jax.experimental import pallas as pl
from jax.experimental.pallas import tpu as pltpu

MLA_HEADS = 8
MLA_Q_RANK = 512
MLA_KV_RANK = 256
MLA_NOPE = 128
MLA_ROPE = 64
MLA_V = 128
MLA_SCALE = (MLA_NOPE + MLA_ROPE) ** -0.5
DIFF_HEADS = 8
DIFF_QK = 64
DIFF_V = 128
DIFF_SCALE = DIFF_QK ** -0.5
REL_BUCKETS = 32
ROPE_THETA = 10000.0
NORM_EPS = 1e-6
N_MOD = 6
LOG2E = math.log2(math.e)
T5_SATURATE = 128
F32_MANT_BITS = 23
F32_EXP_MASK = 0xFF
F32_EXP_BIAS = 127
SQRT2_MANT = 0x3504F4

LANE = 128
SUBLANE = 8
VMEM_LIMIT_BYTES = 56 * 1024 * 1024

ADA_TN = 1536
LAYOUT_TK = 256
INPROJ_TM = 512
ATTN_TQ = 512
DIFF_TQ = 256
ATTN_CK = 512
OUTPROJ_TM = 512
ROW_IL = 8
UP_TN = 512
DOWN_TM = 256

F32 = jnp.float32
BF16 = jnp.bfloat16


def _params(*semantics):
    return pltpu.CompilerParams(dimension_semantics=semantics, vmem_limit_bytes=VMEM_LIMIT_BYTES)


def _resident(shape):
    nd = len(shape)
    return pl.BlockSpec(shape, lambda *_: (0,) * nd, pipeline_mode=pl.Buffered(1))


def _rms(x, w_row):
    ms = jnp.mean(x * x, axis=-1, keepdims=True)
    return (x * lax.rsqrt(ms + NORM_EPS)) * w_row


def _nt_dot(a, b):
    return lax.dot_general(a, b, (((1,), (1,)), ((), ())), preferred_element_type=F32)


def _ada_kernel(c_ref, w_ref, b_ref, o_ref):
    c = c_ref[...]
    c_act = (c * jax.nn.sigmoid(c)).astype(BF16)
    o_ref[...] = jnp.dot(c_act, w_ref[...].astype(BF16), preferred_element_type=F32) + b_ref[...]


def _ada_mod(c, ada_w, ada_b):
    bsz, d = c.shape
    n = ada_w.shape[1]
    return pl.pallas_call(
        _ada_kernel,
        grid=(n // ADA_TN,),
        in_specs=[
            pl.BlockSpec((bsz, d), lambda j: (0, 0)),
            pl.BlockSpec((d, ADA_TN), lambda j: (0, j)),
            pl.BlockSpec((1, ADA_TN), lambda j: (0, j)),
        ],
        out_specs=pl.BlockSpec((bsz, ADA_TN), lambda j: (0, j)),
        out_shape=jax.ShapeDtypeStruct((bsz, n), F32),
        compiler_params=_params("arbitrary"),
        name="ada_mod",
    )(c, ada_w, ada_b.reshape(1, n))


N_W1 = MLA_Q_RANK + MLA_KV_RANK + 2 * DIFF_HEADS * 2 * DIFF_QK
N_WT = 2 * LANE + DIFF_HEADS * 2 * DIFF_QK
Q_HEAD_COLS = 3 * LANE


def _in_proj_kernel(x_ref, cr_ref, sr_ref, ct_ref, st_ref, sh_ref, sc_ref, pre_ref, w1_ref, wt_ref,
                    qn_ref, wuq_ref, kvn_ref, wukt_ref, wuv_ref,
                    q_out, knt_out, krt_out, v_out, dq_out, dkt_out, dv_out):
    x = x_ref[...]
    a_row = pre_ref[...] * (1.0 + sc_ref[0])
    ms = jnp.mean(x * x, axis=-1, keepdims=True)
    h = ((x * lax.rsqrt(ms + NORM_EPS)) * a_row + sh_ref[0]).astype(BF16)

    p1 = jnp.dot(h, w1_ref[...], preferred_element_type=F32)
    p2t = _nt_dot(wt_ref[...], h)

    cos_r, sin_r = cr_ref[0], sr_ref[0]
    cos_t, sin_t = ct_ref[0], st_ref[0]

    q_lat = p1[:, 0:MLA_Q_RANK]
    rq = _rms(q_lat, qn_ref[...] * (MLA_SCALE * LOG2E)).astype(BF16)
    qf = jnp.dot(rq, wuq_ref[...], preferred_element_type=F32)
    for hd in range(MLA_HEADS):
        c0 = hd * Q_HEAD_COLS
        q_out[0, hd, :, 0:LANE] = qf[:, c0:c0 + LANE].astype(BF16)
        rope = qf[:, c0 + LANE:c0 + 2 * LANE] * cos_r + qf[:, c0 + 2 * LANE:c0 + 3 * LANE] * sin_r
        q_out[0, hd, :, LANE:2 * LANE] = rope.astype(BF16)

    kv_lat = p1[:, MLA_Q_RANK:MLA_Q_RANK + MLA_KV_RANK]
    rkv = _rms(kv_lat, kvn_ref[...]).astype(BF16)
    knt = _nt_dot(wukt_ref[...], rkv)
    vv = jnp.dot(rkv, wuv_ref[...], preferred_element_type=F32)
    for hd in range(MLA_HEADS):
        knt_out[0, hd] = knt[hd * LANE:(hd + 1) * LANE, :].astype(BF16)
        v_out[0, hd] = vv[:, hd * LANE:(hd + 1) * LANE].astype(BF16)
    krt_out[0] = (p2t[0:LANE, :] * cos_t + p2t[LANE:2 * LANE, :] * sin_t).astype(BF16)

    o_dq = MLA_Q_RANK + MLA_KV_RANK
    o_dv = o_dq + DIFF_HEADS * 2 * DIFF_QK
    for hd in range(DIFF_HEADS):
        dq_out[0, hd] = (p1[:, o_dq + hd * LANE:o_dq + (hd + 1) * LANE] * (DIFF_SCALE * LOG2E)).astype(BF16)
        dv_out[0, hd] = p1[:, o_dv + hd * LANE:o_dv + (hd + 1) * LANE].astype(BF16)
        dkt = p2t[2 * LANE + hd * LANE:2 * LANE + (hd + 1) * LANE, :].astype(BF16)
        for t in range(dkt_out.shape[2]):
            dkt_out[0, hd, t] = dkt[:, t * LANE:(t + 1) * LANE]


def _in_proj(x2, rope, sh_a, sc_a, pre_w, w1, wt, q_norm, wuq, kv_norm, wukt, wuv, bsz, seq):
    tm = INPROJ_TM
    tpb = seq // tm
    d = x2.shape[1]
    bmap3 = lambda i: (i // tpb, 0, 0)
    out_shapes = (
        jax.ShapeDtypeStruct((bsz, MLA_HEADS, seq, 2 * LANE), BF16),
        jax.ShapeDtypeStruct((bsz, MLA_HEADS, LANE, seq), BF16),
        jax.ShapeDtypeStruct((bsz, LANE, seq), BF16),
        jax.ShapeDtypeStruct((bsz, MLA_HEADS, seq, LANE), BF16),
        jax.ShapeDtypeStruct((bsz, DIFF_HEADS, seq, LANE), BF16),
        jax.ShapeDtypeStruct((bsz, DIFF_HEADS, seq // LANE, LANE, LANE), BF16),
        jax.ShapeDtypeStruct((bsz, DIFF_HEADS, seq, LANE), BF16),
    )
    row_blk = lambda nh, w: pl.BlockSpec((1, nh, tm, w), lambda i: (i // tpb, 0, i % tpb, 0))
    col_blk = lambda nh: pl.BlockSpec((1, nh, LANE, tm), lambda i: (i // tpb, 0, 0, i % tpb))
    return pl.pallas_call(
        _in_proj_kernel,
        grid=(bsz * tpb,),
        in_specs=[
            pl.BlockSpec((tm, d), lambda i: (i, 0)),
            pl.BlockSpec((1, tm, LANE), lambda i: (i // tpb, i % tpb, 0)),
            pl.BlockSpec((1, tm, LANE), lambda i: (i // tpb, i % tpb, 0)),
            pl.BlockSpec((1, LANE, tm), lambda i: (i // tpb, 0, i % tpb)),
            pl.BlockSpec((1, LANE, tm), lambda i: (i // tpb, 0, i % tpb)),
            pl.BlockSpec((1, 1, d), bmap3),
            pl.BlockSpec((1, 1, d), bmap3),
            _resident((1, d)),
            _resident(w1.shape),
            _resident(wt.shape),
            _resident(q_norm.shape),
            _resident(wuq.shape),
            _resident(kv_norm.shape),
            _resident(wukt.shape),
            _resident(wuv.shape),
        ],
        out_specs=(
            row_blk(MLA_HEADS, 2 * LANE),
            col_blk(MLA_HEADS),
            pl.BlockSpec((1, LANE, tm), lambda i: (i // tpb, 0, i % tpb)),
            row_blk(MLA_HEADS, LANE),
            row_blk(DIFF_HEADS, LANE),
            pl.BlockSpec((1, DIFF_HEADS, tm // LANE, LANE, LANE), lambda i: (i // tpb, 0, i % tpb, 0, 0)),
            row_blk(DIFF_HEADS, LANE),
        ),
        out_shape=out_shapes,
        compiler_params=_params("arbitrary"),
        name="in_proj",
    )(x2, *rope, sh_a, sc_a, pre_w, w1, wt, q_norm, wuq, kv_norm, wukt, wuv)


def _lane_tile_reduce(x, op):
    out = x[:, 0:LANE]
    for t in range(1, x.shape[1] // LANE):
        out = op(out, x[:, t * LANE:(t + 1) * LANE])
    return out


def _pv_normalised(chunks, v_ref):
    ones = jnp.ones((ATTN_CK, LANE), BF16)
    acc = None
    for c, p in enumerate(chunks):
        v_aug = jnp.concatenate([v_ref[c * ATTN_CK:(c + 1) * ATTN_CK, :], ones], axis=1)
        pv = jnp.dot(p.astype(BF16), v_aug, preferred_element_type=F32)
        acc = pv if acc is None else acc + pv
    return acc[:, 0:LANE] * (1.0 / acc[:, LANE:2 * LANE])


def _mla_kernel(q_ref, knt_ref, krt_ref, v_ref, o_ref, s_ref):
    seq = s_ref.shape[2]
    n_chunks = seq // ATTN_CK
    ones = jnp.ones((ATTN_CK, LANE), BF16)

    def scores_chunk(hd, c, m_part):
        sl = slice(c * ATTN_CK, (c + 1) * ATTN_CK)
        kt = jnp.concatenate([knt_ref[0, hd, :, sl], krt_ref[0, :, sl]], axis=0)
        s = jnp.dot(q_ref[0, hd], kt, preferred_element_type=F32)
        s_ref[hd % 2, :, sl] = s
        mp = _lane_tile_reduce(s, jnp.maximum)
        return mp if m_part is None else jnp.maximum(m_part, mp)

    def pv_chunk(hd, c, m_col, acc):
        sl = slice(c * ATTN_CK, (c + 1) * ATTN_CK)
        p = jnp.exp2(s_ref[hd % 2, :, sl] - m_col).astype(BF16)
        pv = jnp.dot(p, jnp.concatenate([v_ref[0, hd, sl, :], ones], axis=1), preferred_element_type=F32)
        return pv if acc is None else acc + pv

    m_part = None
    for c in range(n_chunks):
        m_part = scores_chunk(0, c, m_part)
    for hd in range(MLA_HEADS):
        m_col = jnp.max(m_part, axis=-1, keepdims=True)
        m_part = None
        acc = None
        for c in range(n_chunks):
            if hd + 1 < MLA_HEADS:
                m_part = scores_chunk(hd + 1, c, m_part)
            acc = pv_chunk(hd, c, m_col, acc)
        o = acc[:, 0:LANE] * (1.0 / acc[:, LANE:2 * LANE])
        o_ref[0, :, hd * MLA_V:(hd + 1) * MLA_V] = o.astype(o_ref.dtype)


def _mla_attn(q, knt, krt, v):
    bsz, nh, seq, _ = q.shape
    tq = ATTN_TQ
    return pl.pallas_call(
        _mla_kernel,
        grid=(bsz, seq // tq),
        in_specs=[
            pl.BlockSpec((1, nh, tq, 2 * LANE), lambda b, i: (b, 0, i, 0)),
            pl.BlockSpec((1, nh, LANE, seq), lambda b, i: (b, 0, 0, 0)),
            pl.BlockSpec((1, LANE, seq), lambda b, i: (b, 0, 0)),
            pl.BlockSpec((1, nh, seq, LANE), lambda b, i: (b, 0, 0, 0)),
        ],
        out_specs=pl.BlockSpec((1, tq, nh * MLA_V), lambda b, i: (b, i, 0)),
        out_shape=jax.ShapeDtypeStruct((bsz, seq, nh * MLA_V), BF16),
        scratch_shapes=[pltpu.VMEM((2, tq, seq), F32)],
        compiler_params=_params("arbitrary", "arbitrary"),
        name="mla_attn",
    )(q, knt, krt, v)


def _t5_bucket(rel):
    n = jnp.abs(rel)
    bits = lax.bitcast_convert_type(n.astype(F32), jnp.int32)
    twice_exp = jnp.bitwise_and(lax.shift_right_logical(bits, F32_MANT_BITS - 1), 2 * F32_EXP_MASK)
    upper_half = jnp.where(jnp.bitwise_and(bits, (1 << F32_MANT_BITS) - 1) >= SQRT2_MANT, 1, 0)
    large = jnp.minimum(twice_exp - (2 * F32_EXP_BIAS - 2) + upper_half, REL_BUCKETS // 2 - 1)
    return jnp.where(rel > 0, REL_BUCKETS // 2, 0) + jnp.where(n < REL_BUCKETS // 4, n, large)


def _diff_split_q(q):
    lane = lax.broadcasted_iota(jnp.int32, q.shape, 1)
    zero = jnp.zeros_like(q)
    return jnp.where(lane < DIFF_QK, q, zero), jnp.where(lane >= DIFF_QK, q, zero)


class _TiledRows:
    def __init__(self, ref):
        self.ref = ref

    def __getitem__(self, idx):
        rows, _ = idx
        tiles = range(rows.start // LANE, rows.stop // LANE)
        return jnp.concatenate([self.ref[t] for t in tiles], axis=0)


def _softmax_pv_tiles(s_ref, m_col, v_ref):
    per = ATTN_CK // LANE
    chunks = []
    for c in range(s_ref.shape[0] // per):
        sv = jnp.concatenate([s_ref[c * per + t] for t in range(per)], axis=1)
        chunks.append(jnp.exp2(sv - m_col))
    return _pv_normalised(chunks, _TiledRows(v_ref))


def _diff_finish(s1_ref, s2_ref, m1, m2, v_ref, lam, subln_row, lambda_init):
    m1c = jnp.max(m1, axis=-1, keepdims=True)
    m2c = jnp.max(m2, axis=-1, keepdims=True)
    o = _softmax_pv_tiles(s1_ref, m1c, v_ref) - _softmax_pv_tiles(s2_ref, m2c, v_ref) * lam
    return _rms(o, subln_row) * (1.0 - lambda_init)


def _bias_rows(t_neg, t_pos, ws, nw, seq):
    shape = (2 * SUBLANE, seq)
    tile = lax.broadcasted_iota(jnp.int32, shape, 1) // LANE
    cv = jnp.where(tile < nw, 0.0, jnp.where(tile + ws >= seq // LANE, t_neg, t_pos))
    hi = cv.astype(BF16).astype(F32)
    row = lax.broadcasted_iota(jnp.int32, shape, 0)
    return jnp.where(row == 0, hi, jnp.where(row == 1, cv - hi, 0.0)).astype(BF16)


def _diff_kernel(pos_sm, tab_sm, dq_ref, dkt_ref, dv_ref, pcol_ref, pch_ref, tabt_ref,
                 lq1_ref, lk1_ref, lq2_ref, lk2_ref, subln_ref, o_ref,
                 sa0_ref, sb0_ref, sa1_ref, sb1_ref, bkt_ref, oslow_ref, *, lambda_init):
    b = pl.program_id(0)
    qi = pl.program_id(1)
    nch, tq, _ = sa0_ref.shape
    nw = tq // LANE + 2
    per = ATTN_CK // LANE

    q_lo = pos_sm[b, qi * tq]
    q_hi = pos_sm[b, qi * tq + tq - 1]
    n_neg = jnp.int32(0)
    n_pos = jnp.int32(0)
    for c in range(nch):
        n_neg = n_neg + (pos_sm[b, c * LANE + LANE - 1] - q_lo <= -T5_SATURATE).astype(jnp.int32)
        n_pos = n_pos + (pos_sm[b, c * LANE] - q_hi >= T5_SATURATE).astype(jnp.int32)
    fits = (nch - n_neg - n_pos) <= nw
    ws = jnp.minimum(n_neg, nch - nw)

    pq = pcol_ref[0]
    lam = (jnp.exp(jnp.sum(lq1_ref[...] * lk1_ref[...], axis=-1, keepdims=True))
           - jnp.exp(jnp.sum(lq2_ref[...] * lk2_ref[...], axis=-1, keepdims=True)) + lambda_init)
    subln_row = subln_ref[...]
    neg_inf = jnp.full((tq, LANE), -jnp.inf, F32)
    slots = ((sa0_ref, sb0_ref), (sa1_ref, sb1_ref))
    pick2 = jnp.where(lax.broadcasted_iota(jnp.int32, (tq, LANE), 1) < 2, 1.0, 0.0).astype(BF16)
    kt_pad = jnp.zeros((LANE - 2 * SUBLANE, ATTN_CK), BF16)

    def gather_bias(tab_b, bkt):
        return jnp.take_along_axis(tab_b, bkt, axis=1, mode="promise_in_bounds")

    @pl.when(fits)
    def _():
        for j in range(nw):
            bkt_ref[j] = _t5_bucket(pch_ref[0, ws + j] - pq)
        ones = jnp.ones((ATTN_CK, LANE), BF16)

        def head_operands(hd):
            t_neg = tab_sm[REL_BUCKETS // 2 - 1, hd] * LOG2E
            t_pos = tab_sm[REL_BUCKETS - 1, hd] * LOG2E
            q1, q2 = _diff_split_q(dq_ref[0, hd])
            return (jnp.concatenate([q1, pick2], axis=1), jnp.concatenate([q2, pick2], axis=1),
                    _bias_rows(t_neg, t_pos, ws, nw, nch * LANE))

        tile_at = [jnp.where(ws + j >= nch, ws + j - nch, ws + j) for j in range(nch)]

        def scores_chunk(hd, c4, q1, q2, rows, m12):
            s1_ref, s2_ref = slots[hd % len(slots)]
            m1, m2 = m12
            sl = slice(c4 * ATTN_CK, (c4 + 1) * ATTN_CK)
            k_tiles = jnp.concatenate([dkt_ref[0, hd, tile_at[c4 * per + t]] for t in range(per)], axis=1)
            kt = jnp.concatenate([k_tiles, rows[:, sl], kt_pad], axis=0)
            sa = jnp.dot(q1, kt, preferred_element_type=F32)
            sb = jnp.dot(q2, kt, preferred_element_type=F32)
            for t in range(per):
                j = c4 * per + t
                va = sa[:, t * LANE:(t + 1) * LANE]
                vb = sb[:, t * LANE:(t + 1) * LANE]
                s1_ref[j] = va
                s2_ref[j] = vb
                if j >= nw:
                    m1 = jnp.maximum(m1, va)
                    m2 = jnp.maximum(m2, vb)
            return m1, m2

        def window_bias_and_max(hd, m12):
            s1_ref, s2_ref = slots[hd % len(slots)]
            m1, m2 = m12
            tab_b = jnp.broadcast_to(tabt_ref[hd:hd + 1, :] * LOG2E, (tq, LANE))
            for j in range(nw):
                bias = gather_bias(tab_b, bkt_ref[j])
                va = s1_ref[j] + bias
                vb = s2_ref[j] + bias
                s1_ref[j] = va
                s2_ref[j] = vb
                m1 = jnp.maximum(m1, va)
                m2 = jnp.maximum(m2, vb)
            return m1, m2

        def pv_chunk(hd, c4, m_cols, accs):
            v_rows = jnp.concatenate([dv_ref[0, hd, tile_at[c4 * per + t]] for t in range(per)], axis=0)
            v_aug = jnp.concatenate([v_rows, ones], axis=1)
            out = []
            for s_ref, m_col, acc in zip(slots[hd % len(slots)], m_cols, accs):
                sv = jnp.concatenate([s_ref[c4 * per + t] for t in range(per)], axis=1)
                pv = jnp.dot(jnp.exp2(sv - m_col).astype(BF16), v_aug, preferred_element_type=F32)
                out.append(pv if acc is None else acc + pv)
            return out

        n_parts = nch // per
        col_max = lambda m12: tuple(jnp.max(m, axis=-1, keepdims=True) for m in m12)
        ops = head_operands(0)
        m12 = (neg_inf, neg_inf)
        for c4 in range(n_parts):
            m12 = scores_chunk(0, c4, *ops, m12)
        m_cols = col_max(window_bias_and_max(0, m12))
        for hd in range(DIFF_HEADS):
            nxt = hd + 1
            if nxt < DIFF_HEADS:
                ops = head_operands(nxt)
            accs = [None, None]
            m12 = (neg_inf, neg_inf)
            for c4 in range(n_parts):
                if nxt < DIFF_HEADS:
                    m12 = scores_chunk(nxt, c4, *ops, m12)
                accs = pv_chunk(hd, c4, m_cols, accs)
            a1, a2 = accs
            o = a1[:, 0:LANE] * (1.0 / a1[:, LANE:2 * LANE]) - (a2[:, 0:LANE] * (1.0 / a2[:, LANE:2 * LANE])) * lam
            o_ref[0, :, hd * DIFF_V:(hd + 1) * DIFF_V] = (
                _rms(o, subln_row) * (1.0 - lambda_init)).astype(o_ref.dtype)
            if nxt < DIFF_HEADS:
                m_cols = col_max(window_bias_and_max(nxt, m12))

    @pl.when(jnp.logical_not(fits))
    def _():
        for c in range(nch):
            bkt_ref[c] = _t5_bucket(pch_ref[0, c] - pq)

        def body(hd, carry):
            q1, q2 = _diff_split_q(dq_ref[0, hd])
            tab_b = jnp.broadcast_to(tabt_ref[pl.ds(hd, 1), :] * LOG2E, (tq, LANE))
            m1 = neg_inf
            m2 = neg_inf
            for c4 in range(nch // per):
                kt = jnp.concatenate([dkt_ref[0, hd, c4 * per + t] for t in range(per)], axis=1)
                sa = jnp.dot(q1, kt, preferred_element_type=F32)
                sb = jnp.dot(q2, kt, preferred_element_type=F32)
                for t in range(per):
                    c = c4 * per + t
                    bias = gather_bias(tab_b, bkt_ref[c])
                    va = sa[:, t * LANE:(t + 1) * LANE] + bias
                    vb = sb[:, t * LANE:(t + 1) * LANE] + bias
                    sa0_ref[c] = va
                    sb0_ref[c] = vb
                    m1 = jnp.maximum(m1, va)
                    m2 = jnp.maximum(m2, vb)
            oslow_ref[hd] = _diff_finish(sa0_ref, sb0_ref, m1, m2, dv_ref.at[0, hd], lam, subln_row,
                                         lambda_init)
            return carry

        lax.fori_loop(0, DIFF_HEADS, body, 0)
        for hd in range(DIFF_HEADS):
            o_ref[0, :, hd * DIFF_V:(hd + 1) * DIFF_V] = oslow_ref[hd].astype(o_ref.dtype)


def _diff_attn(positions, rel_bias, dq, dkt, dv, pos_col, pos_tiles, tab_t, lq1, lk1, lq2, lk2, subln,
               lambda_init):
    bsz, nh, seq, _ = dq.shape
    tq = DIFF_TQ
    nch = seq // LANE
    const2 = lambda b, i, *_: (0, 0)
    once = pl.Buffered(1)
    grid_spec = pltpu.PrefetchScalarGridSpec(
        num_scalar_prefetch=2,
        grid=(bsz, seq // tq),
        in_specs=[
            pl.BlockSpec((1, nh, tq, LANE), lambda b, i, *_: (b, 0, i, 0)),
            pl.BlockSpec((1, nh, nch, LANE, LANE), lambda b, i, *_: (b, 0, 0, 0, 0), pipeline_mode=once),
            pl.BlockSpec((1, nh, nch, LANE, LANE), lambda b, i, *_: (b, 0, 0, 0, 0), pipeline_mode=once),
            pl.BlockSpec((1, tq, 1), lambda b, i, *_: (b, i, 0)),
            pl.BlockSpec((1, nch, 1, LANE), lambda b, i, *_: (b, 0, 0, 0)),
            pl.BlockSpec(tab_t.shape, const2),
            pl.BlockSpec((1, DIFF_QK), const2),
            pl.BlockSpec((1, DIFF_QK), const2),
            pl.BlockSpec((1, DIFF_QK), const2),
            pl.BlockSpec((1, DIFF_QK), const2),
            pl.BlockSpec((1, DIFF_V), const2),
        ],
        out_specs=pl.BlockSpec((1, tq, nh * DIFF_V), lambda b, i, *_: (b, i, 0)),
        scratch_shapes=[
            *([pltpu.VMEM((nch, tq, LANE), F32)] * 4),
            pltpu.VMEM((nch, tq, LANE), jnp.int32),
            pltpu.VMEM((nh, tq, LANE), F32),
        ],
    )
    return pl.pallas_call(
        partial(_diff_kernel, lambda_init=lambda_init),
        grid_spec=grid_spec,
        out_shape=jax.ShapeDtypeStruct((bsz, seq, nh * DIFF_V), BF16),
        compiler_params=_params("arbitrary", "arbitrary"),
        name="diff_attn",
    )(positions, rel_bias, dq, dkt, dv.reshape(bsz, nh, nch, LANE, LANE), pos_col, pos_tiles, tab_t,
      lq1, lk1, lq2, lk2, subln)


def _out_proj_kernel(om_ref, od_ref, wa_ref, wb_ref, x_ref, g_ref, post_ref, sh_ref, sc_ref, pre_ref,
                     x1_ref, h2_ref, il_ref):
    _, nph, rows, d = x_ref.shape
    tm = nph * rows
    merge = lambda ref: ref[0].reshape(tm, ref.shape[3])
    y = (jnp.dot(merge(om_ref), wa_ref[...], preferred_element_type=F32)
         + jnp.dot(merge(od_ref), wb_ref[...], preferred_element_type=F32))
    x1 = merge(x_ref) + g_ref[0] * _rms(y, post_ref[...])
    x1_ref[0] = x1.reshape(nph, rows, d)
    a_row = pre_ref[...] * (1.0 + sc_ref[0])
    ms = jnp.mean(x1 * x1, axis=-1, keepdims=True)
    h2 = (x1 * lax.rsqrt(ms + NORM_EPS)) * a_row + sh_ref[0]
    for j in range(il_ref.shape[0]):
        for s in range(nph):
            il_ref[j, pl.ds(s, rows, stride=nph), :] = h2[s * rows:(s + 1) * rows, j * LANE:(j + 1) * LANE]
    for j in range(il_ref.shape[0]):
        h2_ref[:, j * LANE:(j + 1) * LANE] = il_ref[j].astype(h2_ref.dtype)


def _out_proj(o_mla, o_diff, wo_a, wo_b, x2, g_a, post_w, sh_f, sc_f, pre_w, seq):
    m, d = x2.shape
    bsz = m // seq
    tm = OUTPROJ_TM
    rows = tm // ROW_IL
    tpb = seq // tm
    phases = lambda a: a.reshape(bsz, ROW_IL, seq // ROW_IL, a.shape[-1])
    ph_block = lambda w: pl.BlockSpec((1, ROW_IL, rows, w), lambda i: (i // tpb, 0, i % tpb, 0))
    bmap3 = lambda i: (i // tpb, 0, 0)
    x1, h2 = pl.pallas_call(
        _out_proj_kernel,
        grid=(m // tm,),
        in_specs=[
            ph_block(o_mla.shape[1]),
            ph_block(o_diff.shape[1]),
            _resident(wo_a.shape),
            _resident(wo_b.shape),
            ph_block(d),
            pl.BlockSpec((1, 1, d), bmap3),
            _resident((1, d)),
            pl.BlockSpec((1, 1, d), bmap3),
            pl.BlockSpec((1, 1, d), bmap3),
            _resident((1, d)),
        ],
        out_specs=(ph_block(d), pl.BlockSpec((tm, d), lambda i: (i, 0))),
        out_shape=(jax.ShapeDtypeStruct((bsz, ROW_IL, seq // ROW_IL, d), F32),
                   jax.ShapeDtypeStruct((m, d), BF16)),
        scratch_shapes=[pltpu.VMEM((d // LANE, tm, LANE), F32)],
        compiler_params=_params("arbitrary"),
        name="out_proj",
    )(phases(o_mla), phases(o_diff), wo_a, wo_b, phases(x2), g_a, post_w, sh_f, sc_f, pre_w)
    return x1.reshape(m, d), h2


def _gelu_tanh(g):
    a = -2.0 * math.sqrt(2.0 / math.pi) * LOG2E
    return g * (1.0 / (1.0 + jnp.exp2(g * (a + (a * 0.044715) * (g * g)))))


def _neighbour_rows(u, step):
    n = u.shape[0]
    row = lax.broadcasted_iota(jnp.int32, (ROW_IL, u.shape[1]), 0)
    if step == -1:
        wrap = jnp.where(row == 0, 0.0, pltpu.roll(u[n - ROW_IL:], 1, 0))
        return jnp.concatenate([wrap, u[:n - ROW_IL]], axis=0)
    wrap = jnp.where(row == ROW_IL - 1, 0.0, pltpu.roll(u[0:ROW_IL], ROW_IL - 1, 0))
    return jnp.concatenate([u[ROW_IL:], wrap], axis=0)


def _ffn_up_kernel(h_ref, wg_ref, wv_ref, cwg_ref, cwv_ref, cbg_ref, cbv_ref, o_ref):
    h = h_ref[...]

    def conv(u, cw_ref, cb_ref):
        return (cb_ref[...] + u * cw_ref[1:2, :]
                + _neighbour_rows(u, -1) * cw_ref[0:1, :] + _neighbour_rows(u, 1) * cw_ref[2:3, :])

    gate = conv(jnp.dot(h, wg_ref[...].astype(BF16), preferred_element_type=F32), cwg_ref, cbg_ref)
    val = conv(jnp.dot(h, wv_ref[...].astype(BF16), preferred_element_type=F32), cwv_ref, cbv_ref)
    o_ref[...] = (_gelu_tanh(gate) * val).astype(o_ref.dtype)


def _ffn_up(h2, w_up, conv_w, conv_b, seq):
    m, d = h2.shape
    f = w_up.shape[1] // 2
    tn = UP_TN
    nj = f // tn
    return pl.pallas_call(
        _ffn_up_kernel,
        grid=(nj, m // seq),
        in_specs=[
            pl.BlockSpec((seq, d), lambda j, i: (i, 0)),
            pl.BlockSpec((d, tn), lambda j, i: (0, j)),
            pl.BlockSpec((d, tn), lambda j, i: (0, j + nj)),
            pl.BlockSpec((3, tn), lambda j, i: (0, j)),
            pl.BlockSpec((3, tn), lambda j, i: (0, j + nj)),
            pl.BlockSpec((1, tn), lambda j, i: (0, j)),
            pl.BlockSpec((1, tn), lambda j, i: (0, j + nj)),
        ],
        out_specs=pl.BlockSpec((seq, tn), lambda j, i: (i, j)),
        out_shape=jax.ShapeDtypeStruct((m, f), BF16),
        compiler_params=_params("arbitrary", "arbitrary"),
        name="ffn_up",
    )(h2, w_up, w_up, conv_w, conv_w, conv_b, conv_b)


def _ffn_down_kernel(a_ref, w_ref, x1_ref, g_ref, post_ref, o_ref, y_ref):
    tm = a_ref.shape[0]
    y = jnp.dot(a_ref[...], w_ref[...], preferred_element_type=F32)
    yn = _rms(y, post_ref[...])
    g = g_ref[0]
    for j in range(y_ref.shape[0]):
        y_ref[j] = yn[:, j * LANE:(j + 1) * LANE]
    for s in range(ROW_IL):
        for j in range(y_ref.shape[0]):
            cs = slice(j * LANE, (j + 1) * LANE)
            rows = y_ref[j, pl.ds(s, tm // ROW_IL, stride=ROW_IL), :]
            o_ref[0, s, :, cs] = x1_ref[0, s, :, cs] + g[:, cs] * rows


def _ffn_down(act, w_down, x1, g_f, post_w, seq):
    m, f = act.shape
    d = w_down.shape[1]
    tm = DOWN_TM
    tpb = seq // tm
    bsz = m // seq
    nat_block = pl.BlockSpec((1, ROW_IL, tm // ROW_IL, d), lambda i: (i // tpb, 0, i % tpb, 0))
    out = pl.pallas_call(
        _ffn_down_kernel,
        grid=(m // tm,),
        in_specs=[
            pl.BlockSpec((tm, f), lambda i: (i, 0)),
            _resident(w_down.shape),
            nat_block,
            pl.BlockSpec((1, 1, d), lambda i: (i // tpb, 0, 0)),
            _resident((1, d)),
        ],
        out_specs=nat_block,
        out_shape=jax.ShapeDtypeStruct((bsz, ROW_IL, seq // ROW_IL, d), F32),
        scratch_shapes=[pltpu.VMEM((d // LANE, tm, LANE), F32)],
        compiler_params=_params("arbitrary"),
        name="ffn_down",
    )(act, w_down, x1.reshape(bsz, ROW_IL, seq // ROW_IL, d), g_f, post_w)
    return out.reshape(m, d)


def _rope_pair_tiles(w_rope):
    half = MLA_ROPE // 2
    t1, t2 = w_rope[:, :half], w_rope[:, half:]
    z = jnp.zeros((w_rope.shape[0], LANE - MLA_ROPE), w_rope.dtype)
    return jnp.concatenate([t1, t2, z], axis=1), jnp.concatenate([t2, t1, z], axis=1)


def _layout_w_in_kernel(w_ref, w1_ref, wt_ref):
    o_kr = MLA_Q_RANK + MLA_KV_RANK
    o_dq = o_kr + MLA_ROPE
    o_dk = o_dq + DIFF_HEADS * 2 * DIFF_QK
    o_dv = o_dk + DIFF_HEADS * 2 * DIFF_QK
    w = w_ref[...]
    w1_ref[...] = jnp.concatenate([w[:, :o_kr], w[:, o_dq:o_dk], w[:, o_dv:]], axis=1).astype(BF16)
    ra, rb = _rope_pair_tiles(w[:, o_kr:o_dq])
    wt_ref[...] = jnp.concatenate([ra, rb, w[:, o_dk:o_dv]], axis=1).T.astype(BF16)


def _layout_w_in(w_in):
    k, n = w_in.shape
    tk = LAYOUT_TK
    return pl.pallas_call(
        _layout_w_in_kernel,
        grid=(k // tk,),
        in_specs=[pl.BlockSpec((tk, n), lambda i: (i, 0))],
        out_specs=(pl.BlockSpec((tk, N_W1), lambda i: (i, 0)), pl.BlockSpec((N_WT, tk), lambda i: (0, i))),
        out_shape=(jax.ShapeDtypeStruct((k, N_W1), BF16), jax.ShapeDtypeStruct((N_WT, k), BF16)),
        compiler_params=_params("arbitrary"),
        name="layout_w_in",
    )(w_in)


def _layout_w_uq(w_uq):
    k = w_uq.shape[0]
    w3 = w_uq.reshape(k, MLA_HEADS, MLA_NOPE + MLA_ROPE)
    blocks = []
    for hd in range(MLA_HEADS):
        ra, rb = _rope_pair_tiles(w3[:, hd, MLA_NOPE:])
        blocks += [w3[:, hd, :MLA_NOPE], ra, rb]
    return jnp.concatenate(blocks, axis=1).astype(BF16)


def _layout_w_ukv(w_ukv):
    k = w_ukv.shape[0]
    w3 = w_ukv.reshape(k, MLA_HEADS, MLA_NOPE + MLA_V)
    wukt = w3[:, :, :MLA_NOPE].reshape(k, MLA_HEADS * MLA_NOPE).T.astype(BF16)
    wuv = w3[:, :, MLA_NOPE:].reshape(k, MLA_HEADS * MLA_V).astype(BF16)
    return wukt, wuv


def _rope_kernel(pos_ref, inv_ref, cr_ref, sr_ref, ct_ref, st_ref):
    ang = inv_ref[...] * pos_ref[0].astype(F32)
    c = jnp.cos(ang)
    s = jnp.sin(ang)
    zeros = jnp.zeros((LANE - MLA_ROPE, ang.shape[1]), F32)
    ct = jnp.concatenate([c, c, zeros], axis=0)
    st = jnp.concatenate([-s, s, zeros], axis=0)
    ct_ref[0] = ct
    st_ref[0] = st
    cr_ref[0] = ct.T
    sr_ref[0] = st.T


def _rope_tables(pos_row):
    bsz, _, seq = pos_row.shape
    inv = 1.0 / (ROPE_THETA ** (np.arange(0, MLA_ROPE, 2, dtype=np.float32) / MLA_ROPE))
    inv_col = jnp.asarray(inv.astype(np.float32).reshape(MLA_ROPE // 2, 1))
    row_tab = jax.ShapeDtypeStruct((bsz, seq, LANE), F32)
    col_tab = jax.ShapeDtypeStruct((bsz, LANE, seq), F32)
    return pl.pallas_call(
        _rope_kernel,
        grid=(bsz,),
        in_specs=[pl.BlockSpec((1, 1, seq), lambda b: (b, 0, 0)),
                  pl.BlockSpec(inv_col.shape, lambda b: (0, 0))],
        out_specs=(pl.BlockSpec((1, seq, LANE), lambda b: (b, 0, 0)),
                   pl.BlockSpec((1, seq, LANE), lambda b: (b, 0, 0)),
                   pl.BlockSpec((1, LANE, seq), lambda b: (b, 0, 0)),
                   pl.BlockSpec((1, LANE, seq), lambda b: (b, 0, 0))),
        out_shape=(row_tab, row_tab, col_tab, col_tab),
        compiler_params=_params("arbitrary"),
        name="rope_tables",
    )(pos_row, inv_col)


def kernel(x, c, positions, rel_bias, ada_w, ada_b, attn_pre_norm, attn_post_norm, w_in, q_norm, w_uq,
           kv_norm, w_ukv, lambda_q1, lambda_k1, lambda_q2, lambda_k2, diff_subln, w_o, ffn_pre_norm,
           ffn_post_norm, w_up, conv_w, conv_b, w_down):
    bsz, seq, d = x.shape
    depth = ada_w.shape[0]
    m = bsz * seq
    positions = positions.astype(jnp.int32)
    pos_col = positions.reshape(bsz, seq, 1)
    pos_row = positions.reshape(bsz, 1, seq)
    pos_tiles = positions.reshape(bsz, seq // LANE, 1, LANE)
    rope = _rope_tables(pos_row)
    tab_t = jnp.zeros((DIFF_HEADS, LANE), F32).at[:, :REL_BUCKETS].set(rel_bias.astype(F32).T)
    row = lambda v: v.reshape(1, -1).astype(F32)

    x2 = x.reshape(m, d)
    for l in range(depth):
        lambda_init = 0.8 - 0.6 * math.exp(-0.3 * l)
        mod = _ada_mod(c, ada_w[l], ada_b[l])
        sh_a, sc_a, g_a, sh_f, sc_f, g_f = [t.reshape(bsz, 1, d) for t in jnp.split(mod, N_MOD, axis=-1)]

        w1, wt = _layout_w_in(w_in[l])
        wuq = _layout_w_uq(w_uq[l])
        wukt, wuv = _layout_w_ukv(w_ukv[l])
        q, knt, krt, v, dq, dkt, dv = _in_proj(
            x2, rope, sh_a, sc_a, row(attn_pre_norm[l]), w1, wt, row(q_norm[l]), wuq,
            row(kv_norm[l]), wukt, wuv, bsz, seq)

        o_mla = _mla_attn(q, knt, krt, v)
        o_diff = _diff_attn(positions, rel_bias.astype(F32), dq, dkt, dv, pos_col, pos_tiles, tab_t,
                            row(lambda_q1[l]), row(lambda_k1[l]), row(lambda_q2[l]), row(lambda_k2[l]),
                            row(diff_subln[l]), lambda_init)

        wo = w_o[l].astype(BF16)
        n_mla = MLA_HEADS * MLA_V
        x1, h2 = _out_proj(o_mla.reshape(m, n_mla), o_diff.reshape(m, DIFF_HEADS * DIFF_V),
                           wo[:n_mla], wo[n_mla:], x2, g_a, row(attn_post_norm[l]), sh_f, sc_f,
                           row(ffn_pre_norm[l]), seq)

        act = _ffn_up(h2, w_up[l], conv_w[l].astype(F32), row(conv_b[l]), seq)
        x2 = _ffn_down(act, w_down[l].astype(BF16), x1, g_f, row(ffn_post_norm[l]), seq)
    return x2.reshape(bsz, seq, d)
```

```python
import math
from functools import partial

import jax
import jax.numpy as jnp
import numpy as np
from jax import lax
from jax.experimental import pallas as pl
from jax.experimental.pallas import tpu as pltpu

MLA_HEADS = 8
MLA_Q_RANK = 512
MLA_KV_RANK = 256
MLA_NOPE = 128
MLA_ROPE = 64
MLA_V = 128
MLA_SCALE = (MLA_NOPE + MLA_ROPE) ** -0.5
DIFF_HEADS = 8
DIFF_QK = 64
DIFF_V = 128
DIFF_SCALE = DIFF_QK ** -0.5
REL_BUCKETS = 32
ROPE_THETA = 10000.0
NORM_EPS = 1e-6
N_MOD = 6
LOG2E = math.log2(math.e)
T5_SATURATE = 128
F32_MANT_BITS = 23
F32_EXP_MASK = 0xFF
F32_EXP_BIAS = 127
SQRT2_MANT = 0x3504F4

LANE = 128
SUBLANE = 8
VMEM_LIMIT_BYTES = 56 * 1024 * 1024

ADA_TN = 1536
LAYOUT_TK = 256
INPROJ_TM = 512
ATTN_TQ = 512
DIFF_TQ = 256
ATTN_CK = 512
OUTPROJ_TM = 512
ROW_IL = 8
UP_TN = 512
DOWN_TM = 256

F32 = jnp.float32
BF16 = jnp.bfloat16


def _params(*semantics):
    return pltpu.CompilerParams(dimension_semantics=semantics, vmem_limit_bytes=VMEM_LIMIT_BYTES)


def _resident(shape):
    nd = len(shape)
    return pl.BlockSpec(shape, lambda *_: (0,) * nd, pipeline_mode=pl.Buffered(1))


def _rms(x, w_row):
    ms = jnp.mean(x * x, axis=-1, keepdims=True)
    return (x * lax.rsqrt(ms + NORM_EPS)) * w_row


def _nt_dot(a, b):
    return lax.dot_general(a, b, (((1,), (1,)), ((), ())), preferred_element_type=F32)


def _ada_kernel(c_ref, w_ref, b_ref, o_ref):
    c = c_ref[...]
    c_act = (c * jax.nn.sigmoid(c)).astype(BF16)
    o_ref[...] = jnp.dot(c_act, w_ref[...].astype(BF16), preferred_element_type=F32) + b_ref[...]


def _ada_mod(c, ada_w, ada_b):
    bsz, d = c.shape
    n = ada_w.shape[1]
    return pl.pallas_call(
        _ada_kernel,
        grid=(n // ADA_TN,),
        in_specs=[
            pl.BlockSpec((bsz, d), lambda j: (0, 0)),
            pl.BlockSpec((d, ADA_TN), lambda j: (0, j)),
            pl.BlockSpec((1, ADA_TN), lambda j: (0, j)),
        ],
        out_specs=pl.BlockSpec((bsz, ADA_TN), lambda j: (0, j)),
        out_shape=jax.ShapeDtypeStruct((bsz, n), F32),
        compiler_params=_params("arbitrary"),
        name="ada_mod",
    )(c, ada_w, ada_b.reshape(1, n))


N_W1 = MLA_Q_RANK + MLA_KV_RANK + 2 * DIFF_HEADS * 2 * DIFF_QK
N_WT = 2 * LANE + DIFF_HEADS * 2 * DIFF_QK
Q_HEAD_COLS = 3 * LANE


def _in_proj_kernel(x_ref, cr_ref, sr_ref, ct_ref, st_ref, sh_ref, sc_ref, pre_ref, w1_ref, wt_ref,
                    qn_ref, wuq_ref, kvn_ref, wukt_ref, wuv_ref,
                    q_out, knt_out, krt_out, v_out, dq_out, dkt_out, dv_out):
    x = x_ref[...]
    a_row = pre_ref[...] * (1.0 + sc_ref[0])
    ms = jnp.mean(x * x, axis=-1, keepdims=True)
    h = ((x * lax.rsqrt(ms + NORM_EPS)) * a_row + sh_ref[0]).astype(BF16)

    p1 = jnp.dot(h, w1_ref[...], preferred_element_type=F32)
    p2t = _nt_dot(wt_ref[...], h)

    cos_r, sin_r = cr_ref[0], sr_ref[0]
    cos_t, sin_t = ct_ref[0], st_ref[0]

    q_lat = p1[:, 0:MLA_Q_RANK]
    rq = _rms(q_lat, qn_ref[...] * (MLA_SCALE * LOG2E)).astype(BF16)
    qf = jnp.dot(rq, wuq_ref[...], preferred_element_type=F32)
    for hd in range(MLA_HEADS):
        c0 = hd * Q_HEAD_COLS
        q_out[0, hd, :, 0:LANE] = qf[:, c0:c0 + LANE].astype(BF16)
        rope = qf[:, c0 + LANE:c0 + 2 * LANE] * cos_r + qf[:, c0 + 2 * LANE:c0 + 3 * LANE] * sin_r
        q_out[0, hd, :, LANE:2 * LANE] = rope.astype(BF16)

    kv_lat = p1[:, MLA_Q_RANK:MLA_Q_RANK + MLA_KV_RANK]
    rkv = _rms(kv_lat, kvn_ref[...]).astype(BF16)
    knt = _nt_dot(wukt_ref[...], rkv)
    vv = jnp.dot(rkv, wuv_ref[...], preferred_element_type=F32)
    for hd in range(MLA_HEADS):
        knt_out[0, hd] = knt[hd * LANE:(hd + 1) * LANE, :].astype(BF16)
        v_out[0, hd] = vv[:, hd * LANE:(hd + 1) * LANE].astype(BF16)
    krt_out[0] = (p2t[0:LANE, :] * cos_t + p2t[LANE:2 * LANE, :] * sin_t).astype(BF16)

    o_dq = MLA_Q_RANK + MLA_KV_RANK
    o_dv = o_dq + DIFF_HEADS * 2 * DIFF_QK
    for hd in range(DIFF_HEADS):
        dq_out[0, hd] = (p1[:, o_dq + hd * LANE:o_dq + (hd + 1) * LANE] * (DIFF_SCALE * LOG2E)).astype(BF16)
        dv_out[0, hd] = p1[:, o_dv + hd * LANE:o_dv + (hd + 1) * LANE].astype(BF16)
        dkt = p2t[2 * LANE + hd * LANE:2 * LANE + (hd + 1) * LANE, :].astype(BF16)
        for t in range(dkt_out.shape[2]):
            dkt_out[0, hd, t] = dkt[:, t * LANE:(t + 1) * LANE]


def _in_proj(x2, rope, sh_a, sc_a, pre_w, w1, wt, q_norm, wuq, kv_norm, wukt, wuv, bsz, seq):
    tm = INPROJ_TM
    tpb = seq // tm
    d = x2.shape[1]
    bmap3 = lambda i: (i // tpb, 0, 0)
    out_shapes = (
        jax.ShapeDtypeStruct((bsz, MLA_HEADS, seq, 2 * LANE), BF16),
        jax.ShapeDtypeStruct((bsz, MLA_HEADS, LANE, seq), BF16),
        jax.ShapeDtypeStruct((bsz, LANE, seq), BF16),
        jax.ShapeDtypeStruct((bsz, MLA_HEADS, seq, LANE), BF16),
        jax.ShapeDtypeStruct((bsz, DIFF_HEADS, seq, LANE), BF16),
        jax.ShapeDtypeStruct((bsz, DIFF_HEADS, seq // LANE, LANE, LANE), BF16),
        jax.ShapeDtypeStruct((bsz, DIFF_HEADS, seq, LANE), BF16),
    )
    row_blk = lambda nh, w: pl.BlockSpec((1, nh, tm, w), lambda i: (i // tpb, 0, i % tpb, 0))
    col_blk = lambda nh: pl.BlockSpec((1, nh, LANE, tm), lambda i: (i // tpb, 0, 0, i % tpb))
    return pl.pallas_call(
        _in_proj_kernel,
        grid=(bsz * tpb,),
        in_specs=[
            pl.BlockSpec((tm, d), lambda i: (i, 0)),
            pl.BlockSpec((1, tm, LANE), lambda i: (i // tpb, i % tpb, 0)),
            pl.BlockSpec((1, tm, LANE), lambda i: (i // tpb, i % tpb, 0)),
            pl.BlockSpec((1, LANE, tm), lambda i: (i // tpb, 0, i % tpb)),
            pl.BlockSpec((1, LANE, tm), lambda i: (i // tpb, 0, i % tpb)),
            pl.BlockSpec((1, 1, d), bmap3),
            pl.BlockSpec((1, 1, d), bmap3),
            _resident((1, d)),
            _resident(w1.shape),
            _resident(wt.shape),
            _resident(q_norm.shape),
            _resident(wuq.shape),
            _resident(kv_norm.shape),
            _resident(wukt.shape),
            _resident(wuv.shape),
        ],
        out_specs=(
            row_blk(MLA_HEADS, 2 * LANE),
            col_blk(MLA_HEADS),
            pl.BlockSpec((1, LANE, tm), lambda i: (i // tpb, 0, i % tpb)),
            row_blk(MLA_HEADS, LANE),
            row_blk(DIFF_HEADS, LANE),
            pl.BlockSpec((1, DIFF_HEADS, tm // LANE, LANE, LANE), lambda i: (i // tpb, 0, i % tpb, 0, 0)),
            row_blk(DIFF_HEADS, LANE),
        ),
        out_shape=out_shapes,
        compiler_params=_params("arbitrary"),
        name="in_proj",
    )(x2, *rope, sh_a, sc_a, pre_w, w1, wt, q_norm, wuq, kv_norm, wukt, wuv)


def _lane_tile_reduce(x, op):
    out = x[:, 0:LANE]
    for t in range(1, x.shape[1] // LANE):
        out = op(out, x[:, t * LANE:(t + 1) * LANE])
    return out


def _pv_normalised(chunks, v_ref):
    ones = jnp.ones((ATTN_CK, LANE), BF16)
    acc = None
    for c, p in enumerate(chunks):
        v_aug = jnp.concatenate([v_ref[c * ATTN_CK:(c + 1) * ATTN_CK, :], ones], axis=1)
        pv = jnp.dot(p.astype(BF16), v_aug, preferred_element_type=F32)
        acc = pv if acc is None else acc + pv
    return acc[:, 0:LANE] * (1.0 / acc[:, LANE:2 * LANE])


def _mla_kernel(q_ref, knt_ref, krt_ref, v_ref, o_ref, s_ref):
    seq = s_ref.shape[2]
    n_chunks = seq // ATTN_CK
    ones = jnp.ones((ATTN_CK, LANE), BF16)

    def scores_chunk(hd, c, m_part):
        sl = slice(c * ATTN_CK, (c + 1) * ATTN_CK)
        kt = jnp.concatenate([knt_ref[0, hd, :, sl], krt_ref[0, :, sl]], axis=0)
        s = jnp.dot(q_ref[0, hd], kt, preferred_element_type=F32)
        s_ref[hd % 2, :, sl] = s
        mp = _lane_tile_reduce(s, jnp.maximum)
        return mp if m_part is None else jnp.maximum(m_part, mp)

    def pv_chunk(hd, c, m_col, acc):
        sl = slice(c * ATTN_CK, (c + 1) * ATTN_CK)
        p = jnp.exp2(s_ref[hd % 2, :, sl] - m_col).astype(BF16)
        pv = jnp.dot(p, jnp.concatenate([v_ref[0, hd, sl, :], ones], axis=1), preferred_element_type=F32)
        return pv if acc is None else acc + pv

    m_part = None
    for c in range(n_chunks):
        m_part = scores_chunk(0, c, m_part)
    for hd in range(MLA_HEADS):
        m_col = jnp.max(m_part, axis=-1, keepdims=True)
        m_part = None
        acc = None
        for c in range(n_chunks):
            if hd + 1 < MLA_HEADS:
                m_part = scores_chunk(hd + 1, c, m_part)
            acc = pv_chunk(hd, c, m_col, acc)
        o = acc[:, 0:LANE] * (1.0 / acc[:, LANE:2 * LANE])
        o_ref[0, :, hd * MLA_V:(hd + 1) * MLA_V] = o.astype(o_ref.dtype)


def _mla_attn(q, knt, krt, v):
    bsz, nh, seq, _ = q.shape
    tq = ATTN_TQ
    return pl.pallas_call(
        _mla_kernel,
        grid=(bsz, seq // tq),
        in_specs=[
            pl.BlockSpec((1, nh, tq, 2 * LANE), lambda b, i: (b, 0, i, 0)),
            pl.BlockSpec((1, nh, LANE, seq), lambda b, i: (b, 0, 0, 0)),
            pl.BlockSpec((1, LANE, seq), lambda b, i: (b, 0, 0)),
            pl.BlockSpec((1, nh, seq, LANE), lambda b, i: (b, 0, 0, 0)),
        ],
        out_specs=pl.BlockSpec((1, tq, nh * MLA_V), lambda b, i: (b, i, 0)),
        out_shape=jax.ShapeDtypeStruct((bsz, seq, nh * MLA_V), BF16),
        scratch_shapes=[pltpu.VMEM((2, tq, seq), F32)],
        compiler_params=_params("arbitrary", "arbitrary"),
        name="mla_attn",
    )(q, knt, krt, v)


def _t5_bucket(rel):
    n = jnp.abs(rel)
    bits = lax.bitcast_convert_type(n.astype(F32), jnp.int32)
    twice_exp = jnp.bitwise_and(lax.shift_right_logical(bits, F32_MANT_BITS - 1), 2 * F32_EXP_MASK)
    upper_half = jnp.where(jnp.bitwise_and(bits, (1 << F32_MANT_BITS) - 1) >= SQRT2_MANT, 1, 0)
    large = jnp.minimum(twice_exp - (2 * F32_EXP_BIAS - 2) + upper_half, REL_BUCKETS // 2 - 1)
    return jnp.where(rel > 0, REL_BUCKETS // 2, 0) + jnp.where(n < REL_BUCKETS // 4, n, large)


def _diff_split_q(q):
    lane = lax.broadcasted_iota(jnp.int32, q.shape, 1)
    zero = jnp.zeros_like(q)
    return jnp.where(lane < DIFF_QK, q, zero), jnp.where(lane >= DIFF_QK, q, zero)


class _TiledRows:
    def __init__(self, ref):
        self.ref = ref

    def __getitem__(self, idx):
        rows, _ = idx
        tiles = range(rows.start // LANE, rows.stop // LANE)
        return jnp.concatenate([self.ref[t] for t in tiles], axis=0)


def _softmax_pv_tiles(s_ref, m_col, v_ref):
    per = ATTN_CK // LANE
    chunks = []
    for c in range(s_ref.shape[0] // per):
        sv = jnp.concatenate([s_ref[c * per + t] for t in range(per)], axis=1)
        chunks.append(jnp.exp2(sv - m_col))
    return _pv_normalised(chunks, _TiledRows(v_ref))


def _diff_finish(s1_ref, s2_ref, m1, m2, v_ref, lam, subln_row, lambda_init):
    m1c = jnp.max(m1, axis=-1, keepdims=True)
    m2c = jnp.max(m2, axis=-1, keepdims=True)
    o = _softmax_pv_tiles(s1_ref, m1c, v_ref) - _softmax_pv_tiles(s2_ref, m2c, v_ref) * lam
    return _rms(o, subln_row) * (1.0 - lambda_init)


def _bias_rows(t_neg, t_pos, ws, nw, seq):
    shape = (2 * SUBLANE, seq)
    tile = lax.broadcasted_iota(jnp.int32, shape, 1) // LANE
    cv = jnp.where(tile < nw, 0.0, jnp.where(tile + ws >= seq // LANE, t_neg, t_pos))
    hi = cv.astype(BF16).astype(F32)
    row = lax.broadcasted_iota(jnp.int32, shape, 0)
    return jnp.where(row == 0, hi, jnp.where(row == 1, cv - hi, 0.0)).astype(BF16)


def _diff_kernel(pos_sm, tab_sm, dq_ref, dkt_ref, dv_ref, pcol_ref, pch_ref, tabt_ref,
                 lq1_ref, lk1_ref, lq2_ref, lk2_ref, subln_ref, o_ref,
                 sa0_ref, sb0_ref, sa1_ref, sb1_ref, bkt_ref, oslow_ref, *, lambda_init):
    b = pl.program_id(0)
    qi = pl.program_id(1)
    nch, tq, _ = sa0_ref.shape
    nw = tq // LANE + 2
    per = ATTN_CK // LANE

    q_lo = pos_sm[b, qi * tq]
    q_hi = pos_sm[b, qi * tq + tq - 1]
    n_neg = jnp.int32(0)
    n_pos = jnp.int32(0)
    for c in range(nch):
        n_neg = n_neg + (pos_sm[b, c * LANE + LANE - 1] - q_lo <= -T5_SATURATE).astype(jnp.int32)
        n_pos = n_pos + (pos_sm[b, c * LANE] - q_hi >= T5_SATURATE).astype(jnp.int32)
    fits = (nch - n_neg - n_pos) <= nw
    ws = jnp.minimum(n_neg, nch - nw)

    pq = pcol_ref[0]
    lam = (jnp.exp(jnp.sum(lq1_ref[...] * lk1_ref[...], axis=-1, keepdims=True))
           - jnp.exp(jnp.sum(lq2_ref[...] * lk2_ref[...], axis=-1, keepdims=True)) + lambda_init)
    subln_row = subln_ref[...]
    neg_inf = jnp.full((tq, LANE), -jnp.inf, F32)
    slots = ((sa0_ref, sb0_ref), (sa1_ref, sb1_ref))
    pick2 = jnp.where(lax.broadcasted_iota(jnp.int32, (tq, LANE), 1) < 2, 1.0, 0.0).astype(BF16)
    kt_pad = jnp.zeros((LANE - 2 * SUBLANE, ATTN_CK), BF16)

    def gather_bias(tab_b, bkt):
        return jnp.take_along_axis(tab_b, bkt, axis=1, mode="promise_in_bounds")

    @pl.when(fits)
    def _():
        for j in range(nw):
            bkt_ref[j] = _t5_bucket(pch_ref[0, ws + j] - pq)
        ones = jnp.ones((ATTN_CK, LANE), BF16)

        def head_operands(hd):
            t_neg = tab_sm[REL_BUCKETS // 2 - 1, hd] * LOG2E
            t_pos = tab_sm[REL_BUCKETS - 1, hd] * LOG2E
            q1, q2 = _diff_split_q(dq_ref[0, hd])
            return (jnp.concatenate([q1, pick2], axis=1), jnp.concatenate([q2, pick2], axis=1),
                    _bias_rows(t_neg, t_pos, ws, nw, nch * LANE))

        tile_at = [jnp.where(ws + j >= nch, ws + j - nch, ws + j) for j in range(nch)]

        def scores_chunk(hd, c4, q1, q2, rows, m12):
            s1_ref, s2_ref = slots[hd % len(slots)]
            m1, m2 = m12
            sl = slice(c4 * ATTN_CK, (c4 + 1) * ATTN_CK)
            k_tiles = jnp.concatenate([dkt_ref[0, hd, tile_at[c4 * per + t]] for t in range(per)], axis=1)
            kt = jnp.concatenate([k_tiles, rows[:, sl], kt_pad], axis=0)
            sa = jnp.dot(q1, kt, preferred_element_type=F32)
            sb = jnp.dot(q2, kt, preferred_element_type=F32)
            for t in range(per):
                j = c4 * per + t
                va = sa[:, t * LANE:(t + 1) * LANE]
                vb = sb[:, t * LANE:(t + 1) * LANE]
                s1_ref[j] = va
                s2_ref[j] = vb
                if j >= nw:
                    m1 = jnp.maximum(m1, va)
                    m2 = jnp.maximum(m2, vb)
            return m1, m2

        def window_bias_and_max(hd, m12):
            s1_ref, s2_ref = slots[hd % len(slots)]
            m1, m2 = m12
            tab_b = jnp.broadcast_to(tabt_ref[hd:hd + 1, :] * LOG2E, (tq, LANE))
            for j in range(nw):
                bias = gather_bias(tab_b, bkt_ref[j])
                va = s1_ref[j] + bias
                vb = s2_ref[j] + bias
                s1_ref[j] = va
                s2_ref[j] = vb
                m1 = jnp.maximum(m1, va)
                m2 = jnp.maximum(m2, vb)
            return m1, m2

        def pv_chunk(hd, c4, m_cols, accs):
            v_rows = jnp.concatenate([dv_ref[0, hd, tile_at[c4 * per + t]] for t in range(per)], axis=0)
            v_aug = jnp.concatenate([v_rows, ones], axis=1)
            out = []
            for s_ref, m_col, acc in zip(slots[hd % len(slots)], m_cols, accs):
                sv = jnp.concatenate([s_ref[c4 * per + t] for t in range(per)], axis=1)
                pv = jnp.dot(jnp.exp2(sv - m_col).astype(BF16), v_aug, preferred_element_type=F32)
                out.append(pv if acc is None else acc + pv)
            return out

        n_parts = nch // per
        col_max = lambda m12: tuple(jnp.max(m, axis=-1, keepdims=True) for m in m12)
        ops = head_operands(0)
        m12 = (neg_inf, neg_inf)
        for c4 in range(n_parts):
            m12 = scores_chunk(0, c4, *ops, m12)
        m_cols = col_max(window_bias_and_max(0, m12))
        for hd in range(DIFF_HEADS):
            nxt = hd + 1
            if nxt < DIFF_HEADS:
                ops = head_operands(nxt)
            accs = [None, None]
            m12 = (neg_inf, neg_inf)
            for c4 in range(n_parts):
                if nxt < DIFF_HEADS:
                    m12 = scores_chunk(nxt, c4, *ops, m12)
                accs = pv_chunk(hd, c4, m_cols, accs)
            a1, a2 = accs
            o = a1[:, 0:LANE] * (1.0 / a1[:, LANE:2 * LANE]) - (a2[:, 0:LANE] * (1.0 / a2[:, LANE:2 * LANE])) * lam
            o_ref[0, :, hd * DIFF_V:(hd + 1) * DIFF_V] = (
                _rms(o, subln_row) * (1.0 - lambda_init)).astype(o_ref.dtype)
            if nxt < DIFF_HEADS:
                m_cols = col_max(window_bias_and_max(nxt, m12))

    @pl.when(jnp.logical_not(fits))
    def _():
        for c in range(nch):
            bkt_ref[c] = _t5_bucket(pch_ref[0, c] - pq)

        def body(hd, carry):
            q1, q2 = _diff_split_q(dq_ref[0, hd])
            tab_b = jnp.broadcast_to(tabt_ref[pl.ds(hd, 1), :] * LOG2E, (tq, LANE))
            m1 = neg_inf
            m2 = neg_inf
            for c4 in range(nch // per):
                kt = jnp.concatenate([dkt_ref[0, hd, c4 * per + t] for t in range(per)], axis=1)
                sa = jnp.dot(q1, kt, preferred_element_type=F32)
                sb = jnp.dot(q2, kt, preferred_element_type=F32)
                for t in range(per):
                    c = c4 * per + t
                    bias = gather_bias(tab_b, bkt_ref[c])
                    va = sa[:, t * LANE:(t + 1) * LANE] + bias
                    vb = sb[:, t * LANE:(t + 1) * LANE] + bias
                    sa0_ref[c] = va
                    sb0_ref[c] = vb
                    m1 = jnp.maximum(m1, va)
                    m2 = jnp.maximum(m2, vb)
            oslow_ref[hd] = _diff_finish(sa0_ref, sb0_ref, m1, m2, dv_ref.at[0, hd], lam, subln_row,
                                         lambda_init)
            return carry

        lax.fori_loop(0, DIFF_HEADS, body, 0)
        for hd in range(DIFF_HEADS):
            o_ref[0, :, hd * DIFF_V:(hd + 1) * DIFF_V] = oslow_ref[hd].astype(o_ref.dtype)


def _diff_attn(positions, rel_bias, dq, dkt, dv, pos_col, pos_tiles, tab_t, lq1, lk1, lq2, lk2, subln,
               lambda_init):
    bsz, nh, seq, _ = dq.shape
    tq = DIFF_TQ
    nch = seq // LANE
    const2 = lambda b, i, *_: (0, 0)
    once = pl.Buffered(1)
    grid_spec = pltpu.PrefetchScalarGridSpec(
        num_scalar_prefetch=2,
        grid=(bsz, seq // tq),
        in_specs=[
            pl.BlockSpec((1, nh, tq, LANE), lambda b, i, *_: (b, 0, i, 0)),
            pl.BlockSpec((1, nh, nch, LANE, LANE), lambda b, i, *_: (b, 0, 0, 0, 0), pipeline_mode=once),
            pl.BlockSpec((1, nh, nch, LANE, LANE), lambda b, i, *_: (b, 0, 0, 0, 0), pipeline_mode=once),
            pl.BlockSpec((1, tq, 1), lambda b, i, *_: (b, i, 0)),
            pl.BlockSpec((1, nch, 1, LANE), lambda b, i, *_: (b, 0, 0, 0)),
            pl.BlockSpec(tab_t.shape, const2),
            pl.BlockSpec((1, DIFF_QK), const2),
            pl.BlockSpec((1, DIFF_QK), const2),
            pl.BlockSpec((1, DIFF_QK), const2),
            pl.BlockSpec((1, DIFF_QK), const2),
            pl.BlockSpec((1, DIFF_V), const2),
        ],
        out_specs=pl.BlockSpec((1, tq, nh * DIFF_V), lambda b, i, *_: (b, i, 0)),
        scratch_shapes=[
            *([pltpu.VMEM((nch, tq, LANE), F32)] * 4),
            pltpu.VMEM((nch, tq, LANE), jnp.int32),
            pltpu.VMEM((nh, tq, LANE), F32),
        ],
    )
    return pl.pallas_call(
        partial(_diff_kernel, lambda_init=lambda_init),
        grid_spec=grid_spec,
        out_shape=jax.ShapeDtypeStruct((bsz, seq, nh * DIFF_V), BF16),
        compiler_params=_params("arbitrary", "arbitrary"),
        name="diff_attn",
    )(positions, rel_bias, dq, dkt, dv.reshape(bsz, nh, nch, LANE, LANE), pos_col, pos_tiles, tab_t,
      lq1, lk1, lq2, lk2, subln)


def _out_proj_kernel(om_ref, od_ref, wa_ref, wb_ref, x_ref, g_ref, post_ref, sh_ref, sc_ref, pre_ref,
                     x1_ref, h2_ref, il_ref):
    _, nph, rows, d = x_ref.shape
    tm = nph * rows
    merge = lambda ref: ref[0].reshape(tm, ref.shape[3])
    y = (jnp.dot(merge(om_ref), wa_ref[...], preferred_element_type=F32)
         + jnp.dot(merge(od_ref), wb_ref[...], preferred_element_type=F32))
    x1 = merge(x_ref) + g_ref[0] * _rms(y, post_ref[...])
    x1_ref[0] = x1.reshape(nph, rows, d)
    a_row = pre_ref[...] * (1.0 + sc_ref[0])
    ms = jnp.mean(x1 * x1, axis=-1, keepdims=True)
    h2 = (x1 * lax.rsqrt(ms + NORM_EPS)) * a_row + sh_ref[0]
    for j in range(il_ref.shape[0]):
        for s in range(nph):
            il_ref[j, pl.ds(s, rows, stride=nph), :] = h2[s * rows:(s + 1) * rows, j * LANE:(j + 1) * LANE]
    for j in range(il_ref.shape[0]):
        h2_ref[:, j * LANE:(j + 1) * LANE] = il_ref[j].astype(h2_ref.dtype)


def _out_proj(o_mla, o_diff, wo_a, wo_b, x2, g_a, post_w, sh_f, sc_f, pre_w, seq):
    m, d = x2.shape
    bsz = m // seq
    tm = OUTPROJ_TM
    rows = tm // ROW_IL
    tpb = seq // tm
    phases = lambda a: a.reshape(bsz, ROW_IL, seq // ROW_IL, a.shape[-1])
    ph_block = lambda w: pl.BlockSpec((1, ROW_IL, rows, w), lambda i: (i // tpb, 0, i % tpb, 0))
    bmap3 = lambda i: (i // tpb, 0, 0)
    x1, h2 = pl.pallas_call(
        _out_proj_kernel,
        grid=(m // tm,),
        in_specs=[
            ph_block(o_mla.shape[1]),
            ph_block(o_diff.shape[1]),
            _resident(wo_a.shape),
            _resident(wo_b.shape),
            ph_block(d),
            pl.BlockSpec((1, 1, d), bmap3),
            _resident((1, d)),
            pl.BlockSpec((1, 1, d), bmap3),
            pl.BlockSpec((1, 1, d), bmap3),
            _resident((1, d)),
        ],
        out_specs=(ph_block(d), pl.BlockSpec((tm, d), lambda i: (i, 0))),
        out_shape=(jax.ShapeDtypeStruct((bsz, ROW_IL, seq // ROW_IL, d), F32),
                   jax.ShapeDtypeStruct((m, d), BF16)),
        scratch_shapes=[pltpu.VMEM((d // LANE, tm, LANE), F32)],
        compiler_params=_params("arbitrary"),
        name="out_proj",
    )(phases(o_mla), phases(o_diff), wo_a, wo_b, phases(x2), g_a, post_w, sh_f, sc_f, pre_w)
    return x1.reshape(m, d), h2


def _gelu_tanh(g):
    a = -2.0 * math.sqrt(2.0 / math.pi) * LOG2E
    return g * (1.0 / (1.0 + jnp.exp2(g * (a + (a * 0.044715) * (g * g)))))


def _neighbour_rows(u, step):
    n = u.shape[0]
    row = lax.broadcasted_iota(jnp.int32, (ROW_IL, u.shape[1]), 0)
    if step == -1:
        wrap = jnp.where(row == 0, 0.0, pltpu.roll(u[n - ROW_IL:], 1, 0))
        return jnp.concatenate([wrap, u[:n - ROW_IL]], axis=0)
    wrap = jnp.where(row == ROW_IL - 1, 0.0, pltpu.roll(u[0:ROW_IL], ROW_IL - 1, 0))
    return jnp.concatenate([u[ROW_IL:], wrap], axis=0)


def _ffn_up_kernel(h_ref, wg_ref, wv_ref, cwg_ref, cwv_ref, cbg_ref, cbv_ref, o_ref):
    h = h_ref[...]

    def conv(u, cw_ref, cb_ref):
        return (cb_ref[...] + u * cw_ref[1:2, :]
                + _neighbour_rows(u, -1) * cw_ref[0:1, :] + _neighbour_rows(u, 1) * cw_ref[2:3, :])

    gate = conv(jnp.dot(h, wg_ref[...].astype(BF16), preferred_element_type=F32), cwg_ref, cbg_ref)
    val = conv(jnp.dot(h, wv_ref[...].astype(BF16), preferred_element_type=F32), cwv_ref, cbv_ref)
    o_ref[...] = (_gelu_tanh(gate) * val).astype(o_ref.dtype)


def _ffn_up(h2, w_up, conv_w, conv_b, seq):
    m, d = h2.shape
    f = w_up.shape[1] // 2
    tn = UP_TN
    nj = f // tn
    return pl.pallas_call(
        _ffn_up_kernel,
        grid=(nj, m // seq),
        in_specs=[
            pl.BlockSpec((seq, d), lambda j, i: (i, 0)),
            pl.BlockSpec((d, tn), lambda j, i: (0, j)),
            pl.BlockSpec((d, tn), lambda j, i: (0, j + nj)),
            pl.BlockSpec((3, tn), lambda j, i: (0, j)),
            pl.BlockSpec((3, tn), lambda j, i: (0, j + nj)),
            pl.BlockSpec((1, tn), lambda j, i: (0, j)),
            pl.BlockSpec((1, tn), lambda j, i: (0, j + nj)),
        ],
        out_specs=pl.BlockSpec((seq, tn), lambda j, i: (i, j)),
        out_shape=jax.ShapeDtypeStruct((m, f), BF16),
        compiler_params=_params("arbitrary", "arbitrary"),
        name="ffn_up",
    )(h2, w_up, w_up, conv_w, conv_w, conv_b, conv_b)


def _ffn_down_kernel(a_ref, w_ref, x1_ref, g_ref, post_ref, o_ref, y_ref):
    tm = a_ref.shape[0]
    y = jnp.dot(a_ref[...], w_ref[...], preferred_element_type=F32)
    yn = _rms(y, post_ref[...])
    g = g_ref[0]
    for j in range(y_ref.shape[0]):
        y_ref[j] = yn[:, j * LANE:(j + 1) * LANE]
    for s in range(ROW_IL):
        for j in range(y_ref.shape[0]):
            cs = slice(j * LANE, (j + 1) * LANE)
            rows = y_ref[j, pl.ds(s, tm // ROW_IL, stride=ROW_IL), :]
            o_ref[0, s, :, cs] = x1_ref[0, s, :, cs] + g[:, cs] * rows


def _ffn_down(act, w_down, x1, g_f, post_w, seq):
    m, f = act.shape
    d = w_down.shape[1]
    tm = DOWN_TM
    tpb = seq // tm
    bsz = m // seq
    nat_block = pl.BlockSpec((1, ROW_IL, tm // ROW_IL, d), lambda i: (i // tpb, 0, i % tpb, 0))
    out = pl.pallas_call(
        _ffn_down_kernel,
        grid=(m // tm,),
        in_specs=[
            pl.BlockSpec((tm, f), lambda i: (i, 0)),
            _resident(w_down.shape),
            nat_block,
            pl.BlockSpec((1, 1, d), lambda i: (i // tpb, 0, 0)),
            _resident((1, d)),
        ],
        out_specs=nat_block,
        out_shape=jax.ShapeDtypeStruct((bsz, ROW_IL, seq // ROW_IL, d), F32),
        scratch_shapes=[pltpu.VMEM((d // LANE, tm, LANE), F32)],
        compiler_params=_params("arbitrary"),
        name="ffn_down",
    )(act, w_down, x1.reshape(bsz, ROW_IL, seq // ROW_IL, d), g_f, post_w)
    return out.reshape(m, d)


def _rope_pair_tiles(w_rope):
    half = MLA_ROPE // 2
    t1, t2 = w_rope[:, :half], w_rope[:, half:]
    z = jnp.zeros((w_rope.shape[0], LANE - MLA_ROPE), w_rope.dtype)
    return jnp.concatenate([t1, t2, z], axis=1), jnp.concatenate([t2, t1, z], axis=1)


def _layout_w_in_kernel(w_ref, w1_ref, wt_ref):
    o_kr = MLA_Q_RANK + MLA_KV_RANK
    o_dq = o_kr + MLA_ROPE
    o_dk = o_dq + DIFF_HEADS * 2 * DIFF_QK
    o_dv = o_dk + DIFF_HEADS * 2 * DIFF_QK
    w = w_ref[0]
    w1_ref[...] = jnp.concatenate([w[:, :o_kr], w[:, o_dq:o_dk], w[:, o_dv:]], axis=1).astype(BF16)
    ra, rb = _rope_pair_tiles(w[:, o_kr:o_dq])
    wt_ref[...] = jnp.concatenate([ra, rb, w[:, o_dk:o_dv]], axis=1).T.astype(BF16)


def _layout_w_in(w_in, layer):
    _, k, n = w_in.shape
    tk = LAYOUT_TK
    return pl.pallas_call(
        _layout_w_in_kernel,
        grid=(k // tk,),
        in_specs=[pl.BlockSpec((1, tk, n), lambda i: (layer, i, 0))],
        out_specs=(pl.BlockSpec((tk, N_W1), lambda i: (i, 0)), pl.BlockSpec((N_WT, tk), lambda i: (0, i))),
        out_shape=(jax.ShapeDtypeStruct((k, N_W1), BF16), jax.ShapeDtypeStruct((N_WT, k), BF16)),
        compiler_params=_params("arbitrary"),
        name="layout_w_in",
    )(w_in)


def _layout_w_uq(w_uq):
    k = w_uq.shape[0]
    w3 = w_uq.reshape(k, MLA_HEADS, MLA_NOPE + MLA_ROPE)
    blocks = []
    for hd in range(MLA_HEADS):
        ra, rb = _rope_pair_tiles(w3[:, hd, MLA_NOPE:])
        blocks += [w3[:, hd, :MLA_NOPE], ra, rb]
    return jnp.concatenate(blocks, axis=1).astype(BF16)


def _layout_w_ukv(w_ukv):
    k = w_ukv.shape[0]
    w3 = w_ukv.reshape(k, MLA_HEADS, MLA_NOPE + MLA_V)
    wukt = w3[:, :, :MLA_NOPE].reshape(k, MLA_HEADS * MLA_NOPE).T.astype(BF16)
    wuv = w3[:, :, MLA_NOPE:].reshape(k, MLA_HEADS * MLA_V).astype(BF16)
    return wukt, wuv


def _rope_kernel(pos_ref, inv_ref, cr_ref, sr_ref, ct_ref, st_ref):
    ang = inv_ref[...] * pos_ref[0].astype(F32)
    c = jnp.cos(ang)
    s = jnp.sin(ang)
    zeros = jnp.zeros((LANE - MLA_ROPE, ang.shape[1]), F32)
    ct = jnp.concatenate([c, c, zeros], axis=0)
    st = jnp.concatenate([-s, s, zeros], axis=0)
    ct_ref[0] = ct
    st_ref[0] = st
    cr_ref[0] = ct.T
    sr_ref[0] = st.T


def _rope_tables(pos_row):
    bsz, _, seq = pos_row.shape
    inv = 1.0 / (ROPE_THETA ** (np.arange(0, MLA_ROPE, 2, dtype=np.float32) / MLA_ROPE))
    inv_col = jnp.asarray(inv.astype(np.float32).reshape(MLA_ROPE // 2, 1))
    row_tab = jax.ShapeDtypeStruct((bsz, seq, LANE), F32)
    col_tab = jax.ShapeDtypeStruct((bsz, LANE, seq), F32)
    return pl.pallas_call(
        _rope_kernel,
        grid=(bsz,),
        in_specs=[pl.BlockSpec((1, 1, seq), lambda b: (b, 0, 0)),
                  pl.BlockSpec(inv_col.shape, lambda b: (0, 0))],
        out_specs=(pl.BlockSpec((1, seq, LANE), lambda b: (b, 0, 0)),
                   pl.BlockSpec((1, seq, LANE), lambda b: (b, 0, 0)),
                   pl.BlockSpec((1, LANE, seq), lambda b: (b, 0, 0)),
                   pl.BlockSpec((1, LANE, seq), lambda b: (b, 0, 0))),
        out_shape=(row_tab, row_tab, col_tab, col_tab),
        compiler_params=_params("arbitrary"),
        name="rope_tables",
    )(pos_row, inv_col)


def kernel(x, c, positions, rel_bias, ada_w, ada_b, attn_pre_norm, attn_post_norm, w_in, q_norm, w_uq,
           kv_norm, w_ukv, lambda_q1, lambda_k1, lambda_q2, lambda_k2, diff_subln, w_o, ffn_pre_norm,
           ffn_post_norm, w_up, conv_w, conv_b, w_down):
    bsz, seq, d = x.shape
    depth = ada_w.shape[0]
    m = bsz * seq
    positions = positions.astype(jnp.int32)
    pos_col = positions.reshape(bsz, seq, 1)
    pos_row = positions.reshape(bsz, 1, seq)
    pos_tiles = positions.reshape(bsz, seq // LANE, 1, LANE)
    rope = _rope_tables(pos_row)
    tab_t = jnp.zeros((DIFF_HEADS, LANE), F32).at[:, :REL_BUCKETS].set(rel_bias.astype(F32).T)
    row = lambda v: v.reshape(1, -1).astype(F32)

    x2 = x.reshape(m, d)
    for l in range(depth):
        lambda_init = 0.8 - 0.6 * math.exp(-0.3 * l)
        mod = _ada_mod(c, ada_w[l], ada_b[l])
        sh_a, sc_a, g_a, sh_f, sc_f, g_f = [t.reshape(bsz, 1, d) for t in jnp.split(mod, N_MOD, axis=-1)]

        w1, wt = _layout_w_in(w_in, l)
        wuq = _layout_w_uq(w_uq[l])
        wukt, wuv = _layout_w_ukv(w_ukv[l])
        q, knt, krt, v, dq, dkt, dv = _in_proj(
            x2, rope, sh_a, sc_a, row(attn_pre_norm[l]), w1, wt, row(q_norm[l]), wuq,
            row(kv_norm[l]), wukt, wuv, bsz, seq)

        o_mla = _mla_attn(q, knt, krt, v)
        o_diff = _diff_attn(positions, rel_bias.astype(F32), dq, dkt, dv, pos_col, pos_tiles, tab_t,
                            row(lambda_q1[l]), row(lambda_k1[l]), row(lambda_q2[l]), row(lambda_k2[l]),
                            row(diff_subln[l]), lambda_init)

        wo = w_o[l].astype(BF16)
        n_mla = MLA_HEADS * MLA_V
        x1, h2 = _out_proj(o_mla.reshape(m, n_mla), o_diff.reshape(m, DIFF_HEADS * DIFF_V),
                           wo[:n_mla], wo[n_mla:], x2, g_a, row(attn_post_norm[l]), sh_f, sc_f,
                           row(ffn_pre_norm[l]), seq)

        act = _ffn_up(h2, w_up[l], conv_w[l].astype(F32), row(conv_b[l]), seq)
        x2 = _ffn_down(act, w_down[l].astype(BF16), x1, g_f, row(ffn_post_norm[l]), seq)
    return x2.reshape(bsz, seq, d)
```
